```python
import math
import jax, jax.numpy as jnp
from jax import lax
import numpy as np

D_MODEL = 1024
BATCH = 4
SEQ = 4096
DEPTH = 4
DEC_BATCH = 128
DEC_SEQ = 8
PAST_LEN = 2048
PAGE_SIZE = 128

GDN_HEADS = 4
GDN_DK = 128
GDN_DV = 128
GDN_CONV = 4
GDN_CHUNK = 64
GDN_QK = GDN_HEADS * GDN_DK
GDN_V = GDN_HEADS * GDN_DV
GDN_CONV_CH = 2 * GDN_QK + GDN_V
HGRN_HEADS = 4
HGRN_DK = 128
HGRN_DV = 128
HGRN_CHUNK = 64
HGRN_K = HGRN_HEADS * HGRN_DK
HGRN_V = HGRN_HEADS * HGRN_DV
DIFF_HEADS = 4
DIFF_DH = 64
DIFF_DV = 2 * DIFF_DH
DIFF_QK = DIFF_HEADS * 2 * DIFF_DH
DIFF_V = DIFF_HEADS * DIFF_DV
ROPE_THETA = 500000.0
ROPE_DIMS = DIFF_DH // 4
ATTN_QBLOCK = 128
N_BRANCH = 3
BRANCH_W = 512
D_FF = 2816
EPS = 1e-6
SPLIT_SIZES = (GDN_CONV_CH, GDN_V, GDN_HEADS, GDN_HEADS, HGRN_K, HGRN_K, HGRN_V, HGRN_V, DIFF_QK, DIFF_QK, DIFF_V, N_BRANCH * D_MODEL)
IN_COLS = GDN_CONV_CH + GDN_V + 2 * GDN_HEADS + 2 * HGRN_K + 2 * HGRN_V + 2 * DIFF_QK + DIFF_V + N_BRANCH * D_MODEL

kernel_name = 'hybrid_gdn_hgrn2_diffattn_step'


def rmsnorm(x, w):
    xf = x.astype(jnp.float32)
    y = xf * lax.rsqrt(jnp.mean(xf * xf, axis=-1, keepdims=True) + EPS)
    return (y * w.astype(jnp.float32)).astype(x.dtype)


def l2norm(x):
    xf = x.astype(jnp.float32)
    return xf * lax.rsqrt(jnp.sum(xf * xf, axis=-1, keepdims=True) + EPS)


def swiglu(x, w_in, w_out):
    gu = x @ w_in
    return (jax.nn.silu(gu[..., :D_FF]) * gu[..., D_FF:]) @ w_out


def split_cols(z, sizes):
    outs, off = [], 0
    for s in sizes:
        outs.append(z[..., off:off + s])
        off += s
    return outs


def causal_conv(x, state, w):
    t = x.shape[1]
    xp = jnp.concatenate([state.astype(x.dtype), x], axis=1)
    y = xp[:, 0:t] * w[0]
    for j in range(1, GDN_CONV):
        y = y + xp[:, j:j + t] * w[j]
    return jax.nn.silu(y), xp[:, t:]


def to_chunks(a, c):
    b, t, h = a.shape[:3]
    a = a.reshape((b, t // c, c, h) + a.shape[3:])
    return jnp.moveaxis(a, (1, 3), (0, 2))


def from_chunks(o):
    n, b, h, c = o.shape[:4]
    return jnp.moveaxis(o, (0, 2), (1, 3)).reshape((b, n * c, h) + o.shape[4:])


def gdn_chunked(q, k, v, beta, g, s0):
    t = q.shape[1]
    c = math.gcd(GDN_CHUNK, t)
    dv = v.shape[-1]
    incl = jnp.tril(jnp.ones((c, c), dtype=bool))
    strict = jnp.tril(jnp.ones((c, c), dtype=bool), -1)
    eye = jnp.eye(c, dtype=jnp.float32)

    def step(S, inp):
        qc, kc, vc, bc, gc = inp
        gcum = jnp.cumsum(gc, axis=-1)
        decay = jnp.exp(jnp.where(incl, gcum[..., :, None] - gcum[..., None, :], -jnp.inf))
        kb = kc * bc[..., None]
        lmat = jnp.where(strict, jnp.einsum('bhid,bhjd->bhij', kb, kc) * decay, 0.0)
        rhs = jnp.concatenate([vc * bc[..., None], kb * jnp.exp(gcum)[..., None]], axis=-1)
        sol = lax.linalg.triangular_solve(lmat + eye, rhs, left_side=True, lower=True, unit_diagonal=True)
        u, w = sol[..., :dv], sol[..., dv:]
        v_new = u - jnp.einsum('bhcd,bhdv->bhcv', w, S)
        attn = jnp.einsum('bhid,bhjd->bhij', qc, kc) * decay
        o = (jnp.einsum('bhcd,bhdv->bhcv', qc * jnp.exp(gcum)[..., None], S)
             + jnp.einsum('bhij,bhjv->bhiv', attn, v_new))
        glast = gcum[..., -1:]
        S = (S * jnp.exp(glast)[..., None]
             + jnp.einsum('bhcd,bhcv->bhdv', kc * jnp.exp(glast - gcum)[..., None], v_new))
        return S, o

    s_fin, o = lax.scan(step, s0, (to_chunks(q, c), to_chunks(k, c), to_chunks(v, c),
                                   to_chunks(beta, c), to_chunks(g, c)))
    return from_chunks(o), s_fin


def hgrn2_chunked(q, k, v, logf, s0):
    t = q.shape[1]
    c = math.gcd(HGRN_CHUNK, t)
    incl = jnp.tril(jnp.ones((c, c), dtype=bool))[:, :, None]

    def step(S, inp):
        qc, kc, vc, lf = inp
        b = jnp.cumsum(lf, axis=-2)
        dec = jnp.exp(jnp.where(incl, b[..., :, None, :] - b[..., None, :, :], -jnp.inf))
        a = jnp.einsum('bhtd,bhsd,bhtsd->bhts', qc, kc, dec)
        o = (jnp.einsum('bhtd,bhdv->bhtv', qc * jnp.exp(b), S)
             + jnp.einsum('bhts,bhsv->bhtv', a, vc))
        bl = b[..., -1:, :]
        S = (S * jnp.exp(bl[..., 0, :])[..., None]
             + jnp.einsum('bhsd,bhsv->bhdv', kc * jnp.exp(bl - b), vc))
        return S, o

    s_fin, o = lax.scan(step, s0, (to_chunks(q, c), to_chunks(k, c), to_chunks(v, c), to_chunks(logf, c)))
    return from_chunks(o), s_fin


def rope_partial(x, pos):
    half = ROPE_DIMS // 2
    inv = jnp.power(jnp.float32(ROPE_THETA), -jnp.arange(half, dtype=jnp.float32) / half)
    ang = pos.astype(jnp.float32)[:, None] * inv
    cos = jnp.cos(ang)[:, None, None, :]
    sin = jnp.sin(ang)[:, None, None, :]
    xf = x.astype(jnp.float32)
    x1, x2, rest = xf[..., :half], xf[..., half:ROPE_DIMS], xf[..., ROPE_DIMS:]
    return jnp.concatenate([x1 * cos - x2 * sin, x2 * cos + x1 * sin, rest], axis=-1).astype(x.dtype)


def diff_attention(q, k, v, q_pos, k_pos, lam):
    b, t, h = q.shape[:3]
    qb = math.gcd(ATTN_QBLOCK, t)
    nb = t // qb
    scale = DIFF_DH ** -0.5
    q_blocks = jnp.moveaxis(q.reshape((b, nb, qb) + q.shape[2:]), 1, 0)
    p_blocks = q_pos.reshape(nb, qb)

    def block(args):
        qi, pi = args
        s = jnp.einsum('bqhmd,bkhmd->bhmqk', qi, k).astype(jnp.float32) * scale
        s = jnp.where(k_pos[None, :] <= pi[:, None], s, -jnp.inf)
        p = jax.nn.softmax(s, axis=-1)
        pdiff = p[:, :, 0] - lam * p[:, :, 1]
        return jnp.einsum('bhqk,bkhv->bqhv', pdiff.astype(v.dtype), v)

    o = lax.map(block, (q_blocks, p_blocks))
    return jnp.moveaxis(o, 0, 1).reshape(b, t, h, v.shape[-1])


def token_mixer(h, lp, lam_init, lb, conv_state, gdn_state, hgrn_state, past_k, past_v, pos0):
    f32 = jnp.float32
    b, t, _ = h.shape
    (qkv_a, z_a, beta_raw, a_raw, q_b, f_b, i_b, g_b,
     q_c, k_c, v_c, gate_raw) = split_cols(h @ lp['w_in'], SPLIT_SIZES)

    qkv_a, new_conv = causal_conv(qkv_a, conv_state, lp['gdn_conv_w'])
    qa = l2norm(qkv_a[..., :GDN_QK].reshape(b, t, GDN_HEADS, GDN_DK)) * (GDN_DK ** -0.5)
    ka = l2norm(qkv_a[..., GDN_QK:2 * GDN_QK].reshape(b, t, GDN_HEADS, GDN_DK))
    va = qkv_a[..., 2 * GDN_QK:].reshape(b, t, GDN_HEADS, GDN_DV).astype(f32)
    beta = jax.nn.sigmoid(beta_raw.astype(f32))
    g = -jnp.exp(lp['gdn_a_log'].astype(f32)) * jax.nn.softplus(a_raw.astype(f32) + lp['gdn_dt_bias'].astype(f32))
    o_a, new_gdn = gdn_chunked(qa, ka, va, beta, g, gdn_state.astype(f32))
    o_a = rmsnorm(o_a.astype(h.dtype), lp['gdn_norm_w']) * jax.nn.silu(z_a.reshape(b, t, GDN_HEADS, GDN_DV))

    qh = jax.nn.silu(q_b.astype(f32)).reshape(b, t, HGRN_HEADS, HGRN_DK) * (HGRN_DK ** -0.5)
    zf = f_b.astype(f32)
    logf = jnp.logaddexp(jnp.log(lb), jnp.log1p(-lb) + jax.nn.log_sigmoid(zf))
    kh = (1.0 - lb) * jax.nn.sigmoid(-zf)
    o_b, new_hgrn = hgrn2_chunked(qh, kh.reshape(b, t, HGRN_HEADS, HGRN_DK),
                                  i_b.astype(f32).reshape(b, t, HGRN_HEADS, HGRN_DV),
                                  logf.reshape(b, t, HGRN_HEADS, HGRN_DK), hgrn_state.astype(f32))
    o_b = rmsnorm(o_b.astype(h.dtype), lp['hgrn_norm_w']) * jax.nn.sigmoid(g_b.reshape(b, t, HGRN_HEADS, HGRN_DV))

    q_pos = pos0 + jnp.arange(t)
    qc = rope_partial(q_c.reshape(b, t, DIFF_HEADS, 2, DIFF_DH), q_pos)
    kc = rope_partial(k_c.reshape(b, t, DIFF_HEADS, 2, DIFF_DH), q_pos)
    vc = v_c.reshape(b, t, DIFF_HEADS, DIFF_DV)
    if past_k is None:
        k_all, v_all = kc, vc
    else:
        k_all = jnp.concatenate([past_k.reshape(b, -1, DIFF_HEADS, 2, DIFF_DH).astype(kc.dtype), kc], axis=1)
        v_all = jnp.concatenate([past_v.astype(vc.dtype), vc], axis=1)
    k_pos = jnp.arange(k_all.shape[1])
    lmb = lp['diff_lambda'].astype(f32)
    lam = jnp.exp(jnp.sum(lmb[0] * lmb[1])) - jnp.exp(jnp.sum(lmb[2] * lmb[3])) + lam_init
    o_c = diff_attention(qc, k_all, v_all, q_pos, k_pos, lam)
    o_c = rmsnorm(o_c, lp['diff_norm_w']) * (1.0 - lam_init)

    branches = jnp.stack([o_a.reshape(b, t, BRANCH_W), o_b.reshape(b, t, BRANCH_W),
                          o_c.reshape(b, t, BRANCH_W)], axis=2)
    ys = jnp.einsum('btnc,ncd->btnd', branches, lp['w_branch'])
    gates = jax.nn.sigmoid(gate_raw.astype(f32)).astype(h.dtype).reshape(b, t, N_BRANCH, D_MODEL)
    y = jnp.sum(gates * ys, axis=2) @ lp['w_out']
    return y, (new_gdn, new_conv, new_hgrn, kc.reshape(b, t, DIFF_HEADS, 2 * DIFF_DH), vc)


def decoder_layer(x, lp, lam_init, lb, conv_state, gdn_state, hgrn_state, past_k, past_v, pos0):
    f1 = swiglu(rmsnorm(x, lp['ffn1_norm_pre']), lp['ffn1_w_in'], lp['ffn1_w_out'])
    x = x + 0.5 * rmsnorm(f1, lp['ffn1_norm_post'])
    y, new_state = token_mixer(rmsnorm(x, lp['mix_norm_pre']), lp, lam_init, lb, conv_state,
                               gdn_state, hgrn_state, past_k, past_v, pos0)
    x = x + rmsnorm(y, lp['mix_norm_post'])
    f2 = swiglu(rmsnorm(x, lp['ffn2_norm_pre']), lp['ffn2_w_in'], lp['ffn2_w_out'])
    x = x + 0.5 * rmsnorm(f2, lp['ffn2_norm_post'])
    return x, new_state


def setup_inputs(seed: int = 0) -> dict:
    key = jax.random.key(seed)
    ks = iter(jax.random.split(key, 48))
    f32 = jnp.float32

    def nrm(shape, scale):
        return jax.random.normal(next(ks), shape, f32) * scale

    def gain(shape):
        return 1.0 + nrm(shape, 0.05)

    n_pages = PAST_LEN // PAGE_SIZE
    n_pool = (DEC_BATCH * n_pages * 5) // 4
    page_table = jax.random.permutation(next(ks), n_pool)[:DEC_BATCH * n_pages]
    page_table = page_table.reshape(DEC_BATCH, n_pages).astype(jnp.int32)
    dt = jnp.exp(jax.random.uniform(next(ks), (DEPTH, GDN_HEADS), f32, math.log(1e-3), math.log(1e-1)))
    return {
        'x_prompt': nrm((BATCH, SEQ, D_MODEL), 1.0),
        'x_sample': nrm((DEC_BATCH, DEC_SEQ, D_MODEL), 1.0),
        'state_gdn': nrm((DEPTH, DEC_BATCH, GDN_HEADS, GDN_DK, GDN_DV), 0.1),
        'state_gdn_conv': nrm((DEPTH, DEC_BATCH, GDN_CONV - 1, GDN_CONV_CH), 1.0),
        'state_hgrn': nrm((DEPTH, DEC_BATCH, HGRN_HEADS, HGRN_DK, HGRN_DV), 0.5),
        'cache_k': nrm((DEPTH, n_pool, PAGE_SIZE, DIFF_HEADS, 2 * DIFF_DH), 1.0),
        'cache_v': nrm((DEPTH, n_pool, PAGE_SIZE, DIFF_HEADS, DIFF_DV), 1.0),
        'page_table': page_table,
        'ffn1_norm_pre': gain((DEPTH, D_MODEL)),
        'ffn1_norm_post': gain((DEPTH, D_MODEL)),
        'ffn1_w_in': nrm((DEPTH, D_MODEL, 2 * D_FF), D_MODEL ** -0.5),
        'ffn1_w_out': nrm((DEPTH, D_FF, D_MODEL), D_FF ** -0.5),
        'mix_norm_pre': gain((DEPTH, D_MODEL)),
        'mix_norm_post': gain((DEPTH, D_MODEL)),
        'w_in': nrm((DEPTH, D_MODEL, IN_COLS), D_MODEL ** -0.5),
        'gdn_conv_w': nrm((DEPTH, GDN_CONV, GDN_CONV_CH), GDN_CONV ** -0.5),
        'gdn_a_log': jnp.log(jax.random.uniform(next(ks), (DEPTH, GDN_HEADS), f32, 1.0, 16.0)),
        'gdn_dt_bias': dt + jnp.log(-jnp.expm1(-dt)),
        'gdn_norm_w': gain((DEPTH, GDN_DV)),
        'hgrn_lb_raw': nrm((DEPTH, HGRN_K), 0.1),
        'hgrn_norm_w': gain((DEPTH, HGRN_DV)),
        'diff_lambda': nrm((DEPTH, 4, DIFF_DH), 0.1),
        'diff_norm_w': gain((DEPTH, DIFF_DV)),
        'w_branch': nrm((DEPTH, N_BRANCH, BRANCH_W, D_MODEL), BRANCH_W ** -0.5),
        'w_out': nrm((DEPTH, D_MODEL, D_MODEL), D_MODEL ** -0.5),
        'ffn2_norm_pre': gain((DEPTH, D_MODEL)),
        'ffn2_norm_post': gain((DEPTH, D_MODEL)),
        'ffn2_w_in': nrm((DEPTH, D_MODEL, 2 * D_FF), D_MODEL ** -0.5),
        'ffn2_w_out': nrm((DEPTH, D_FF, D_MODEL), D_FF ** -0.5),
    }


def reference(x_prompt, x_sample, state_gdn, state_gdn_conv, state_hgrn, cache_k, cache_v, page_table,
              ffn1_norm_pre, ffn1_norm_post, ffn1_w_in, ffn1_w_out, mix_norm_pre, mix_norm_post, w_in,
              gdn_conv_w, gdn_a_log, gdn_dt_bias, gdn_norm_w, hgrn_lb_raw, hgrn_norm_w, diff_lambda,
              diff_norm_w, w_branch, w_out, ffn2_norm_pre, ffn2_norm_post, ffn2_w_in, ffn2_w_out):
    f32 = jnp.float32
    bp = x_prompt.shape[0]
    db = x_sample.shape[0]
    past_len = page_table.shape[1] * cache_k.shape[2]
    lb_cum = jnp.cumsum(jax.nn.softmax(hgrn_lb_raw.astype(f32), axis=0), axis=0)
    lb_all = lb_cum - lb_cum[0:1]

    xp, xs = x_prompt, x_sample
    p_gdn, p_conv, p_hgrn, p_k, p_v = [], [], [], [], []
    s_gdn, s_conv, s_hgrn, s_k, s_v = [], [], [], [], []
    for l in range(DEPTH):
        lp = dict(ffn1_norm_pre=ffn1_norm_pre[l], ffn1_norm_post=ffn1_norm_post[l],
                  ffn1_w_in=ffn1_w_in[l], ffn1_w_out=ffn1_w_out[l],
                  mix_norm_pre=mix_norm_pre[l], mix_norm_post=mix_norm_post[l], w_in=w_in[l],
                  gdn_conv_w=gdn_conv_w[l], gdn_a_log=gdn_a_log[l], gdn_dt_bias=gdn_dt_bias[l],
                  gdn_norm_w=gdn_norm_w[l], hgrn_norm_w=hgrn_norm_w[l], diff_lambda=diff_lambda[l],
                  diff_norm_w=diff_norm_w[l], w_branch=w_branch[l], w_out=w_out[l],
                  ffn2_norm_pre=ffn2_norm_pre[l], ffn2_norm_post=ffn2_norm_post[l],
                  ffn2_w_in=ffn2_w_in[l], ffn2_w_out=ffn2_w_out[l])
        lam_init = 0.8 - 0.6 * math.exp(-0.3 * l)
        lb = lb_all[l]
        xp, st = decoder_layer(
            xp, lp, lam_init, lb,
            jnp.zeros((bp, GDN_CONV - 1, GDN_CONV_CH), x_prompt.dtype),
            jnp.zeros((bp, GDN_HEADS, GDN_DK, GDN_DV), f32),
            jnp.zeros((bp, HGRN_HEADS, HGRN_DK, HGRN_DV), f32),
            None, None, 0)
        p_gdn.append(st[0].astype(state_gdn.dtype))
        p_conv.append(st[1].astype(state_gdn_conv.dtype))
        p_hgrn.append(st[2].astype(state_hgrn.dtype))
        p_k.append(st[3].astype(cache_k.dtype))
        p_v.append(st[4].astype(cache_v.dtype))
        past_k = cache_k[l][page_table].reshape(db, past_len, DIFF_HEADS, 2 * DIFF_DH)
        past_v = cache_v[l][page_table].reshape(db, past_len, DIFF_HEADS, DIFF_DV)
        xs, ss = decoder_layer(xs, lp, lam_init, lb, state_gdn_conv[l], state_gdn[l], state_hgrn[l],
                               past_k, past_v, past_len)
        s_gdn.append(ss[0].astype(state_gdn.dtype))
        s_conv.append(ss[1].astype(state_gdn_conv.dtype))
        s_hgrn.append(ss[2].astype(state_hgrn.dtype))
        s_k.append(ss[3].astype(cache_k.dtype))
        s_v.append(ss[4].astype(cache_v.dtype))

    return (xp, xs,
            jnp.stack(p_gdn), jnp.stack(p_conv), jnp.stack(p_hgrn), jnp.stack(p_k), jnp.stack(p_v),
            jnp.stack(s_gdn), jnp.stack(s_conv), jnp.stack(s_hgrn), jnp.stack(s_k), jnp.stack(s_v))
```

```python
import functools
import math

import jax
import jax.numpy as jnp
from jax import lax
from jax.experimental import pallas as pl
from jax.experimental.pallas import tpu as pltpu

F32 = jnp.float32
BF16 = jnp.bfloat16
EPS = 1e-6
HI = lax.Precision.HIGHEST

VMEM_LIMIT_BYTES = 56 * 1024 * 1024


def _cparams(sem):
    return pltpu.CompilerParams(dimension_semantics=sem, vmem_limit_bytes=VMEM_LIMIT_BYTES)


def _rms(x, w):
    return x * lax.rsqrt(jnp.mean(x * x, axis=-1, keepdims=True) + EPS) * w


def _sigmoid(x):
    return 1.0 / (1.0 + jnp.exp(-x))


def _silu(x):
    return x * _sigmoid(x)


def _softplus(x):
    return jnp.maximum(x, 0.0) + jnp.log1p(jnp.exp(-jnp.abs(x)))


def _ffn_kernel(x_ref, pre_ref, wg_ref, wu_ref, wo_ref, post_ref, o_ref, h_scr, acc_scr):
    j = pl.program_id(1)

    @pl.when(j == 0)
    def _():
        h_scr[...] = _rms(x_ref[...], pre_ref[...]).astype(BF16)
        acc_scr[...] = jnp.zeros_like(acc_scr)

    h = h_scr[...]
    g = jnp.dot(h, wg_ref[...], preferred_element_type=F32)
    u = jnp.dot(h, wu_ref[...], preferred_element_type=F32)
    a = (_silu(g) * u).astype(BF16)
    acc_scr[...] += jnp.dot(a, wo_ref[...], preferred_element_type=F32)

    @pl.when(j == pl.num_programs(1) - 1)
    def _():
        o_ref[...] = x_ref[...] + 0.5 * _rms(acc_scr[...], post_ref[...])


def ffn(x, pre_w, w_in_bf, w_out_bf, post_w, *, tm, tf):
    n, d = x.shape
    dff = w_out_bf.shape[0]
    nj = dff // tf
    return pl.pallas_call(
        _ffn_kernel,
        grid=(n // tm, nj),
        in_specs=[
            pl.BlockSpec((tm, d), lambda i, j: (i, 0)),
            pl.BlockSpec((1, d), lambda i, j: (0, 0)),
            pl.BlockSpec((d, tf), lambda i, j: (0, j)),
            pl.BlockSpec((d, tf), lambda i, j: (0, j + nj)),
            pl.BlockSpec((tf, d), lambda i, j: (j, 0)),
            pl.BlockSpec((1, d), lambda i, j: (0, 0)),
        ],
        out_specs=pl.BlockSpec((tm, d), lambda i, j: (i, 0)),
        out_shape=jax.ShapeDtypeStruct((n, d), F32),
        scratch_shapes=[pltpu.VMEM((tm, d), BF16), pltpu.VMEM((tm, d), F32)],
        compiler_params=_cparams(("parallel", "arbitrary")),
        name="ffn",
    )(x, pre_w.reshape(1, d), w_in_bf, w_in_bf, w_out_bf, post_w.reshape(1, d))


def _proj_kernel(x_ref, pre_ref, w_ref, ws_ref, z_ref, zs_ref, h_scr):
    j = pl.program_id(1)

    @pl.when(j == 0)
    def _():
        h = _rms(x_ref[...], pre_ref[...]).astype(BF16)
        h_scr[...] = h
        zs_ref[...] = jnp.dot(h, ws_ref[...], preferred_element_type=F32)

    z_ref[...] = jnp.dot(h_scr[...], w_ref[...], preferred_element_type=F32)


def proj(x, pre_w, w_main_bf, w_small_bf, *, tm, tn):
    n, d = x.shape
    cols = w_main_bf.shape[1]
    sc = w_small_bf.shape[1]
    return pl.pallas_call(
        _proj_kernel,
        grid=(n // tm, cols // tn),
        in_specs=[
            pl.BlockSpec((tm, d), lambda i, j: (i, 0)),
            pl.BlockSpec((1, d), lambda i, j: (0, 0)),
            pl.BlockSpec((d, tn), lambda i, j: (0, j)),
            pl.BlockSpec((d, sc), lambda i, j: (0, 0)),
        ],
        out_specs=[
            pl.BlockSpec((tm, tn), lambda i, j: (i, j)),
            pl.BlockSpec((tm, sc), lambda i, j: (i, 0)),
        ],
        out_shape=[jax.ShapeDtypeStruct((n, cols), F32), jax.ShapeDtypeStruct((n, sc), F32)],
        scratch_shapes=[pltpu.VMEM((tm, d), BF16)],
        compiler_params=_cparams(("parallel", "arbitrary")),
        name="proj",
    )(x, pre_w.reshape(1, d), w_main_bf, w_small_bf)


def _dot(a, b, precision=None):
    return jnp.dot(a, b, preferred_element_type=F32, precision=precision)


def _dot_nt(a, b, precision=None):
    return lax.dot_general(a, b, (((1,), (1,)), ((), ())), preferred_element_type=F32, precision=precision)


def _dot_tn(a, b, precision=None):
    return lax.dot_general(a, b, (((0,), (0,)), ((), ())), preferred_element_type=F32, precision=precision)


def _iota2(shape, dim):
    return lax.broadcasted_iota(jnp.int32, shape, dim)


def _tri_inverse(lmat, c):
    ri = _iota2((c, c), 0)
    ci = _iota2((c, c), 1)
    eye = jnp.where(ri == ci, 1.0, 0.0).astype(F32)
    nb = min(16, c)
    sh = int(math.log2(nb))
    lb = jnp.where((ri >> sh) == (ci >> sh), lmat, 0.0)
    t = eye - lb
    p = lb
    k = 2
    while k < nb:
        p = _dot(p, p, HI)
        t = _dot(t, eye + p, HI)
        k *= 2
    blk = nb
    while blk < c:
        s1 = int(math.log2(blk))
        off = jnp.where(((ri >> (s1 + 1)) == (ci >> (s1 + 1))) & ((ri >> s1) != (ci >> s1)), lmat, 0.0)
        t = t - _dot(_dot(t, off, HI), t, HI)
        blk *= 2
    return t


def _gdn_kernel(qkv_ref, za_ref, zs_ref, cs_ref, s0_ref, cw_ref, alog_ref, dtb_ref, nw_ref,
                o_ref, sout_ref, cout_ref, xbuf, s_scr, *, c, heads, dk, dv):
    n = pl.program_id(1)
    nlast = pl.num_programs(1) - 1
    nqk = heads * dk

    @pl.when(n == 0)
    def _():
        xbuf[5:8, :] = cs_ref[0]
        s_scr[...] = s0_ref[0]

    xbuf[8:8 + c, :] = qkv_ref[0]
    y = xbuf[5:5 + c, :] * cw_ref[0:1, :]
    for j in range(1, 4):
        y = y + xbuf[5 + j:5 + j + c, :] * cw_ref[j:j + 1, :]
    y = _silu(y)

    @pl.when(n == nlast)
    def _():
        cout_ref[0] = xbuf[5 + c:8 + c, :]

    xbuf[5:8, :] = xbuf[5 + c:8 + c, :]

    zs = zs_ref[0]
    beta_all = _sigmoid(zs)
    g_all = -jnp.exp(alog_ref[...]) * _softplus(zs + dtb_ref[...])
    ri = _iota2((c, c), 0)
    ci = _iota2((c, c), 1)
    incl = ri >= ci
    strict = ri > ci
    tril = jnp.where(incl, 1.0, 0.0).astype(F32)
    upper = jnp.where(ri > ci, 1.0, 0.0).astype(F32)
    gcum_all = _dot(tril, g_all, HI)
    nw = nw_ref[...]
    za = za_ref[0]

    for h in range(heads):
        g_col = g_all[:, heads + h:heads + h + 1]
        gc = gcum_all[:, heads + h:heads + h + 1]
        beta = beta_all[:, h:h + 1]
        dmat = _dot(tril, g_col * upper, HI)
        decay = jnp.where(incl, jnp.exp(dmat), 0.0)
        q = y[:, h * dk:(h + 1) * dk]
        k = y[:, nqk + h * dk:nqk + (h + 1) * dk]
        v = y[:, 2 * nqk + h * dv:2 * nqk + (h + 1) * dv]
        q = q * lax.rsqrt(jnp.sum(q * q, axis=-1, keepdims=True) + EPS) * (dk ** -0.5)
        k = k * lax.rsqrt(jnp.sum(k * k, axis=-1, keepdims=True) + EPS)
        kb = k * beta
        k16 = k.astype(BF16)
        lmat = jnp.where(strict, _dot_nt(kb.astype(BF16), k16) * decay, 0.0)
        tinv = _tri_inverse(lmat, c)
        eg = jnp.exp(gc)
        rhs = jnp.concatenate([v * beta, kb * eg], axis=-1)
        sol = _dot(tinv, rhs, HI)
        u = sol[:, :dv]
        w = sol[:, dv:]
        s = s_scr[h]
        s16 = s.astype(BF16)
        v_new = u - _dot(w.astype(BF16), s16)
        vn16 = v_new.astype(BF16)
        attn = _dot_nt(q.astype(BF16), k16) * decay
        o = _dot((q * eg).astype(BF16), s16) + _dot(attn.astype(BF16), vn16)
        gl = gc[c - 1:c, :]
        kdec = k * jnp.exp(gl - gc)
        s_scr[h] = s * jnp.exp(gl) + _dot_tn(kdec.astype(BF16), vn16)
        o_ref[0, :, h * dv:(h + 1) * dv] = _rms(o, nw) * _silu(za[:, h * dv:(h + 1) * dv])

    @pl.when(n == nlast)
    def _():
        sout_ref[0] = s_scr[...]


def gdn(z3, zs3, conv_state, state, conv_w, a_log, dt_bias, norm_w, *, c, qkv_blk, za_blk):
    b, t, _ = z3.shape
    _, heads, dk, dv = state.shape
    cch = conv_state.shape[-1]
    sc = zs3.shape[-1]
    pad = jnp.zeros((1, sc), F32)
    alog_p = lax.dynamic_update_slice(pad, a_log.reshape(1, heads).astype(F32), (0, heads))
    dtb_p = lax.dynamic_update_slice(pad, dt_bias.reshape(1, heads).astype(F32), (0, heads))
    kern = functools.partial(_gdn_kernel, c=c, heads=heads, dk=dk, dv=dv)
    return pl.pallas_call(
        kern,
        grid=(b, t // c),
        in_specs=[
            pl.BlockSpec((1, c, cch), lambda i, n: (i, n, qkv_blk)),
            pl.BlockSpec((1, c, heads * dv), lambda i, n: (i, n, za_blk)),
            pl.BlockSpec((1, c, sc), lambda i, n: (i, n, 0)),
            pl.BlockSpec((1, 3, cch), lambda i, n: (i, 0, 0)),
            pl.BlockSpec((1, heads, dk, dv), lambda i, n: (i, 0, 0, 0)),
            pl.BlockSpec((4, cch), lambda i, n: (0, 0)),
            pl.BlockSpec((1, sc), lambda i, n: (0, 0)),
            pl.BlockSpec((1, sc), lambda i, n: (0, 0)),
            pl.BlockSpec((1, dv), lambda i, n: (0, 0)),
        ],
        out_specs=[
            pl.BlockSpec((1, c, heads * dv), lambda i, n: (i, n, 0)),
            pl.BlockSpec((1, heads, dk, dv), lambda i, n: (i, 0, 0, 0)),
            pl.BlockSpec((1, 3, cch), lambda i, n: (i, 0, 0)),
        ],
        out_shape=[
            jax.ShapeDtypeStruct((b, t, heads * dv), F32),
            jax.ShapeDtypeStruct((b, heads, dk, dv), F32),
            jax.ShapeDtypeStruct((b, 3, cch), F32),
        ],
        scratch_shapes=[pltpu.VMEM((8 + c, cch), F32), pltpu.VMEM((heads, dk, dv), F32)],
        compiler_params=_cparams(("parallel", "arbitrary")),
        name="gdn",
    )(z3, z3, zs3, conv_state, state, conv_w, alog_p, dtb_p, norm_w.reshape(1, dv))


def _hgrn_kernel(q_ref, f_ref, i_ref, g_ref, lbraw_ref, s0_ref, nw_ref, o_ref, sout_ref,
                 st_scr, kbuf, bbuf, vbuf, *, c, sb, heads, dk, dv, layer):
    n = pl.program_id(1)
    nlast = pl.num_programs(1) - 1

    @pl.when(n == 0)
    def _():
        for h in range(heads):
            st_scr[h] = s0_ref[0, h].T
        kbuf[0:sb, :] = jnp.zeros((sb, heads * dk), F32)
        bbuf[0:sb, :] = jnp.zeros((sb, heads * dk), F32)
        vbuf[0:sb, :] = jnp.zeros((sb, heads * dv), F32)

    raw = lbraw_ref[...]
    e = jnp.exp(raw - jnp.max(raw, axis=0, keepdims=True))
    sm = e / jnp.sum(e, axis=0, keepdims=True)
    lb = jnp.zeros((1, heads * dk), F32)
    for l in range(1, layer + 1):
        lb = lb + sm[l:l + 1, :]

    zf = f_ref[0]
    logsig = jnp.minimum(zf, 0.0) - jnp.log1p(jnp.exp(-jnp.abs(zf)))
    la = jnp.log(lb)
    lbb = jnp.log1p(-lb) + logsig
    logf = jnp.maximum(la, lbb) + jnp.log1p(jnp.exp(-jnp.abs(la - lbb)))
    kh = (1.0 - lb) * _sigmoid(-zf)
    qh = _silu(q_ref[0]) * (dk ** -0.5)
    vv = i_ref[0]
    gate = _sigmoid(g_ref[0])

    ri = _iota2((c, c), 0)
    ci = _iota2((c, c), 1)
    tril = jnp.where(ri >= ci, 1.0, 0.0).astype(F32)
    bcum = _dot(tril, logf, HI)
    kbuf[sb:sb + c, :] = kh
    bbuf[sb:sb + c, :] = bcum
    vbuf[sb:sb + c, :] = vv
    row = _iota2((c, dk), 0)
    rmod = row & (sb - 1)
    nsb = c // sb
    nw = nw_ref[...]

    for h in range(heads):
        ks = slice(h * dk, (h + 1) * dk)
        vs = slice(h * dv, (h + 1) * dv)
        q = qh[:, ks]
        k = kh[:, ks]
        b = bcum[:, ks]
        v = vv[:, vs]
        st = st_scr[h]
        o = _dot_nt((q * jnp.exp(b)).astype(BF16), st.astype(BF16))
        for d in range(sb):
            ksh = kbuf[sb - d:sb - d + c, ks]
            bsh = bbuf[sb - d:sb - d + c, ks]
            vsh = vbuf[sb - d:sb - d + c, vs]
            m = jnp.where(rmod >= d, q * ksh * jnp.exp(jnp.minimum(b - bsh, 0.0)), 0.0)
            o = o + jnp.sum(m, axis=-1, keepdims=True) * vsh
        if nsb > 1:
            qs, kk = [], []
            for jb in range(nsb - 1):
                r = b[(jb + 1) * sb - 1:(jb + 1) * sb, :]
                qs.append(jnp.where(row >= (jb + 1) * sb, q * jnp.exp(jnp.minimum(b - r, 0.0)), 0.0))
                kk.append(jnp.where((row >= jb * sb) & (row < (jb + 1) * sb),
                                    k * jnp.exp(jnp.minimum(r - b, 0.0)), 0.0))
            amat = _dot_nt(jnp.concatenate(qs, axis=-1), jnp.concatenate(kk, axis=-1), HI)
            o = o + _dot(amat.astype(BF16), v.astype(BF16))
        bl = b[c - 1:c, :]
        kdec = k * jnp.exp(bl - b)
        st_scr[h] = st * jnp.exp(bl) + _dot_tn(v.astype(BF16), kdec.astype(BF16))
        o_ref[0, :, vs] = _rms(o, nw) * gate[:, vs]

    @pl.when(n == nlast)
    def _():
        for h in range(heads):
            sout_ref[0, h] = st_scr[h].T


def hgrn(z3, lb_raw, state, norm_w, *, c, sb, layer, q_blk):
    b, t, _ = z3.shape
    _, heads, dk, dv = state.shape
    depth = lb_raw.shape[0]
    wk = heads * dk
    kern = functools.partial(_hgrn_kernel, c=c, sb=sb, heads=heads, dk=dk, dv=dv, layer=layer)
    zspec = lambda off: pl.BlockSpec((1, c, wk), lambda i, n: (i, n, q_blk + off))
    return pl.pallas_call(
        kern,
        grid=(b, t // c),
        in_specs=[
            zspec(0), zspec(1), zspec(2), zspec(3),
            pl.BlockSpec((depth, wk), lambda i, n: (0, 0)),
            pl.BlockSpec((1, heads, dk, dv), lambda i, n: (i, 0, 0, 0)),
            pl.BlockSpec((1, dv), lambda i, n: (0, 0)),
        ],
        out_specs=[
            pl.BlockSpec((1, c, heads * dv), lambda i, n: (i, n, 0)),
            pl.BlockSpec((1, heads, dk, dv), lambda i, n: (i, 0, 0, 0)),
        ],
        out_shape=[
            jax.ShapeDtypeStruct((b, t, heads * dv), F32),
            jax.ShapeDtypeStruct((b, heads, dk, dv), F32),
        ],
        scratch_shapes=[
            pltpu.VMEM((heads, dv, dk), F32),
            pltpu.VMEM((sb + c, wk), F32),
            pltpu.VMEM((sb + c, wk), F32),
            pltpu.VMEM((sb + c, heads * dv), F32),
        ],
        compiler_params=_cparams(("parallel", "arbitrary")),
        name="hgrn",
    )(z3, z3, z3, z3, lb_raw, state, norm_w.reshape(1, dv))


def rope_tables(t, pos0, dh, theta):
    rd = dh // 4
    half = rd // 2
    inv = jnp.power(jnp.float32(theta), -jnp.arange(half, dtype=F32) / half)
    ang = (pos0 + jnp.arange(t)).astype(F32)[:, None] * inv
    cos, sin = jnp.cos(ang), jnp.sin(ang)
    d = jnp.arange(128) % dh
    f = d % half
    cc = jnp.where(d < rd, cos[:, f], 1.0)
    s1 = jnp.where(d < half, -sin[:, f], 0.0)
    s2 = jnp.where((d >= half) & (d < rd), sin[:, f], 0.0)
    return cc.astype(F32), s1.astype(F32), s2.astype(F32)


def _prep_kernel(q_ref, k_ref, v_ref, c_ref, s1_ref, s2_ref, q16_ref, k32_ref, k16_ref, v32_ref, v16_ref,
                 *, scale, half, heads):
    tm, w = q_ref.shape[1:]
    hw = w // heads
    reps = w // c_ref.shape[-1]
    cc = jnp.concatenate([c_ref[...]] * reps, axis=-1)
    s1 = jnp.concatenate([s1_ref[...]] * reps, axis=-1)
    s2 = jnp.concatenate([s2_ref[...]] * reps, axis=-1)

    def rope(x):
        return x * cc + pltpu.roll(x, w - half, 1) * s1 + pltpu.roll(x, half, 1) * s2

    q16_ref[0] = (rope(q_ref[0]) * scale).astype(BF16)
    k = rope(k_ref[0])
    k16_ref[0] = k.astype(BF16)
    v = v_ref[0]
    v16_ref[0] = v.astype(BF16)
    for h in range(heads):
        k32_ref[0, pl.ds(h, tm, stride=heads), :] = k[:, h * hw:(h + 1) * hw]
        v32_ref[0, pl.ds(h, tm, stride=heads), :] = v[:, h * hw:(h + 1) * hw]


def qkv_prep(z3, tables, *, q_blk, dh, heads, tm):
    b, t, _ = z3.shape
    w = tables[0].shape[-1] * (heads * 2 * dh // tables[0].shape[-1])
    hw = w // heads
    cc, s1, s2 = tables
    kern = functools.partial(_prep_kernel, scale=dh ** -0.5, half=dh // 8, heads=heads)
    zspec = lambda off: pl.BlockSpec((1, tm, w), lambda i, n: (i, n, q_blk + off))
    tspec = pl.BlockSpec((tm, cc.shape[-1]), lambda i, n: (n, 0))
    ospec = pl.BlockSpec((1, tm, w), lambda i, n: (i, n, 0))
    cspec = pl.BlockSpec((1, tm * heads, hw), lambda i, n: (i, n, 0))
    sd = lambda dt: jax.ShapeDtypeStruct((b, t, w), dt)
    sc = jax.ShapeDtypeStruct((b, t * heads, hw), F32)
    return pl.pallas_call(
        kern,
        grid=(b, t // tm),
        in_specs=[zspec(0), zspec(1), zspec(2), tspec, tspec, tspec],
        out_specs=[ospec, cspec, ospec, cspec, ospec],
        out_shape=[sd(BF16), sc, sd(BF16), sc, sd(BF16)],
        compiler_params=_cparams(("parallel", "parallel")),
        name="qkv_prep",
    )(z3, z3, z3, cc, s1, s2)


def _lambda(lam_ref, lam_init):
    lm = lam_ref[...]
    a = jnp.sum(lm[0:1, :] * lm[1:2, :], axis=-1, keepdims=True)
    b = jnp.sum(lm[2:3, :] * lm[3:4, :], axis=-1, keepdims=True)
    return jnp.exp(a) - jnp.exp(b) + lam_init


def _attn_prompt_kernel(qt_ref, kt_ref, q_ref, k_ref, v_ref, lam_ref, nw_ref, o_ref,
                        qs_scr, m_scr, l_scr, acc_scr, *, tq, dh, lam_init):
    p = pl.program_id(2)
    qi = qt_ref[p]
    ki = kt_ref[p]

    @pl.when(ki == 0)
    def _():
        q = q_ref[0]
        lane = _iota2(q.shape, 1)
        zero = jnp.zeros_like(q)
        qs_scr[0:tq, :] = jnp.where(lane < dh, q, zero)
        qs_scr[tq:2 * tq, :] = jnp.where(lane >= dh, q, zero)
        m_scr[...] = jnp.full(m_scr.shape, -jnp.inf, F32)
        l_scr[...] = jnp.zeros_like(l_scr)
        acc_scr[...] = jnp.zeros_like(acc_scr)

    def update(s):
        m_prev = m_scr[...]
        m_new = jnp.maximum(m_prev, jnp.max(s, axis=-1, keepdims=True))
        alpha = jnp.exp(m_prev - m_new)
        pr = jnp.exp(s - m_new)
        l_scr[...] = alpha * l_scr[...] + jnp.sum(pr, axis=-1, keepdims=True)
        acc_scr[...] = alpha * acc_scr[...] + _dot(pr.astype(BF16), v_ref[0])
        m_scr[...] = m_new

    @pl.when(ki < qi)
    def _():
        update(_dot_nt(qs_scr[...], k_ref[0]))

    @pl.when(ki == qi)
    def _():
        s = _dot_nt(qs_scr[...], k_ref[0])
        row = _iota2(s.shape, 0) & (tq - 1)
        col = _iota2(s.shape, 1)
        update(jnp.where(col <= row, s, -jnp.inf))
        acc = acc_scr[...]
        l = l_scr[...]
        o = acc[0:tq] / l[0:tq] - _lambda(lam_ref, lam_init) * (acc[tq:2 * tq] / l[tq:2 * tq])
        o_ref[0] = _rms(o, nw_ref[...]) * (1.0 - lam_init)


def attn_prompt(q16, k16, v16, lam, norm_w, *, tq, dh, lam_init):
    b, t, w = q16.shape
    dv = norm_w.shape[-1]
    heads = w // dv
    nq = t // tq
    pairs = [(i, j) for i in range(nq) for j in range(i + 1)]
    qt = jnp.asarray([pq for pq, _ in pairs], jnp.int32)
    kt = jnp.asarray([pk for _, pk in pairs], jnp.int32)
    kern = functools.partial(_attn_prompt_kernel, tq=tq, dh=dh, lam_init=lam_init)
    grid_spec = pltpu.PrefetchScalarGridSpec(
        num_scalar_prefetch=2,
        grid=(b, heads, len(pairs)),
        in_specs=[
            pl.BlockSpec((1, tq, dv), lambda i, h, p, qt, kt: (i, qt[p], h)),
            pl.BlockSpec((1, tq, dv), lambda i, h, p, qt, kt: (i, kt[p], h)),
            pl.BlockSpec((1, tq, dv), lambda i, h, p, qt, kt: (i, kt[p], h)),
            pl.BlockSpec(lam.shape, lambda i, h, p, qt, kt: (0, 0)),
            pl.BlockSpec((1, dv), lambda i, h, p, qt, kt: (0, 0)),
        ],
        out_specs=pl.BlockSpec((1, tq, dv), lambda i, h, p, qt, kt: (i, qt[p], h)),
        scratch_shapes=[
            pltpu.VMEM((2 * tq, dv), BF16),
            pltpu.VMEM((2 * tq, 1), F32),
            pltpu.VMEM((2 * tq, 1), F32),
            pltpu.VMEM((2 * tq, dv), F32),
        ],
    )
    return pl.pallas_call(
        kern,
        grid_spec=grid_spec,
        out_shape=jax.ShapeDtypeStruct((b, t, w), F32),
        compiler_params=_cparams(("parallel", "parallel", "arbitrary")),
        name="attn_prompt",
    )(qt, kt, q16, k16, v16, lam, norm_w.reshape(1, dv))


def _attn_sample_kernel(pt_ref, q_ref, kc_ref, vc_ref, lam_ref, nw_ref, *rest,
                        npages, tq, heads, dh, dv, page, lam_init):
    k_refs = rest[:npages]
    v_refs = rest[npages:2 * npages]
    o_ref = rest[2 * npages]
    lam = _lambda(lam_ref, lam_init)
    nw = nw_ref[...]
    q = q_ref[0].astype(F32)
    kc = kc_ref[0]
    vc = vc_ref[0]
    lane = _iota2((tq, dv), 1)
    qidx = _iota2((2 * tq, tq), 0) % tq
    kidx = _iota2((2 * tq, tq), 1)
    for h in range(heads):
        hs = slice(h * dv, (h + 1) * dv)
        qh = q[:, hs]
        qrow = jnp.concatenate([jnp.where(lane < dh, qh, 0.0), jnp.where(lane >= dh, qh, 0.0)], axis=0).astype(BF16)
        s_cur = jnp.where(kidx <= qidx, _dot_nt(qrow, kc[:, hs]), -jnp.inf)
        s_pages = [_dot_nt(qrow, k_refs[j][0, 0, pl.ds(h, page, stride=heads), :].astype(BF16))
                   for j in range(npages)]
        m = jnp.max(s_cur, axis=-1, keepdims=True)
        for s in s_pages:
            m = jnp.maximum(m, jnp.max(s, axis=-1, keepdims=True))
        p_cur = jnp.exp(s_cur - m)
        p_pages = [jnp.exp(s - m) for s in s_pages]
        l = jnp.sum(p_cur, axis=-1, keepdims=True)
        for pp in p_pages:
            l = l + jnp.sum(pp, axis=-1, keepdims=True)
        inv = 1.0 / l

        def diff(pp):
            pn = pp * inv
            return pn[0:tq] - lam * pn[tq:2 * tq]

        o = _dot(diff(p_cur), vc[:, hs].astype(F32))
        for j in range(npages):
            o = o + _dot(diff(p_pages[j]).astype(BF16), v_refs[j][0, 0, pl.ds(h, page, stride=heads), :].astype(BF16))
        o_ref[0, :, hs] = _rms(o, nw) * (1.0 - lam_init)


def attn_sample(q16, k16, v16, cache_k, cache_v, page_table, lam, norm_w, *, layer, heads, dh, lam_init):
    b, tq, w = q16.shape
    dv = norm_w.shape[-1]
    npages = page_table.shape[1]
    prow = cache_k.shape[2]
    kern = functools.partial(_attn_sample_kernel, npages=npages, tq=tq, heads=heads, dh=dh, dv=dv,
                             page=prow // heads, lam_init=lam_init)
    cur = pl.BlockSpec((1, tq, w), lambda i, pt: (i, 0, 0))

    def page_spec(j):
        return pl.BlockSpec((1, 1, prow, dv), lambda i, pt: (layer, pt[i, j], 0, 0))

    grid_spec = pltpu.PrefetchScalarGridSpec(
        num_scalar_prefetch=1,
        grid=(b,),
        in_specs=[cur, cur, cur,
                  pl.BlockSpec(lam.shape, lambda i, pt: (0, 0)),
                  pl.BlockSpec((1, dv), lambda i, pt: (0, 0))]
        + [page_spec(j) for j in range(npages)] * 2,
        out_specs=pl.BlockSpec((1, tq, w), lambda i, pt: (i, 0, 0)),
    )
    return pl.pallas_call(
        kern,
        grid_spec=grid_spec,
        out_shape=jax.ShapeDtypeStruct((b, tq, w), F32),
        compiler_params=_cparams(("parallel",)),
        name="attn_sample",
    )(page_table, q16, k16, v16, lam, norm_w.reshape(1, dv), *([cache_k] * npages), *([cache_v] * npages))


def _merge_kernel(x_ref, oa_ref, ob_ref, oc_ref, g0_ref, g1_ref, g2_ref, wb_ref, wo_ref, post_ref, o_ref):
    y = None
    for i, (o_i, g_i) in enumerate(((oa_ref, g0_ref), (ob_ref, g1_ref), (oc_ref, g2_ref))):
        ys = _dot(o_i[...].astype(BF16), wb_ref[i])
        t = _sigmoid(g_i[...]) * ys
        y = t if y is None else y + t
    y2 = _dot(y.astype(BF16), wo_ref[...])
    o_ref[...] = x_ref[...] + _rms(y2, post_ref[...])


def merge(x, o_a, o_b, o_c, z, w_branch_bf, w_out_bf, post_w, *, tm, gate_blk):
    n, d = x.shape
    bw = o_a.shape[-1]
    nb = w_branch_bf.shape[0]
    ospec = pl.BlockSpec((tm, bw), lambda i: (i, 0))
    gspec = lambda k: pl.BlockSpec((tm, d), lambda i: (i, gate_blk + k))
    return pl.pallas_call(
        _merge_kernel,
        grid=(n // tm,),
        in_specs=[
            pl.BlockSpec((tm, d), lambda i: (i, 0)),
            ospec, ospec, ospec, gspec(0), gspec(1), gspec(2),
            pl.BlockSpec((nb, bw, d), lambda i: (0, 0, 0)),
            pl.BlockSpec((d, d), lambda i: (0, 0)),
            pl.BlockSpec((1, d), lambda i: (0, 0)),
        ],
        out_specs=pl.BlockSpec((tm, d), lambda i: (i, 0)),
        out_shape=jax.ShapeDtypeStruct((n, d), F32),
        compiler_params=_cparams(("parallel",)),
        name="merge",
    )(x, o_a, o_b, o_c, z, z, z, w_branch_bf, w_out_bf, post_w.reshape(1, d))


ROPE_THETA = 500000.0
SCAN_CHUNK = 64
HGRN_SUBBLOCK = 16
ROW_TILE = 1024
FF_TILE = 256
PROJ_COL_TILES = 4
ATTN_TILE = 512
PREP_TILE = 512
MERGE_TILE = 512
SMALL_COLS = 128


def _split_w_in(w_in_l, sizes):
    offs = [0]
    for s in sizes:
        offs.append(offs[-1] + s)
    seg = lambda i: w_in_l[:, offs[i]:offs[i + 1]]
    main = jnp.concatenate([seg(11), seg(0), seg(1), seg(4), seg(5), seg(6), seg(7), seg(8), seg(9), seg(10)], axis=1)
    small = jnp.concatenate([seg(2), seg(3)], axis=1)
    small = jnp.pad(small, ((0, 0), (0, SMALL_COLS - small.shape[1])))
    return main.astype(BF16), small.astype(BF16)


def _row_tile(n, pref):
    return pref if n % pref == 0 else n


def kernel(x_prompt, x_sample, state_gdn, state_gdn_conv, state_hgrn, cache_k, cache_v, page_table,
           ffn1_norm_pre, ffn1_norm_post, ffn1_w_in, ffn1_w_out, mix_norm_pre, mix_norm_post, w_in,
           gdn_conv_w, gdn_a_log, gdn_dt_bias, gdn_norm_w, hgrn_lb_raw, hgrn_norm_w, diff_lambda,
           diff_norm_w, w_branch, w_out, ffn2_norm_pre, ffn2_norm_post, ffn2_w_in, ffn2_w_out):
    depth = w_in.shape[0]
    bp, tp, d = x_prompt.shape
    bs, ts, _ = x_sample.shape
    _, _, gh, gdk, gdv = state_gdn.shape
    cch = state_gdn_conv.shape[-1]
    _, _, hh, hdk, hdv = state_hgrn.shape
    _, n_pool, page, ah, adh2 = cache_k.shape
    adh = adh2 // 2
    adv = cache_v.shape[-1]
    aw = ah * adv
    assert ah * adh2 == aw and hh * hdk == aw and hh * hdv == aw and gh * gdv == aw and cch == 3 * aw and d == 2 * aw
    sizes = (cch, gh * gdv, gh, gh, hh * hdk, hh * hdk, hh * hdv, hh * hdv, ah * adh2, ah * adh2, aw, 3 * d)
    gate_blk, qkv_blk, za_blk, hq_blk, aq_blk = 0, (3 * d) // cch, (3 * d + cch) // aw, (3 * d + cch) // aw + 1, (3 * d + cch) // aw + 5
    past_len = page_table.shape[1] * page
    ck = cache_k.reshape(depth, n_pool, page * ah, adh2)
    cv = cache_v.reshape(depth, n_pool, page * ah, adv)
    tabs_p = rope_tables(tp, 0, adh, ROPE_THETA)
    tabs_s = rope_tables(ts, past_len, adh, ROPE_THETA)
    zeros_conv = jnp.zeros((bp, 3, cch), F32)
    zeros_gdn = jnp.zeros((bp, gh, gdk, gdv), F32)
    zeros_hgrn = jnp.zeros((bp, hh, hdk, hdv), F32)

    def run_layer(l, x, b, t, conv_state, gdn_state, hgrn_state, tabs, prompt, wts):
        (f1_in, f1_out, w_main, w_small, wb, wo, f2_in, f2_out) = wts
        n = b * t
        tm = _row_tile(n, ROW_TILE)
        lam_init = 0.8 - 0.6 * math.exp(-0.3 * l)
        x = ffn(x, ffn1_norm_pre[l], f1_in, f1_out, ffn1_norm_post[l], tm=tm, tf=FF_TILE)
        z, zs = proj(x, mix_norm_pre[l], w_main, w_small, tm=tm, tn=w_main.shape[1] // PROJ_COL_TILES)
        z3 = z.reshape(b, t, z.shape[-1])
        zs3 = zs.reshape(b, t, SMALL_COLS)
        c = math.gcd(SCAN_CHUNK, t)
        o_a, new_gdn, new_conv = gdn(z3, zs3, conv_state, gdn_state, gdn_conv_w[l], gdn_a_log[l], gdn_dt_bias[l],
                                     gdn_norm_w[l], c=c, qkv_blk=qkv_blk, za_blk=za_blk)
        o_b, new_hgrn = hgrn(z3, hgrn_lb_raw, hgrn_state, hgrn_norm_w[l], c=c, sb=min(HGRN_SUBBLOCK, c),
                             layer=l, q_blk=hq_blk)
        q16, k32, k16, v32, v16 = qkv_prep(z3, tabs, q_blk=aq_blk, dh=adh, heads=ah, tm=_row_tile(t, PREP_TILE))
        if prompt:
            o_c = attn_prompt(q16, k16, v16, diff_lambda[l], diff_norm_w[l], tq=_row_tile(t, ATTN_TILE), dh=adh,
                              lam_init=lam_init)
        else:
            o_c = attn_sample(q16, k16, v16, ck, cv, page_table, diff_lambda[l], diff_norm_w[l], layer=l, heads=ah, dh=adh,
                              lam_init=lam_init)
        x = merge(x, o_a.reshape(n, aw), o_b.reshape(n, aw), o_c.reshape(n, aw), z, wb, wo, mix_norm_post[l],
                  tm=_row_tile(n, MERGE_TILE), gate_blk=gate_blk)
        x = ffn(x, ffn2_norm_pre[l], f2_in, f2_out, ffn2_norm_post[l], tm=tm, tf=FF_TILE)
        return x, (new_gdn, new_conv, new_hgrn, k32.reshape(b, t, ah, adh2), v32.reshape(b, t, ah, adv))

    xp = x_prompt.reshape(bp * tp, d)
    xs = x_sample.reshape(bs * ts, d)
    outs_p, outs_s = [], []
    for l in range(depth):
        w_main, w_small = _split_w_in(w_in[l], sizes)
        wts = (ffn1_w_in[l].astype(BF16), ffn1_w_out[l].astype(BF16), w_main, w_small,
               w_branch[l].astype(BF16), w_out[l].astype(BF16), ffn2_w_in[l].astype(BF16), ffn2_w_out[l].astype(BF16))
        xp, st = run_layer(l, xp, bp, tp, zeros_conv, zeros_gdn, zeros_hgrn, tabs_p, True, wts)
        outs_p.append(st)
        xs, ss = run_layer(l, xs, bs, ts, state_gdn_conv[l], state_gdn[l], state_hgrn[l], tabs_s, False, wts)
        outs_s.append(ss)

    stack = lambda outs, i: jnp.stack([o[i] for o in outs])
    return (xp.reshape(bp, tp, d), xs.reshape(bs, ts, d),
            stack(outs_p, 0), stack(outs_p, 1), stack(outs_p, 2), stack(outs_p, 3), stack(outs_p, 4),
            stack(outs_s, 0), stack(outs_s, 1), stack(outs_s, 2), stack(outs_s, 3), stack(outs_s, 4))
```

```python
import functools
import math

import jax
import jax.numpy as jnp
from jax import lax
from jax.experimental import pallas as pl
from jax.experimental.pallas import tpu as pltpu

F32 = jnp.float32
BF16 = jnp.bfloat16
EPS = 1e-6
HI = lax.Precision.HIGHEST

VMEM_LIMIT_BYTES = 56 * 1024 * 1024


def _cparams(sem):
    return pltpu.CompilerParams(dimension_semantics=sem, vmem_limit_bytes=VMEM_LIMIT_BYTES)


def _rms(x, w):
    return x * lax.rsqrt(jnp.mean(x * x, axis=-1, keepdims=True) + EPS) * w


def _sigmoid(x):
    return 1.0 / (1.0 + jnp.exp(-x))


def _silu(x):
    return x * _sigmoid(x)


def _softplus(x):
    return jnp.maximum(x, 0.0) + jnp.log1p(jnp.exp(-jnp.abs(x)))


def _ffn_kernel(x_ref, pre_ref, wg_ref, wu_ref, wo_ref, post_ref, o_ref, h_scr, acc_scr):
    j = pl.program_id(1)

    @pl.when(j == 0)
    def _():
        h_scr[...] = _rms(x_ref[...], pre_ref[...]).astype(BF16)
        acc_scr[...] = jnp.zeros_like(acc_scr)

    h = h_scr[...]
    g = jnp.dot(h, wg_ref[...], preferred_element_type=F32)
    u = jnp.dot(h, wu_ref[...], preferred_element_type=F32)
    a = (_silu(g) * u).astype(BF16)
    acc_scr[...] += jnp.dot(a, wo_ref[...], preferred_element_type=F32)

    @pl.when(j == pl.num_programs(1) - 1)
    def _():
        o_ref[...] = x_ref[...] + 0.5 * _rms(acc_scr[...], post_ref[...])


def ffn(x, pre_w, w_in_bf, w_out_bf, post_w, *, layer, tm, tf):
    n, d = x.shape
    dff = w_out_bf.shape[1]
    nj = dff // tf
    return pl.pallas_call(
        _ffn_kernel,
        grid=(n // tm, nj),
        in_specs=[
            pl.BlockSpec((tm, d), lambda i, j: (i, 0)),
            pl.BlockSpec((1, d), lambda i, j: (0, 0)),
            pl.BlockSpec((None, d, tf), lambda i, j: (layer, 0, j)),
            pl.BlockSpec((None, d, tf), lambda i, j: (layer, 0, j + nj)),
            pl.BlockSpec((None, tf, d), lambda i, j: (layer, j, 0)),
            pl.BlockSpec((1, d), lambda i, j: (0, 0)),
        ],
        out_specs=pl.BlockSpec((tm, d), lambda i, j: (i, 0)),
        out_shape=jax.ShapeDtypeStruct((n, d), F32),
        scratch_shapes=[pltpu.VMEM((tm, d), BF16), pltpu.VMEM((tm, d), F32)],
        compiler_params=_cparams(("parallel", "arbitrary")),
        name="ffn",
    )(x, pre_w.reshape(1, d), w_in_bf, w_in_bf, w_out_bf, post_w.reshape(1, d))


def _proj_kernel(x_ref, pre_ref, w_ref, ws_ref, z_ref, zs_ref, h_scr):
    j = pl.program_id(1)

    @pl.when(j == 0)
    def _():
        h = _rms(x_ref[...], pre_ref[...]).astype(BF16)
        h_scr[...] = h
        zs_ref[...] = jnp.dot(h, ws_ref[...], preferred_element_type=F32)

    z_ref[...] = jnp.dot(h_scr[...], w_ref[...], preferred_element_type=F32)


def proj(x, pre_w, w_main_bf, w_small_bf, *, tm, tn):
    n, d = x.shape
    cols = w_main_bf.shape[1]
    sc = w_small_bf.shape[1]
    return pl.pallas_call(
        _proj_kernel,
        grid=(n // tm, cols // tn),
        in_specs=[
            pl.BlockSpec((tm, d), lambda i, j: (i, 0)),
            pl.BlockSpec((1, d), lambda i, j: (0, 0)),
            pl.BlockSpec((d, tn), lambda i, j: (0, j)),
            pl.BlockSpec((d, sc), lambda i, j: (0, 0)),
        ],
        out_specs=[
            pl.BlockSpec((tm, tn), lambda i, j: (i, j)),
            pl.BlockSpec((tm, sc), lambda i, j: (i, 0)),
        ],
        out_shape=[jax.ShapeDtypeStruct((n, cols), F32), jax.ShapeDtypeStruct((n, sc), F32)],
        scratch_shapes=[pltpu.VMEM((tm, d), BF16)],
        compiler_params=_cparams(("parallel", "arbitrary")),
        name="proj",
    )(x, pre_w.reshape(1, d), w_main_bf, w_small_bf)


def _dot(a, b, precision=None):
    return jnp.dot(a, b, preferred_element_type=F32, precision=precision)


def _dot_nt(a, b, precision=None):
    return lax.dot_general(a, b, (((1,), (1,)), ((), ())), preferred_element_type=F32, precision=precision)


def _dot_tn(a, b, precision=None):
    return lax.dot_general(a, b, (((0,), (0,)), ((), ())), preferred_element_type=F32, precision=precision)


def _iota2(shape, dim):
    return lax.broadcasted_iota(jnp.int32, shape, dim)


def _split_bf16(x):
    hi = x.astype(BF16)
    return hi, (x - hi.astype(F32)).astype(BF16)


def _dot3(a, b):
    ah, al = _split_bf16(a)
    bh, bl = _split_bf16(b)
    return _dot(ah, bh) + (_dot(ah, bl) + _dot(al, bh))


def _tri_inverse_all(lmats, c):
    ri = _iota2((c, c), 0)
    ci = _iota2((c, c), 1)
    eye = jnp.where(ri == ci, 1.0, 0.0).astype(F32)
    nb = min(16, c)
    sh = int(math.log2(nb))
    same = (ri >> sh) == (ci >> sh)
    ps = [jnp.where(same, lm, 0.0) for lm in lmats]
    ts = [eye - p for p in ps]
    k = 2
    while k < nb:
        ps = [_dot3(p, p) for p in ps]
        ts = [_dot3(t, eye + p) for t, p in zip(ts, ps)]
        k *= 2
    blk = nb
    while blk < c:
        s1 = int(math.log2(blk))
        offm = ((ri >> (s1 + 1)) == (ci >> (s1 + 1))) & ((ri >> s1) != (ci >> s1))
        tl = [_dot3(t, jnp.where(offm, lm, 0.0)) for t, lm in zip(ts, lmats)]
        ts = [t - _dot3(x, t) for t, x in zip(ts, tl)]
        blk *= 2
    return ts


def _gdn_kernel(qkv_ref, za_ref, zs_ref, cs_ref, s0_ref, cw_ref, alog_ref, dtb_ref, nw_ref,
                o_ref, sout_ref, cout_ref, xbuf, s_scr, *, c, nc, bb, heads, dk, dv):
    n = pl.program_id(1)
    nlast = pl.num_programs(1) - 1
    nqk = heads * dk
    r = nc * c

    @pl.when(n == 0)
    def _():
        xbuf[:, 5:8, :] = cs_ref[...]
        s_scr[...] = s0_ref[...]

    xbuf[:, 8:8 + r, :] = qkv_ref[...]

    @pl.when(n == nlast)
    def _():
        cout_ref[...] = xbuf[:, 5 + r:8 + r, :]

    ri = _iota2((c, c), 0)
    ci = _iota2((c, c), 1)
    incl = ri >= ci
    strict = ri > ci
    rr = _iota2((r, r), 0)
    rc = _iota2((r, r), 1)
    sh = int(math.log2(c))
    blocktril = jnp.where((rr >= rc) & ((rr >> sh) == (rc >> sh)), 1.0, 0.0).astype(F32)
    nw = nw_ref[...]

    inst = []
    for bi in range(bb):
        y = xbuf[bi, 5:5 + r, :] * cw_ref[0:1, :]
        for j in range(1, 4):
            y = y + xbuf[bi, 5 + j:5 + j + r, :] * cw_ref[j:j + 1, :]
        y = _silu(y)
        zs = zs_ref[bi]
        beta_all = _sigmoid(zs)
        g_all = -jnp.exp(alog_ref[...]) * _softplus(zs + dtb_ref[...])
        gcum_all = _dot3(blocktril, g_all)
        gcum_t = gcum_all.T
        for i in range(nc):
            rows = slice(i * c, (i + 1) * c)
            for h in range(heads):
                inst.append(dict(
                    bi=bi, i=i, h=h,
                    q=y[rows, h * dk:(h + 1) * dk],
                    k=y[rows, nqk + h * dk:nqk + (h + 1) * dk],
                    v=y[rows, 2 * nqk + h * dv:2 * nqk + (h + 1) * dv],
                    beta=beta_all[rows, h:h + 1],
                    gc=gcum_all[rows, heads + h:heads + h + 1],
                    grow=gcum_t[heads + h:heads + h + 1, rows]))
    xbuf[:, 5:8, :] = xbuf[:, 5 + r:8 + r, :]

    for d in inst:
        q, k = d["q"], d["k"]
        d["q"] = q * lax.rsqrt(jnp.sum(q * q, axis=-1, keepdims=True) + EPS) * (dk ** -0.5)
        d["k"] = k * lax.rsqrt(jnp.sum(k * k, axis=-1, keepdims=True) + EPS)
        d["k16"] = d["k"].astype(BF16)
        d["kb"] = d["k"] * d["beta"]
        d["decay"] = jnp.where(incl, jnp.exp(jnp.minimum(d["gc"] - d["grow"], 0.0)), 0.0)
    for d in inst:
        d["lmat"] = jnp.where(strict, _dot_nt(d["kb"].astype(BF16), d["k16"]) * d["decay"], 0.0)
    tinvs = _tri_inverse_all([d["lmat"] for d in inst], c)
    for d, tinv in zip(inst, tinvs):
        eg = jnp.exp(d["gc"])
        sol = _dot3(tinv, jnp.concatenate([d["v"] * d["beta"], d["kb"] * eg], axis=-1))
        d["u"] = sol[:, :dv]
        d["w16"] = sol[:, dv:].astype(BF16)
        d["attn16"] = (_dot_nt(d["q"].astype(BF16), d["k16"]) * d["decay"]).astype(BF16)
        d["qe16"] = (d["q"] * eg).astype(BF16)
        gl = d["gc"][c - 1:c, :]
        d["kdec16"] = (d["k"] * jnp.exp(gl - d["gc"])).astype(BF16)
        d["egl"] = jnp.exp(gl)

    state = {(bi, h): s_scr[bi, h] for bi in range(bb) for h in range(heads)}
    for i in range(nc):
        for d in inst:
            if d["i"] != i:
                continue
            bi, h = d["bi"], d["h"]
            s = state[(bi, h)]
            s16 = s.astype(BF16)
            v_new = d["u"] - _dot(d["w16"], s16)
            vn16 = v_new.astype(BF16)
            o = _dot(d["qe16"], s16) + _dot(d["attn16"], vn16)
            state[(bi, h)] = s * d["egl"] + _dot_tn(d["kdec16"], vn16)
            za = za_ref[bi, i * c:(i + 1) * c, h * dv:(h + 1) * dv]
            o_ref[bi, i * c:(i + 1) * c, h * dv:(h + 1) * dv] = _rms(o, nw) * _silu(za)
    for (bi, h), s in state.items():
        s_scr[bi, h] = s

    @pl.when(n == nlast)
    def _():
        sout_ref[...] = s_scr[...]


def gdn(z3, zs3, conv_state, state, conv_w, a_log, dt_bias, norm_w, *, state_layer, c, nc, bb, qkv_blk, za_blk):
    b, t, _ = z3.shape
    _, _, heads, dk, dv = state.shape
    cch = conv_state.shape[-1]
    sc = zs3.shape[-1]
    r = nc * c
    pad = jnp.zeros((1, sc), F32)
    alog_p = lax.dynamic_update_slice(pad, a_log.reshape(1, heads).astype(F32), (0, heads))
    dtb_p = lax.dynamic_update_slice(pad, dt_bias.reshape(1, heads).astype(F32), (0, heads))
    kern = functools.partial(_gdn_kernel, c=c, nc=nc, bb=bb, heads=heads, dk=dk, dv=dv)
    return pl.pallas_call(
        kern,
        grid=(b // bb, t // r),
        in_specs=[
            pl.BlockSpec((bb, r, cch), lambda i, n: (i, n, qkv_blk)),
            pl.BlockSpec((bb, r, heads * dv), lambda i, n: (i, n, za_blk)),
            pl.BlockSpec((bb, r, sc), lambda i, n: (i, n, 0)),
            pl.BlockSpec((None, bb, 3, cch), lambda i, n: (state_layer, i, 0, 0)),
            pl.BlockSpec((None, bb, heads, dk, dv), lambda i, n: (state_layer, i, 0, 0, 0)),
            pl.BlockSpec((4, cch), lambda i, n: (0, 0)),
            pl.BlockSpec((1, sc), lambda i, n: (0, 0)),
            pl.BlockSpec((1, sc), lambda i, n: (0, 0)),
            pl.BlockSpec((1, dv), lambda i, n: (0, 0)),
        ],
        out_specs=[
            pl.BlockSpec((bb, r, heads * dv), lambda i, n: (i, n, 0)),
            pl.BlockSpec((bb, heads, dk, dv), lambda i, n: (i, 0, 0, 0)),
            pl.BlockSpec((bb, 3, cch), lambda i, n: (i, 0, 0)),
        ],
        out_shape=[
            jax.ShapeDtypeStruct((b, t, heads * dv), F32),
            jax.ShapeDtypeStruct((b, heads, dk, dv), F32),
            jax.ShapeDtypeStruct((b, 3, cch), F32),
        ],
        scratch_shapes=[pltpu.VMEM((bb, 8 + r, cch), F32), pltpu.VMEM((bb, heads, dk, dv), F32)],
        compiler_params=_cparams(("parallel", "arbitrary")),
        name="gdn",
    )(z3, z3, zs3, conv_state, state, conv_w, alog_p, dtb_p, norm_w.reshape(1, dv))


def _hgrn_kernel(q_ref, f_ref, i_ref, g_ref, lbraw_ref, s0_ref, nw_ref, o_ref, sout_ref,
                 st_scr, kbuf, bbuf, vbuf, *, c, sb, heads, dk, dv, layer):
    n = pl.program_id(1)
    nlast = pl.num_programs(1) - 1

    @pl.when(n == 0)
    def _():
        for h in range(heads):
            st_scr[h] = s0_ref[0, h].T
        kbuf[0:sb, :] = jnp.zeros((sb, heads * dk), F32)
        bbuf[0:sb, :] = jnp.zeros((sb, heads * dk), F32)
        vbuf[0:sb, :] = jnp.zeros((sb, heads * dv), F32)

    raw = lbraw_ref[...]
    e = jnp.exp(raw - jnp.max(raw, axis=0, keepdims=True))
    sm = e / jnp.sum(e, axis=0, keepdims=True)
    lb = jnp.zeros((1, heads * dk), F32)
    for l in range(1, layer + 1):
        lb = lb + sm[l:l + 1, :]

    zf = f_ref[0]
    logsig = jnp.minimum(zf, 0.0) - jnp.log1p(jnp.exp(-jnp.abs(zf)))
    la = jnp.log(lb)
    lbb = jnp.log1p(-lb) + logsig
    logf = jnp.maximum(la, lbb) + jnp.log1p(jnp.exp(-jnp.abs(la - lbb)))
    kh = (1.0 - lb) * _sigmoid(-zf)
    qh = _silu(q_ref[0]) * (dk ** -0.5)
    vv = i_ref[0]
    gate = _sigmoid(g_ref[0])

    ri = _iota2((c, c), 0)
    ci = _iota2((c, c), 1)
    tril = jnp.where(ri >= ci, 1.0, 0.0).astype(F32)
    bcum = _dot(tril, logf, HI)
    kbuf[sb:sb + c, :] = kh
    bbuf[sb:sb + c, :] = bcum
    vbuf[sb:sb + c, :] = vv
    row = _iota2((c, dk), 0)
    rmod = row & (sb - 1)
    nsb = c // sb
    nw = nw_ref[...]

    for h in range(heads):
        ks = slice(h * dk, (h + 1) * dk)
        vs = slice(h * dv, (h + 1) * dv)
        q = qh[:, ks]
        k = kh[:, ks]
        b = bcum[:, ks]
        v = vv[:, vs]
        st = st_scr[h]
        o = _dot_nt((q * jnp.exp(b)).astype(BF16), st.astype(BF16))
        for d in range(sb):
            ksh = kbuf[sb - d:sb - d + c, ks]
            bsh = bbuf[sb - d:sb - d + c, ks]
            vsh = vbuf[sb - d:sb - d + c, vs]
            m = jnp.where(rmod >= d, q * ksh * jnp.exp(jnp.minimum(b - bsh, 0.0)), 0.0)
            o = o + jnp.sum(m, axis=-1, keepdims=True) * vsh
        if nsb > 1:
            qs, kk = [], []
            for jb in range(nsb - 1):
                r = b[(jb + 1) * sb - 1:(jb + 1) * sb, :]
                qs.append(jnp.where(row >= (jb + 1) * sb, q * jnp.exp(jnp.minimum(b - r, 0.0)), 0.0))
                kk.append(jnp.where((row >= jb * sb) & (row < (jb + 1) * sb),
                                    k * jnp.exp(jnp.minimum(r - b, 0.0)), 0.0))
            amat = _dot_nt(jnp.concatenate(qs, axis=-1), jnp.concatenate(kk, axis=-1), HI)
            o = o + _dot(amat.astype(BF16), v.astype(BF16))
        bl = b[c - 1:c, :]
        kdec = k * jnp.exp(bl - b)
        st_scr[h] = st * jnp.exp(bl) + _dot_tn(v.astype(BF16), kdec.astype(BF16))
        o_ref[0, :, vs] = _rms(o, nw) * gate[:, vs]

    @pl.when(n == nlast)
    def _():
        for h in range(heads):
            sout_ref[0, h] = st_scr[h].T


def hgrn(z3, lb_raw, state, norm_w, *, state_layer, c, sb, layer, q_blk):
    b, t, _ = z3.shape
    _, _, heads, dk, dv = state.shape
    depth = lb_raw.shape[0]
    wk = heads * dk
    kern = functools.partial(_hgrn_kernel, c=c, sb=sb, heads=heads, dk=dk, dv=dv, layer=layer)
    zspec = lambda off: pl.BlockSpec((1, c, wk), lambda i, n: (i, n, q_blk + off))
    return pl.pallas_call(
        kern,
        grid=(b, t // c),
        in_specs=[
            zspec(0), zspec(1), zspec(2), zspec(3),
            pl.BlockSpec((depth, wk), lambda i, n: (0, 0)),
            pl.BlockSpec((None, 1, heads, dk, dv), lambda i, n: (state_layer, i, 0, 0, 0)),
            pl.BlockSpec((1, dv), lambda i, n: (0, 0)),
        ],
        out_specs=[
            pl.BlockSpec((1, c, heads * dv), lambda i, n: (i, n, 0)),
            pl.BlockSpec((1, heads, dk, dv), lambda i, n: (i, 0, 0, 0)),
        ],
        out_shape=[
            jax.ShapeDtypeStruct((b, t, heads * dv), F32),
            jax.ShapeDtypeStruct((b, heads, dk, dv), F32),
        ],
        scratch_shapes=[
            pltpu.VMEM((heads, dv, dk), F32),
            pltpu.VMEM((sb + c, wk), F32),
            pltpu.VMEM((sb + c, wk), F32),
            pltpu.VMEM((sb + c, heads * dv), F32),
        ],
        compiler_params=_cparams(("parallel", "arbitrary")),
        name="hgrn",
    )(z3, z3, z3, z3, lb_raw, state, norm_w.reshape(1, dv))


def rope_tables(t, pos0, dh, theta):
    rd = dh // 4
    half = rd // 2
    inv = jnp.power(jnp.float32(theta), -jnp.arange(half, dtype=F32) / half)
    ang = (pos0 + jnp.arange(t)).astype(F32)[:, None] * inv
    cos, sin = jnp.cos(ang), jnp.sin(ang)
    d = jnp.arange(128) % dh
    f = d % half
    cc = jnp.where(d < rd, cos[:, f], 1.0)
    s1 = jnp.where(d < half, -sin[:, f], 0.0)
    s2 = jnp.where((d >= half) & (d < rd), sin[:, f], 0.0)
    return cc.astype(F32), s1.astype(F32), s2.astype(F32)


def _prep_kernel(q_ref, k_ref, v_ref, c_ref, s1_ref, s2_ref, q16_ref, k32_ref, k16_ref, v32_ref, v16_ref,
                 *, scale, half, heads, v_transposed):
    tm, w = q_ref.shape[1:]
    hw = w // heads
    reps = w // c_ref.shape[-1]
    cc = jnp.concatenate([c_ref[...]] * reps, axis=-1)
    s1 = jnp.concatenate([s1_ref[...]] * reps, axis=-1)
    s2 = jnp.concatenate([s2_ref[...]] * reps, axis=-1)

    def rope(x):
        return x * cc + pltpu.roll(x, w - half, 1) * s1 + pltpu.roll(x, half, 1) * s2

    q16_ref[0] = (rope(q_ref[0]) * scale).astype(BF16)
    k = rope(k_ref[0])
    k16_ref[0] = k.astype(BF16)
    v = v_ref[0]
    v16_ref[0] = (v.T if v_transposed else v).astype(BF16)
    for h in range(heads):
        k32_ref[0, pl.ds(h, tm, stride=heads), :] = k[:, h * hw:(h + 1) * hw]
        v32_ref[0, pl.ds(h, tm, stride=heads), :] = v[:, h * hw:(h + 1) * hw]


def qkv_prep(z3, tables, *, q_blk, dh, heads, tm, v_transposed):
    b, t, _ = z3.shape
    w = tables[0].shape[-1] * (heads * 2 * dh // tables[0].shape[-1])
    hw = w // heads
    cc, s1, s2 = tables
    kern = functools.partial(_prep_kernel, scale=dh ** -0.5, half=dh // 8, heads=heads, v_transposed=v_transposed)
    zspec = lambda off: pl.BlockSpec((1, tm, w), lambda i, n: (i, n, q_blk + off))
    tspec = pl.BlockSpec((tm, cc.shape[-1]), lambda i, n: (n, 0))
    ospec = pl.BlockSpec((1, tm, w), lambda i, n: (i, n, 0))
    cspec = pl.BlockSpec((1, tm * heads, hw), lambda i, n: (i, n, 0))
    sd = lambda dt: jax.ShapeDtypeStruct((b, t, w), dt)
    sc = jax.ShapeDtypeStruct((b, t * heads, hw), F32)
    vspec = pl.BlockSpec((1, w, tm), lambda i, n: (i, 0, n)) if v_transposed else ospec
    vsd = jax.ShapeDtypeStruct((b, w, t), BF16) if v_transposed else sd(BF16)
    return pl.pallas_call(
        kern,
        grid=(b, t // tm),
        in_specs=[zspec(0), zspec(1), zspec(2), tspec, tspec, tspec],
        out_specs=[ospec, cspec, ospec, cspec, vspec],
        out_shape=[sd(BF16), sc, sd(BF16), sc, vsd],
        compiler_params=_cparams(("parallel", "parallel")),
        name="qkv_prep",
    )(z3, z3, z3, cc, s1, s2)


def _lambda(lam_ref, lam_init):
    lm = lam_ref[...]
    a = jnp.sum(lm[0:1, :] * lm[1:2, :], axis=-1, keepdims=True)
    b = jnp.sum(lm[2:3, :] * lm[3:4, :], axis=-1, keepdims=True)
    return jnp.exp(a) - jnp.exp(b) + lam_init


def _attn_prompt_kernel(qt_ref, kt_ref, q_ref, k_ref, vt_ref, lam_ref, nw_ref, o_ref,
                        qs_scr, m_scr, l_scr, acc_scr, *, tq, dh, qchunk, lam_init):
    p = pl.program_id(2)
    qi = qt_ref[p]
    ki = kt_ref[p]

    @pl.when(ki == 0)
    def _():
        q = q_ref[0]
        lane = _iota2(q.shape, 1)
        zero = jnp.zeros_like(q)
        qs_scr[0:tq, :] = jnp.where(lane < dh, q, zero)
        qs_scr[tq:2 * tq, :] = jnp.where(lane >= dh, q, zero)
        m_scr[...] = jnp.full(m_scr.shape, -jnp.inf, F32)
        l_scr[...] = jnp.zeros_like(l_scr)
        acc_scr[...] = jnp.zeros_like(acc_scr)

    def update(diag):
        k = k_ref[0]
        vt = vt_ref[0]
        for c0 in range(0, 2 * tq, qchunk):
            cs = slice(c0, c0 + qchunk)
            st = _dot_nt(k, qs_scr[cs, :])
            if diag:
                key = _iota2(st.shape, 0)
                qry = (_iota2(st.shape, 1) + c0) & (tq - 1)
                st = jnp.where(key <= qry, st, -jnp.inf)
            m_prev = m_scr[:, cs]
            m_new = jnp.maximum(m_prev, jnp.max(st, axis=0, keepdims=True))
            alpha = jnp.exp(m_prev - m_new)
            pr = jnp.exp(st - m_new)
            l_scr[:, cs] = alpha * l_scr[:, cs] + jnp.sum(pr, axis=0, keepdims=True)
            acc_scr[:, cs] = alpha * acc_scr[:, cs] + _dot(vt, pr.astype(BF16))
            m_scr[:, cs] = m_new

    @pl.when(ki < qi)
    def _():
        update(False)

    @pl.when(ki == qi)
    def _():
        update(True)
        o1 = acc_scr[:, 0:tq] / l_scr[:, 0:tq]
        o2 = acc_scr[:, tq:2 * tq] / l_scr[:, tq:2 * tq]
        o = (o1 - _lambda(lam_ref, lam_init) * o2).T
        o_ref[0] = _rms(o, nw_ref[...]) * (1.0 - lam_init)


def attn_prompt(q16, k16, vt16, lam, norm_w, *, tq, dh, lam_init):
    b, t, w = q16.shape
    dv = norm_w.shape[-1]
    heads = w // dv
    nq = t // tq
    pairs = [(i, j) for i in range(nq) for j in range(i + 1)]
    qt = jnp.asarray([pq for pq, _ in pairs], jnp.int32)
    kt = jnp.asarray([pk for _, pk in pairs], jnp.int32)
    kern = functools.partial(_attn_prompt_kernel, tq=tq, dh=dh, qchunk=min(ATTN_QCHUNK, 2 * tq), lam_init=lam_init)
    grid_spec = pltpu.PrefetchScalarGridSpec(
        num_scalar_prefetch=2,
        grid=(b, heads, len(pairs)),
        in_specs=[
            pl.BlockSpec((1, tq, dv), lambda i, h, p, qt, kt: (i, qt[p], h)),
            pl.BlockSpec((1, tq, dv), lambda i, h, p, qt, kt: (i, kt[p], h)),
            pl.BlockSpec((1, dv, tq), lambda i, h, p, qt, kt: (i, h, kt[p])),
            pl.BlockSpec(lam.shape, lambda i, h, p, qt, kt: (0, 0)),
            pl.BlockSpec((1, dv), lambda i, h, p, qt, kt: (0, 0)),
        ],
        out_specs=pl.BlockSpec((1, tq, dv), lambda i, h, p, qt, kt: (i, qt[p], h)),
        scratch_shapes=[
            pltpu.VMEM((2 * tq, dv), BF16),
            pltpu.VMEM((1, 2 * tq), F32),
            pltpu.VMEM((1, 2 * tq), F32),
            pltpu.VMEM((dv, 2 * tq), F32),
        ],
    )
    return pl.pallas_call(
        kern,
        grid_spec=grid_spec,
        out_shape=jax.ShapeDtypeStruct((b, t, w), F32),
        compiler_params=_cparams(("parallel", "parallel", "arbitrary")),
        name="attn_prompt",
    )(qt, kt, q16, k16, vt16, lam, norm_w.reshape(1, dv))


def _attn_sample_kernel(pt_ref, q_ref, kc_ref, vc_ref, lam_ref, nw_ref, *rest,
                        npages, tq, heads, dh, dv, page, lam_init):
    k_refs = rest[:npages]
    v_refs = rest[npages:2 * npages]
    o_ref = rest[2 * npages]
    lam = _lambda(lam_ref, lam_init)
    nw = nw_ref[...]
    q = q_ref[0].astype(F32)
    kc = kc_ref[0]
    vc = vc_ref[0]
    lane = _iota2((tq, dv), 1)
    qidx = _iota2((2 * tq, tq), 0) % tq
    kidx = _iota2((2 * tq, tq), 1)
    hsl = [slice(h * dv, (h + 1) * dv) for h in range(heads)]

    def head_rows(refs, h):
        return jnp.concatenate([r[0, 0, pl.ds(h, page, stride=heads), :].astype(BF16) for r in refs], axis=0)

    qrows = [jnp.concatenate([jnp.where(lane < dh, q[:, hs], 0.0), jnp.where(lane >= dh, q[:, hs], 0.0)],
                             axis=0).astype(BF16) for hs in hsl]
    s_past = [_dot_nt(qrows[h], head_rows(k_refs, h)) for h in range(heads)]
    s_cur = [jnp.where(kidx <= qidx, _dot_nt(qrows[h], kc[:, hsl[h]]), -jnp.inf) for h in range(heads)]
    ms = [jnp.maximum(jnp.max(sp, axis=-1, keepdims=True), jnp.max(sc, axis=-1, keepdims=True))
          for sp, sc in zip(s_past, s_cur)]
    p_past = [jnp.exp(sp - m) for sp, m in zip(s_past, ms)]
    p_cur = [jnp.exp(sc - m) for sc, m in zip(s_cur, ms)]
    invs = [1.0 / (jnp.sum(pp, axis=-1, keepdims=True) + jnp.sum(pc, axis=-1, keepdims=True))
            for pp, pc in zip(p_past, p_cur)]

    def diff(pp, inv):
        pn = pp * inv
        return pn[0:tq] - lam * pn[tq:2 * tq]

    for h in range(heads):
        o = (_dot(diff(p_past[h], invs[h]).astype(BF16), head_rows(v_refs, h))
             + _dot(diff(p_cur[h], invs[h]), vc[:, hsl[h]].astype(F32)))
        o_ref[0, :, hsl[h]] = _rms(o, nw) * (1.0 - lam_init)


def attn_sample(q16, k16, v16, cache_k, cache_v, page_table, lam, norm_w, *, layer, heads, dh, lam_init):
    b, tq, w = q16.shape
    dv = norm_w.shape[-1]
    npages = page_table.shape[1]
    prow = cache_k.shape[2]
    kern = functools.partial(_attn_sample_kernel, npages=npages, tq=tq, heads=heads, dh=dh, dv=dv,
                             page=prow // heads, lam_init=lam_init)
    cur = pl.BlockSpec((1, tq, w), lambda i, pt: (i, 0, 0))

    def page_spec(j):
        return pl.BlockSpec((1, 1, prow, dv), lambda i, pt: (layer, pt[i, j], 0, 0))

    grid_spec = pltpu.PrefetchScalarGridSpec(
        num_scalar_prefetch=1,
        grid=(b,),
        in_specs=[cur, cur, cur,
                  pl.BlockSpec(lam.shape, lambda i, pt: (0, 0)),
                  pl.BlockSpec((1, dv), lambda i, pt: (0, 0))]
        + [page_spec(j) for j in range(npages)] * 2,
        out_specs=pl.BlockSpec((1, tq, w), lambda i, pt: (i, 0, 0)),
    )
    return pl.pallas_call(
        kern,
        grid_spec=grid_spec,
        out_shape=jax.ShapeDtypeStruct((b, tq, w), F32),
        compiler_params=_cparams(("parallel",)),
        name="attn_sample",
    )(page_table, q16, k16, v16, lam, norm_w.reshape(1, dv), *([cache_k] * npages), *([cache_v] * npages))


def _merge_kernel(x_ref, oa_ref, ob_ref, oc_ref, g0_ref, g1_ref, g2_ref, wb_ref, wo_ref, post_ref, o_ref):
    y = None
    for i, (o_i, g_i) in enumerate(((oa_ref, g0_ref), (ob_ref, g1_ref), (oc_ref, g2_ref))):
        ys = _dot(o_i[...].astype(BF16), wb_ref[i])
        t = _sigmoid(g_i[...]) * ys
        y = t if y is None else y + t
    y2 = _dot(y.astype(BF16), wo_ref[...])
    o_ref[...] = x_ref[...] + _rms(y2, post_ref[...])


def merge(x, o_a, o_b, o_c, z, w_branch_bf, w_out_bf, post_w, *, layer, tm, gate_blk):
    n, d = x.shape
    bw = o_a.shape[-1]
    nb = w_branch_bf.shape[1]
    ospec = pl.BlockSpec((tm, bw), lambda i: (i, 0))
    gspec = lambda k: pl.BlockSpec((tm, d), lambda i: (i, gate_blk + k))
    return pl.pallas_call(
        _merge_kernel,
        grid=(n // tm,),
        in_specs=[
            pl.BlockSpec((tm, d), lambda i: (i, 0)),
            ospec, ospec, ospec, gspec(0), gspec(1), gspec(2),
            pl.BlockSpec((None, nb, bw, d), lambda i: (layer, 0, 0, 0)),
            pl.BlockSpec((None, d, d), lambda i: (layer, 0, 0)),
            pl.BlockSpec((1, d), lambda i: (0, 0)),
        ],
        out_specs=pl.BlockSpec((tm, d), lambda i: (i, 0)),
        out_shape=jax.ShapeDtypeStruct((n, d), F32),
        compiler_params=_cparams(("parallel",)),
        name="merge",
    )(x, o_a, o_b, o_c, z, z, z, w_branch_bf, w_out_bf, post_w.reshape(1, d))


ROPE_THETA = 500000.0
SCAN_CHUNK = 64
GDN_CHUNKS_PER_STEP = 4
GDN_INSTANCES = 16
HGRN_SUBBLOCK = 16
ROW_TILE = 1024
FF_TILE = 256
PROJ_COL_TILES = 4
ATTN_TILE = 512
ATTN_QCHUNK = 256
PREP_TILE = 512
MERGE_TILE = 512
SMALL_COLS = 128


def _split_w_in(w_in_l, sizes):
    offs = [0]
    for s in sizes:
        offs.append(offs[-1] + s)
    seg = lambda i: w_in_l[:, offs[i]:offs[i + 1]]
    main = jnp.concatenate([seg(11), seg(0), seg(1), seg(4), seg(5), seg(6), seg(7), seg(8), seg(9), seg(10)], axis=1)
    small = jnp.concatenate([seg(2), seg(3)], axis=1)
    small = jnp.pad(small, ((0, 0), (0, SMALL_COLS - small.shape[1])))
    return main.astype(BF16), small.astype(BF16)


def _row_tile(n, pref):
    return pref if n % pref == 0 else n


def kernel(x_prompt, x_sample, state_gdn, state_gdn_conv, state_hgrn, cache_k, cache_v, page_table,
           ffn1_norm_pre, ffn1_norm_post, ffn1_w_in, ffn1_w_out, mix_norm_pre, mix_norm_post, w_in,
           gdn_conv_w, gdn_a_log, gdn_dt_bias, gdn_norm_w, hgrn_lb_raw, hgrn_norm_w, diff_lambda,
           diff_norm_w, w_branch, w_out, ffn2_norm_pre, ffn2_norm_post, ffn2_w_in, ffn2_w_out):
    depth = w_in.shape[0]
    bp, tp, d = x_prompt.shape
    bs, ts, _ = x_sample.shape
    _, _, gh, gdk, gdv = state_gdn.shape
    cch = state_gdn_conv.shape[-1]
    _, _, hh, hdk, hdv = state_hgrn.shape
    _, n_pool, page, ah, adh2 = cache_k.shape
    adh = adh2 // 2
    adv = cache_v.shape[-1]
    aw = ah * adv
    assert ah * adh2 == aw and hh * hdk == aw and hh * hdv == aw and gh * gdv == aw and cch == 3 * aw and d == 2 * aw
    sizes = (cch, gh * gdv, gh, gh, hh * hdk, hh * hdk, hh * hdv, hh * hdv, ah * adh2, ah * adh2, aw, 3 * d)
    gate_blk, qkv_blk, za_blk, hq_blk, aq_blk = 0, (3 * d) // cch, (3 * d + cch) // aw, (3 * d + cch) // aw + 1, (3 * d + cch) // aw + 5
    past_len = page_table.shape[1] * page
    ck = cache_k.reshape(depth, n_pool, page * ah, adh2)
    cv = cache_v.reshape(depth, n_pool, page * ah, adv)
    tabs_p = rope_tables(tp, 0, adh, ROPE_THETA)
    tabs_s = rope_tables(ts, past_len, adh, ROPE_THETA)
    zeros_conv = jnp.zeros((1, bp, 3, cch), F32)
    zeros_gdn = jnp.zeros((1, bp, gh, gdk, gdv), F32)
    zeros_hgrn = jnp.zeros((1, bp, hh, hdk, hdv), F32)

    f1_in, f1_out = ffn1_w_in.astype(BF16), ffn1_w_out.astype(BF16)
    f2_in, f2_out = ffn2_w_in.astype(BF16), ffn2_w_out.astype(BF16)
    wb, wo = w_branch.astype(BF16), w_out.astype(BF16)

    def run_layer(l, x, b, t, conv_state, gdn_state, hgrn_state, sl, tabs, prompt, w_main, w_small):
        n = b * t
        tm = _row_tile(n, ROW_TILE)
        lam_init = 0.8 - 0.6 * math.exp(-0.3 * l)
        x = ffn(x, ffn1_norm_pre[l], f1_in, f1_out, ffn1_norm_post[l], layer=l, tm=tm, tf=FF_TILE)
        z, zs = proj(x, mix_norm_pre[l], w_main, w_small, tm=tm, tn=w_main.shape[1] // PROJ_COL_TILES)
        z3 = z.reshape(b, t, z.shape[-1])
        zs3 = zs.reshape(b, t, SMALL_COLS)
        c = math.gcd(SCAN_CHUNK, t)
        o_a, new_gdn, new_conv = gdn(z3, zs3, conv_state, gdn_state, gdn_conv_w[l], gdn_a_log[l], gdn_dt_bias[l],
                                     gdn_norm_w[l], state_layer=sl, c=c, nc=min(GDN_CHUNKS_PER_STEP, t // c),
                                     bb=math.gcd(b, max(1, GDN_INSTANCES // (gh * min(GDN_CHUNKS_PER_STEP, t // c)))),
                                     qkv_blk=qkv_blk, za_blk=za_blk)
        o_b, new_hgrn = hgrn(z3, hgrn_lb_raw, hgrn_state, hgrn_norm_w[l], state_layer=sl, c=c, sb=min(HGRN_SUBBLOCK, c),
                             layer=l, q_blk=hq_blk)
        q16, k32, k16, v32, v16 = qkv_prep(z3, tabs, q_blk=aq_blk, dh=adh, heads=ah, tm=_row_tile(t, PREP_TILE),
                                           v_transposed=prompt)
        if prompt:
            o_c = attn_prompt(q16, k16, v16, diff_lambda[l], diff_norm_w[l], tq=_row_tile(t, ATTN_TILE), dh=adh,
                              lam_init=lam_init)
        else:
            o_c = attn_sample(q16, k16, v16, ck, cv, page_table, diff_lambda[l], diff_norm_w[l], layer=l, heads=ah, dh=adh,
                              lam_init=lam_init)
        x = merge(x, o_a.reshape(n, aw), o_b.reshape(n, aw), o_c.reshape(n, aw), z, wb, wo, mix_norm_post[l],
                  layer=l, tm=_row_tile(n, MERGE_TILE), gate_blk=gate_blk)
        x = ffn(x, ffn2_norm_pre[l], f2_in, f2_out, ffn2_norm_post[l], layer=l, tm=tm, tf=FF_TILE)
        return x, (new_gdn, new_conv, new_hgrn, k32.reshape(b, t, ah, adh2), v32.reshape(b, t, ah, adv))

    xp = x_prompt.reshape(bp * tp, d)
    xs = x_sample.reshape(bs * ts, d)
    outs_p, outs_s = [], []
    for l in range(depth):
        w_main, w_small = _split_w_in(w_in[l], sizes)
        xp, st = run_layer(l, xp, bp, tp, zeros_conv, zeros_gdn, zeros_hgrn, 0, tabs_p, True, w_main, w_small)
        outs_p.append(st)
        xs, ss = run_layer(l, xs, bs, ts, state_gdn_conv, state_gdn, state_hgrn, l, tabs_s, False, w_main, w_small)
        outs_s.append(ss)

    stack = lambda outs, i: jnp.stack([o[i] for o in outs])
    return (xp.reshape(bp, tp, d), xs.reshape(bs, ts, d),
            stack(outs_p, 0), stack(outs_p, 1), stack(outs_p, 2), stack(outs_p, 3), stack(outs_p, 4),
            stack(outs_s, 0), stack(outs_s, 1), stack(outs_s, 2), stack(outs_s, 3), stack(outs_s, 4))
```

```python
import functools
import math

import jax
import jax.numpy as jnp
from jax import lax
from jax.experimental import pallas as pl
from jax.experimental.pallas import tpu as pltpu

F32 = jnp.float32
BF16 = jnp.bfloat16
EPS = 1e-6
HI = lax.Precision.HIGHEST

VMEM_LIMIT_BYTES = 56 * 1024 * 1024


def _cparams(sem):
    return pltpu.CompilerParams(dimension_semantics=sem, vmem_limit_bytes=VMEM_LIMIT_BYTES)


def _rms(x, w):
    return x * lax.rsqrt(jnp.mean(x * x, axis=-1, keepdims=True) + EPS) * w


def _sigmoid(x):
    return 1.0 / (1.0 + jnp.exp(-x))


def _silu(x):
    return x * _sigmoid(x)


def _softplus(x):
    return jnp.maximum(x, 0.0) + jnp.log1p(jnp.exp(-jnp.abs(x)))


def _ffn_kernel(x_ref, pre_ref, wi_ref, wo_ref, post_ref, o_ref, acc_scr, *, tf):
    dff = wo_ref.shape[0]
    h = _rms(x_ref[...], pre_ref[...]).astype(BF16)
    for j in range(dff // tf):
        cols = slice(j * tf, (j + 1) * tf)
        g = _dot(h, wi_ref[:, cols])
        u = _dot(h, wi_ref[:, dff + j * tf:dff + (j + 1) * tf])
        part = _dot((_silu(g) * u).astype(BF16), wo_ref[cols, :])
        if j == 0:
            acc_scr[...] = part
        else:
            acc_scr[...] += part
    o_ref[...] = x_ref[...] + 0.5 * _rms(acc_scr[...], post_ref[...])


def ffn(x, pre_w, w_in_bf, w_out_bf, post_w, *, layer, tm, tf):
    n, d = x.shape
    dff = w_out_bf.shape[1]
    resident = dict(pipeline_mode=pl.Buffered(1))
    return pl.pallas_call(
        functools.partial(_ffn_kernel, tf=tf),
        grid=(n // tm,),
        in_specs=[
            pl.BlockSpec((tm, d), lambda i: (i, 0)),
            pl.BlockSpec((1, d), lambda i: (0, 0)),
            pl.BlockSpec((None, d, 2 * dff), lambda i: (layer, 0, 0), **resident),
            pl.BlockSpec((None, dff, d), lambda i: (layer, 0, 0), **resident),
            pl.BlockSpec((1, d), lambda i: (0, 0)),
        ],
        out_specs=pl.BlockSpec((tm, d), lambda i: (i, 0)),
        out_shape=jax.ShapeDtypeStruct((n, d), F32),
        scratch_shapes=[pltpu.VMEM((tm, d), F32)],
        compiler_params=_cparams(("parallel",)),
        name="ffn",
    )(x, pre_w.reshape(1, d), w_in_bf, w_out_bf, post_w.reshape(1, d))


def _proj_kernel(x_ref, pre_ref, w_ref, ws_ref, z_ref, zs_ref, h_scr):
    j = pl.program_id(1)

    @pl.when(j == 0)
    def _():
        h = _rms(x_ref[...], pre_ref[...]).astype(BF16)
        h_scr[...] = h
        zs_ref[...] = jnp.dot(h, ws_ref[...], preferred_element_type=F32)

    z_ref[...] = jnp.dot(h_scr[...], w_ref[...], preferred_element_type=F32)


def proj(x, pre_w, w_main_bf, w_small_bf, *, tm, tn):
    n, d = x.shape
    cols = w_main_bf.shape[1]
    sc = w_small_bf.shape[1]
    return pl.pallas_call(
        _proj_kernel,
        grid=(n // tm, cols // tn),
        in_specs=[
            pl.BlockSpec((tm, d), lambda i, j: (i, 0)),
            pl.BlockSpec((1, d), lambda i, j: (0, 0)),
            pl.BlockSpec((d, tn), lambda i, j: (0, j)),
            pl.BlockSpec((d, sc), lambda i, j: (0, 0)),
        ],
        out_specs=[
            pl.BlockSpec((tm, tn), lambda i, j: (i, j)),
            pl.BlockSpec((tm, sc), lambda i, j: (i, 0)),
        ],
        out_shape=[jax.ShapeDtypeStruct((n, cols), F32), jax.ShapeDtypeStruct((n, sc), F32)],
        scratch_shapes=[pltpu.VMEM((tm, d), BF16)],
        compiler_params=_cparams(("parallel", "arbitrary")),
        name="proj",
    )(x, pre_w.reshape(1, d), w_main_bf, w_small_bf)


def _dot(a, b, precision=None):
    return jnp.dot(a, b, preferred_element_type=F32, precision=precision)


def _dot_nt(a, b, precision=None):
    return lax.dot_general(a, b, (((1,), (1,)), ((), ())), preferred_element_type=F32, precision=precision)


def _dot_tn(a, b, precision=None):
    return lax.dot_general(a, b, (((0,), (0,)), ((), ())), preferred_element_type=F32, precision=precision)


def _iota2(shape, dim):
    return lax.broadcasted_iota(jnp.int32, shape, dim)


def _split_bf16(x):
    hi = x.astype(BF16)
    return hi, (x - hi.astype(F32)).astype(BF16)


def _dot3(a, b):
    ah, al = _split_bf16(a)
    bh, bl = _split_bf16(b)
    return _dot(ah, bh) + (_dot(ah, bl) + _dot(al, bh))


def _tri_inverse_all(lmats, c):
    ri = _iota2((c, c), 0)
    ci = _iota2((c, c), 1)
    eye = jnp.where(ri == ci, 1.0, 0.0).astype(F32)
    nb = min(16, c)
    sh = int(math.log2(nb))
    same = (ri >> sh) == (ci >> sh)
    ps = [jnp.where(same, lm, 0.0) for lm in lmats]
    ts = [eye - p for p in ps]
    k = 2
    while k < nb:
        ps = [_dot3(p, p) for p in ps]
        ts = [_dot3(t, eye + p) for t, p in zip(ts, ps)]
        k *= 2
    blk = nb
    while blk < c:
        s1 = int(math.log2(blk))
        offm = ((ri >> (s1 + 1)) == (ci >> (s1 + 1))) & ((ri >> s1) != (ci >> s1))
        tl = [_dot3(t, jnp.where(offm, lm, 0.0)) for t, lm in zip(ts, lmats)]
        ts = [t - _dot3(x, t) for t, x in zip(ts, tl)]
        blk *= 2
    return ts


def _gdn_kernel(qkv_ref, za_ref, zs_ref, cs_ref, s0_ref, cw_ref, alog_ref, dtb_ref, nw_ref,
                o_ref, sout_ref, cout_ref, xbuf, s_scr, *, c, nc, bb, heads, dk, dv):
    n = pl.program_id(1)
    nlast = pl.num_programs(1) - 1
    nqk = heads * dk
    r = nc * c

    @pl.when(n == 0)
    def _():
        xbuf[:, 5:8, :] = cs_ref[...]
        s_scr[...] = s0_ref[...]

    xbuf[:, 8:8 + r, :] = qkv_ref[...]

    @pl.when(n == nlast)
    def _():
        cout_ref[...] = xbuf[:, 5 + r:8 + r, :]

    ri = _iota2((c, c), 0)
    ci = _iota2((c, c), 1)
    incl = ri >= ci
    strict = ri > ci
    rr = _iota2((r, r), 0)
    rc = _iota2((r, r), 1)
    sh = int(math.log2(c))
    blocktril = jnp.where((rr >= rc) & ((rr >> sh) == (rc >> sh)), 1.0, 0.0).astype(F32)
    nw = nw_ref[...]

    inst = []
    for bi in range(bb):
        y = xbuf[bi, 5:5 + r, :] * cw_ref[0:1, :]
        for j in range(1, 4):
            y = y + xbuf[bi, 5 + j:5 + j + r, :] * cw_ref[j:j + 1, :]
        y = _silu(y)
        zs = zs_ref[bi]
        beta_all = _sigmoid(zs)
        g_all = -jnp.exp(alog_ref[...]) * _softplus(zs + dtb_ref[...])
        gcum_all = _dot3(blocktril, g_all)
        gcum_t = gcum_all.T
        for i in range(nc):
            rows = slice(i * c, (i + 1) * c)
            for h in range(heads):
                inst.append(dict(
                    bi=bi, i=i, h=h,
                    q=y[rows, h * dk:(h + 1) * dk],
                    k=y[rows, nqk + h * dk:nqk + (h + 1) * dk],
                    v=y[rows, 2 * nqk + h * dv:2 * nqk + (h + 1) * dv],
                    beta=beta_all[rows, h:h + 1],
                    gc=gcum_all[rows, heads + h:heads + h + 1],
                    grow=gcum_t[heads + h:heads + h + 1, rows]))
    xbuf[:, 5:8, :] = xbuf[:, 5 + r:8 + r, :]

    for d in inst:
        q, k = d["q"], d["k"]
        d["q"] = q * lax.rsqrt(jnp.sum(q * q, axis=-1, keepdims=True) + EPS) * (dk ** -0.5)
        d["k"] = k * lax.rsqrt(jnp.sum(k * k, axis=-1, keepdims=True) + EPS)
        d["k16"] = d["k"].astype(BF16)
        d["kb"] = d["k"] * d["beta"]
        d["decay"] = jnp.where(incl, jnp.exp(jnp.minimum(d["gc"] - d["grow"], 0.0)), 0.0)
    for d in inst:
        d["lmat"] = jnp.where(strict, _dot_nt(d["kb"].astype(BF16), d["k16"]) * d["decay"], 0.0)
    tinvs = _tri_inverse_all([d["lmat"] for d in inst], c)
    for d, tinv in zip(inst, tinvs):
        eg = jnp.exp(d["gc"])
        sol = _dot3(tinv, jnp.concatenate([d["v"] * d["beta"], d["kb"] * eg], axis=-1))
        d["u"] = sol[:, :dv]
        d["w16"] = sol[:, dv:].astype(BF16)
        d["attn16"] = (_dot_nt(d["q"].astype(BF16), d["k16"]) * d["decay"]).astype(BF16)
        d["qe16"] = (d["q"] * eg).astype(BF16)
        gl = d["gc"][c - 1:c, :]
        d["kdec16"] = (d["k"] * jnp.exp(gl - d["gc"])).astype(BF16)
        d["egl"] = jnp.exp(gl)

    state = {(bi, h): s_scr[bi, h] for bi in range(bb) for h in range(heads)}
    for i in range(nc):
        for d in inst:
            if d["i"] != i:
                continue
            bi, h = d["bi"], d["h"]
            s = state[(bi, h)]
            s16 = s.astype(BF16)
            v_new = d["u"] - _dot(d["w16"], s16)
            vn16 = v_new.astype(BF16)
            o = _dot(d["qe16"], s16) + _dot(d["attn16"], vn16)
            state[(bi, h)] = s * d["egl"] + _dot_tn(d["kdec16"], vn16)
            za = za_ref[bi, i * c:(i + 1) * c, h * dv:(h + 1) * dv]
            o_ref[bi, i * c:(i + 1) * c, h * dv:(h + 1) * dv] = _rms(o, nw) * _silu(za)
    for (bi, h), s in state.items():
        s_scr[bi, h] = s

    @pl.when(n == nlast)
    def _():
        sout_ref[...] = s_scr[...]


def gdn(z3, zs3, conv_state, state, conv_w, a_log, dt_bias, norm_w, *, state_layer, c, nc, bb, qkv_blk, za_blk,
        layer, depth, prev):
    b, t, _ = z3.shape
    _, _, heads, dk, dv = state.shape
    cch = conv_state.shape[-1]
    sc = zs3.shape[-1]
    r = nc * c
    pad = jnp.zeros((1, sc), F32)
    alog_p = lax.dynamic_update_slice(pad, a_log.reshape(1, heads).astype(F32), (0, heads))
    dtb_p = lax.dynamic_update_slice(pad, dt_bias.reshape(1, heads).astype(F32), (0, heads))
    kern = functools.partial(_gdn_kernel, c=c, nc=nc, bb=bb, heads=heads, dk=dk, dv=dv)
    prev_ops, prev_specs, aliases = _alias_plan(prev, 9, (1,))
    return pl.pallas_call(
        _drop_refs(kern, 9, len(prev_ops)),
        grid=(b // bb, t // r),
        in_specs=[
            pl.BlockSpec((bb, r, cch), lambda i, n: (i, n, qkv_blk)),
            pl.BlockSpec((bb, r, heads * dv), lambda i, n: (i, n, za_blk)),
            pl.BlockSpec((bb, r, sc), lambda i, n: (i, n, 0)),
            pl.BlockSpec((None, bb, 3, cch), lambda i, n: (state_layer, i, 0, 0)),
            pl.BlockSpec((None, bb, heads, dk, dv), lambda i, n: (state_layer, i, 0, 0, 0)),
            pl.BlockSpec((4, cch), lambda i, n: (0, 0)),
            pl.BlockSpec((1, sc), lambda i, n: (0, 0)),
            pl.BlockSpec((1, sc), lambda i, n: (0, 0)),
            pl.BlockSpec((1, dv), lambda i, n: (0, 0)),
        ] + prev_specs,
        out_specs=[
            pl.BlockSpec((bb, r, heads * dv), lambda i, n: (i, n, 0)),
            pl.BlockSpec((None, bb, heads, dk, dv), lambda i, n: (layer, i, 0, 0, 0)),
            pl.BlockSpec((bb, 3, cch), lambda i, n: (i, 0, 0)),
        ],
        out_shape=[
            jax.ShapeDtypeStruct((b, t, heads * dv), F32),
            jax.ShapeDtypeStruct((depth, b, heads, dk, dv), F32),
            jax.ShapeDtypeStruct((b, 3, cch), F32),
        ],
        scratch_shapes=[pltpu.VMEM((bb, 8 + r, cch), F32), pltpu.VMEM((bb, heads, dk, dv), F32)],
        input_output_aliases=aliases,
        compiler_params=_cparams(("parallel", "arbitrary")),
        name="gdn",
    )(z3, z3, zs3, conv_state, state, conv_w, alog_p, dtb_p, norm_w.reshape(1, dv), *prev_ops)


SB = 8


def _dot_cumsum(tril, x):
    hi = x.astype(BF16)
    r1 = x - hi.astype(F32)
    mid = r1.astype(BF16)
    lo = (r1 - mid.astype(F32)).astype(BF16)
    t16 = tril.astype(BF16)
    return _dot(t16, hi) + (_dot(t16, mid) + _dot(t16, lo))


def _hgrn_kernel(q_ref, f_ref, i_ref, g_ref, lbraw_ref, s0_ref, nw_ref, o_ref, sout_ref,
                 st_scr, *, c, nc, heads, dk, dv, layer):
    n = pl.program_id(1)
    nlast = pl.num_programs(1) - 1

    @pl.when(n == 0)
    def _():
        for h in range(heads):
            st_scr[h] = s0_ref[0, h].T

    raw = lbraw_ref[...]
    e = jnp.exp(raw - jnp.max(raw, axis=0, keepdims=True))
    sm = e / jnp.sum(e, axis=0, keepdims=True)
    lb = jnp.zeros((1, heads * dk), F32)
    for l in range(1, layer + 1):
        lb = lb + sm[l:l + 1, :]

    zf = f_ref[0]
    logsig = jnp.minimum(zf, 0.0) - jnp.log1p(jnp.exp(-jnp.abs(zf)))
    la = jnp.log(lb)
    lbb = jnp.log1p(-lb) + logsig
    logf = jnp.maximum(la, lbb) + jnp.log1p(jnp.exp(-jnp.abs(la - lbb)))
    kh = (1.0 - lb) * _sigmoid(-zf)
    qh = _silu(q_ref[0]) * (dk ** -0.5)
    vv = i_ref[0]
    gate = _sigmoid(g_ref[0])

    r = nc * c
    ri = _iota2((r, r), 0)
    ci = _iota2((r, r), 1)
    sh = int(math.log2(c))
    tril = jnp.where((ri >= ci) & ((ri >> sh) == (ci >> sh)), 1.0, 0.0).astype(F32)
    bcum = _dot_cumsum(tril, logf)
    fgate = jnp.exp(logf)
    nsb = c // SB
    sub = _iota2((nsb, SB, dk), 1)
    nw = nw_ref[...]
    g3 = lambda x: x.reshape(nsb, SB, x.shape[-1])
    states = [st_scr[h] for h in range(heads)]

    for ic, h in [(ic, h) for ic in range(nc) for h in range(heads)]:
        rows = slice(ic * c, (ic + 1) * c)
        ks = slice(h * dk, (h + 1) * dk)
        vs = slice(h * dv, (h + 1) * dv)
        q = qh[rows, ks]
        k = kh[rows, ks]
        b = bcum[rows, ks]
        v = vv[rows, vs]
        st = states[h]
        o = _dot_nt((q * jnp.exp(b)).astype(BF16), st.astype(BF16))
        q3, k3, v3, f3 = g3(q), g3(k), g3(v), g3(fgate[rows, ks])
        o3 = jnp.sum(q3 * k3, axis=-1, keepdims=True) * v3
        e = None
        for d in range(1, SB):
            fd = f3 if d == 1 else pltpu.roll(f3, d - 1, 1)
            e = fd if e is None else e * fd
            m = jnp.where(sub >= d, q3 * pltpu.roll(k3, d, 1) * e, 0.0)
            o3 = o3 + jnp.sum(m, axis=-1, keepdims=True) * pltpu.roll(v3, d, 1)
        o = o + o3.reshape(c, dv)
        if nsb > 1:
            b3 = g3(b)
            rend = jnp.broadcast_to(b3[:, SB - 1:SB, :], b3.shape).reshape(c, dk)
            khat = k * jnp.exp(jnp.minimum(rend - b, 0.0))
            zq = jnp.zeros((c, dk), F32)
            qs, kk = [], []
            for jb in range(nsb - 1):
                lo, hi = jb * SB, (jb + 1) * SB
                rj = b[hi - 1:hi, :]
                qpart = q[hi:, :] * jnp.exp(jnp.minimum(b[hi:, :] - rj, 0.0))
                qs.append(jnp.concatenate([zq[:hi], qpart], axis=0))
                kparts = [khat[lo:hi]] if lo == 0 else [zq[:lo], khat[lo:hi]]
                kk.append(jnp.concatenate(kparts + [zq[hi:]], axis=0))
            amat = _dot_nt(jnp.concatenate(qs, axis=-1).astype(BF16), jnp.concatenate(kk, axis=-1).astype(BF16))
            o = o + _dot(amat.astype(BF16), v.astype(BF16))
        bl = b[c - 1:c, :]
        kdec = k * jnp.exp(bl - b)
        states[h] = st * jnp.exp(bl) + _dot_tn(v.astype(BF16), kdec.astype(BF16))
        o_ref[0, rows, vs] = _rms(o, nw) * gate[rows, vs]
    for h in range(heads):
        st_scr[h] = states[h]

    @pl.when(n == nlast)
    def _():
        for h in range(heads):
            sout_ref[0, h] = st_scr[h].T


def hgrn(z3, lb_raw, state, norm_w, *, state_layer, c, nc, layer, q_blk, prev):
    b, t, _ = z3.shape
    _, _, heads, dk, dv = state.shape
    depth = lb_raw.shape[0]
    wk = heads * dk
    assert c % SB == 0
    kern = functools.partial(_hgrn_kernel, c=c, nc=nc, heads=heads, dk=dk, dv=dv, layer=layer)
    r = nc * c
    zspec = lambda off: pl.BlockSpec((1, r, wk), lambda i, n: (i, n, q_blk + off))
    prev_ops, prev_specs, aliases = _alias_plan(prev, 7, (1,))
    return pl.pallas_call(
        _drop_refs(kern, 7, len(prev_ops)),
        grid=(b, t // r),
        in_specs=[
            zspec(0), zspec(1), zspec(2), zspec(3),
            pl.BlockSpec((depth, wk), lambda i, n: (0, 0)),
            pl.BlockSpec((None, 1, heads, dk, dv), lambda i, n: (state_layer, i, 0, 0, 0)),
            pl.BlockSpec((1, dv), lambda i, n: (0, 0)),
        ] + prev_specs,
        out_specs=[
            pl.BlockSpec((1, r, heads * dv), lambda i, n: (i, n, 0)),
            pl.BlockSpec((None, 1, heads, dk, dv), lambda i, n: (layer, i, 0, 0, 0)),
        ],
        out_shape=[
            jax.ShapeDtypeStruct((b, t, heads * dv), F32),
            jax.ShapeDtypeStruct((depth, b, heads, dk, dv), F32),
        ],
        scratch_shapes=[pltpu.VMEM((heads, dv, dk), F32)],
        input_output_aliases=aliases,
        compiler_params=_cparams(("parallel", "arbitrary")),
        name="hgrn",
    )(z3, z3, z3, z3, lb_raw, state, norm_w.reshape(1, dv), *prev_ops)


def rope_tables(t, pos0, dh, theta):
    rd = dh // 4
    half = rd // 2
    inv = jnp.power(jnp.float32(theta), -jnp.arange(half, dtype=F32) / half)
    ang = (pos0 + jnp.arange(t)).astype(F32)[:, None] * inv
    cos, sin = jnp.cos(ang), jnp.sin(ang)
    d = jnp.arange(128) % dh
    f = d % half
    cc = jnp.where(d < rd, cos[:, f], 1.0)
    s1 = jnp.where(d < half, -sin[:, f], 0.0)
    s2 = jnp.where((d >= half) & (d < rd), sin[:, f], 0.0)
    return cc.astype(F32), s1.astype(F32), s2.astype(F32)


def _prep_kernel(q_ref, k_ref, v_ref, c_ref, s1_ref, s2_ref, q16_ref, k32_ref, k16_ref, v32_ref, v16_ref,
                 *, scale, half, heads, v_transposed):
    tm, w = q_ref.shape[1:]
    hw = w // heads
    reps = w // c_ref.shape[-1]
    cc = jnp.concatenate([c_ref[...]] * reps, axis=-1)
    s1 = jnp.concatenate([s1_ref[...]] * reps, axis=-1)
    s2 = jnp.concatenate([s2_ref[...]] * reps, axis=-1)

    def rope(x):
        return x * cc + pltpu.roll(x, w - half, 1) * s1 + pltpu.roll(x, half, 1) * s2

    q16_ref[0] = (rope(q_ref[0]) * scale).astype(BF16)
    k = rope(k_ref[0])
    k16_ref[0] = k.astype(BF16)
    v = v_ref[0]
    v16_ref[0] = (v.T if v_transposed else v).astype(BF16)
    for h in range(heads):
        k32_ref[0, pl.ds(h, tm, stride=heads), :] = k[:, h * hw:(h + 1) * hw]
        v32_ref[0, pl.ds(h, tm, stride=heads), :] = v[:, h * hw:(h + 1) * hw]


def _drop_refs(kern, start, count):
    def wrapped(*refs):
        return kern(*refs[:start], *refs[start + count:])
    return wrapped


def _alias_plan(prev, n_in, out_idx):
    prev = [] if prev is None else list(prev)
    specs = [pl.BlockSpec(memory_space=pl.ANY)] * len(prev)
    return prev, specs, {n_in + k: out_idx[k] for k in range(len(prev))}


def qkv_prep(z3, tables, *, q_blk, dh, heads, tm, v_transposed, layer, depth, prev):
    b, t, _ = z3.shape
    w = tables[0].shape[-1] * (heads * 2 * dh // tables[0].shape[-1])
    hw = w // heads
    cc, s1, s2 = tables
    kern = functools.partial(_prep_kernel, scale=dh ** -0.5, half=dh // 8, heads=heads, v_transposed=v_transposed)
    zspec = lambda off: pl.BlockSpec((1, tm, w), lambda i, n: (i, n, q_blk + off))
    tspec = pl.BlockSpec((tm, cc.shape[-1]), lambda i, n: (n, 0))
    ospec = pl.BlockSpec((1, tm, w), lambda i, n: (i, n, 0))
    cspec = pl.BlockSpec((None, 1, tm * heads, hw), lambda i, n: (layer, i, n, 0))
    sd = lambda dt: jax.ShapeDtypeStruct((b, t, w), dt)
    sc = jax.ShapeDtypeStruct((depth, b, t * heads, hw), F32)
    vspec = pl.BlockSpec((1, w, tm), lambda i, n: (i, 0, n)) if v_transposed else ospec
    vsd = jax.ShapeDtypeStruct((b, w, t), BF16) if v_transposed else sd(BF16)
    prev_ops, prev_specs, aliases = _alias_plan(prev, 6, (1, 3))
    return pl.pallas_call(
        _drop_refs(kern, 6, len(prev_ops)),
        grid=(b, t // tm),
        in_specs=[zspec(0), zspec(1), zspec(2), tspec, tspec, tspec] + prev_specs,
        out_specs=[ospec, cspec, ospec, cspec, vspec],
        out_shape=[sd(BF16), sc, sd(BF16), sc, vsd],
        input_output_aliases=aliases,
        compiler_params=_cparams(("parallel", "parallel")),
        name="qkv_prep",
    )(z3, z3, z3, cc, s1, s2, *prev_ops)


def _lambda(lam_ref, lam_init):
    lm = lam_ref[...]
    a = jnp.sum(lm[0:1, :] * lm[1:2, :], axis=-1, keepdims=True)
    b = jnp.sum(lm[2:3, :] * lm[3:4, :], axis=-1, keepdims=True)
    return jnp.exp(a) - jnp.exp(b) + lam_init


def _attn_prompt_kernel(qt_ref, kt_ref, q_ref, k_ref, vt_ref, lam_ref, nw_ref, o_ref,
                        qs_scr, m_scr, l_scr, acc_scr, *, tq, dh, qchunk, lam_init):
    p = pl.program_id(2)
    qi = qt_ref[p]
    ki = kt_ref[p]

    @pl.when(ki == 0)
    def _():
        q = q_ref[0]
        lane = _iota2(q.shape, 1)
        zero = jnp.zeros_like(q)
        qs_scr[0:tq, :] = jnp.where(lane < dh, q, zero)
        qs_scr[tq:2 * tq, :] = jnp.where(lane >= dh, q, zero)
        m_scr[...] = jnp.full(m_scr.shape, -jnp.inf, F32)
        l_scr[...] = jnp.zeros_like(l_scr)
        acc_scr[...] = jnp.zeros_like(acc_scr)

    def update(diag):
        k = k_ref[0]
        vt = vt_ref[0]
        for c0 in range(0, 2 * tq, qchunk):
            cs = slice(c0, c0 + qchunk)
            st = _dot_nt(k, qs_scr[cs, :])
            if diag:
                key = _iota2(st.shape, 0)
                qry = (_iota2(st.shape, 1) + c0) & (tq - 1)
                st = jnp.where(key <= qry, st, -jnp.inf)
            m_prev = m_scr[:, cs]
            m_new = jnp.maximum(m_prev, jnp.max(st, axis=0, keepdims=True))
            alpha = jnp.exp(m_prev - m_new)
            pr = jnp.exp(st - m_new)
            l_scr[:, cs] = alpha * l_scr[:, cs] + jnp.sum(pr, axis=0, keepdims=True)
            acc_scr[:, cs] = alpha * acc_scr[:, cs] + _dot(vt, pr.astype(BF16))
            m_scr[:, cs] = m_new

    @pl.when(ki < qi)
    def _():
        update(False)

    @pl.when(ki == qi)
    def _():
        update(True)
        o1 = acc_scr[:, 0:tq] / l_scr[:, 0:tq]
        o2 = acc_scr[:, tq:2 * tq] / l_scr[:, tq:2 * tq]
        o = (o1 - _lambda(lam_ref, lam_init) * o2).T
        o_ref[0] = _rms(o, nw_ref[...]) * (1.0 - lam_init)


def attn_prompt(q16, k16, vt16, lam, norm_w, *, tq, dh, lam_init):
    b, t, w = q16.shape
    dv = norm_w.shape[-1]
    heads = w // dv
    nq = t // tq
    pairs = [(i, j) for i in range(nq) for j in range(i + 1)]
    qt = jnp.asarray([pq for pq, _ in pairs], jnp.int32)
    kt = jnp.asarray([pk for _, pk in pairs], jnp.int32)
    kern = functools.partial(_attn_prompt_kernel, tq=tq, dh=dh, qchunk=min(ATTN_QCHUNK, 2 * tq), lam_init=lam_init)
    grid_spec = pltpu.PrefetchScalarGridSpec(
        num_scalar_prefetch=2,
        grid=(b, heads, len(pairs)),
        in_specs=[
            pl.BlockSpec((1, tq, dv), lambda i, h, p, qt, kt: (i, qt[p], h)),
            pl.BlockSpec((1, tq, dv), lambda i, h, p, qt, kt: (i, kt[p], h)),
            pl.BlockSpec((1, dv, tq), lambda i, h, p, qt, kt: (i, h, kt[p])),
            pl.BlockSpec(lam.shape, lambda i, h, p, qt, kt: (0, 0)),
            pl.BlockSpec((1, dv), lambda i, h, p, qt, kt: (0, 0)),
        ],
        out_specs=pl.BlockSpec((1, tq, dv), lambda i, h, p, qt, kt: (i, qt[p], h)),
        scratch_shapes=[
            pltpu.VMEM((2 * tq, dv), BF16),
            pltpu.VMEM((1, 2 * tq), F32),
            pltpu.VMEM((1, 2 * tq), F32),
            pltpu.VMEM((dv, 2 * tq), F32),
        ],
    )
    return pl.pallas_call(
        kern,
        grid_spec=grid_spec,
        out_shape=jax.ShapeDtypeStruct((b, t, w), F32),
        compiler_params=_cparams(("parallel", "parallel", "arbitrary")),
        name="attn_prompt",
    )(qt, kt, q16, k16, vt16, lam, norm_w.reshape(1, dv))


def _attn_sample_kernel(pt_ref, q_ref, kc_ref, vc_ref, lam_ref, nw_ref, *rest,
                        npages, tq, heads, dh, dv, page, lam_init):
    k_refs = rest[:npages]
    v_refs = rest[npages:2 * npages]
    o_ref = rest[2 * npages]
    lam = _lambda(lam_ref, lam_init)
    nw = nw_ref[...]
    q = q_ref[0].astype(F32)
    kc = kc_ref[0]
    vc = vc_ref[0]
    lane = _iota2((tq, dv), 1)
    qidx = _iota2((2 * tq, tq), 0) % tq
    kidx = _iota2((2 * tq, tq), 1)
    hsl = [slice(h * dv, (h + 1) * dv) for h in range(heads)]

    def head_rows(refs, h):
        return jnp.concatenate([r[0, 0, pl.ds(h, page, stride=heads), :].astype(BF16) for r in refs], axis=0)

    qrows = [jnp.concatenate([jnp.where(lane < dh, q[:, hs], 0.0), jnp.where(lane >= dh, q[:, hs], 0.0)],
                             axis=0).astype(BF16) for hs in hsl]
    s_past = [_dot_nt(qrows[h], head_rows(k_refs, h)) for h in range(heads)]
    s_cur = [jnp.where(kidx <= qidx, _dot_nt(qrows[h], kc[:, hsl[h]]), -jnp.inf) for h in range(heads)]
    ms = [jnp.maximum(jnp.max(sp, axis=-1, keepdims=True), jnp.max(sc, axis=-1, keepdims=True))
          for sp, sc in zip(s_past, s_cur)]
    p_past = [jnp.exp(sp - m) for sp, m in zip(s_past, ms)]
    p_cur = [jnp.exp(sc - m) for sc, m in zip(s_cur, ms)]
    invs = [1.0 / (jnp.sum(pp, axis=-1, keepdims=True) + jnp.sum(pc, axis=-1, keepdims=True))
            for pp, pc in zip(p_past, p_cur)]

    def diff(pp, inv):
        pn = pp * inv
        return pn[0:tq] - lam * pn[tq:2 * tq]

    for h in range(heads):
        o = (_dot(diff(p_past[h], invs[h]).astype(BF16), head_rows(v_refs, h))
             + _dot(diff(p_cur[h], invs[h]), vc[:, hsl[h]].astype(F32)))
        o_ref[0, :, hsl[h]] = _rms(o, nw) * (1.0 - lam_init)


def attn_sample(q16, k16, v16, cache_k, cache_v, page_table, lam, norm_w, *, layer, heads, dh, lam_init):
    b, tq, w = q16.shape
    dv = norm_w.shape[-1]
    npages = page_table.shape[1]
    prow = cache_k.shape[2]
    kern = functools.partial(_attn_sample_kernel, npages=npages, tq=tq, heads=heads, dh=dh, dv=dv,
                             page=prow // heads, lam_init=lam_init)
    cur = pl.BlockSpec((1, tq, w), lambda i, pt: (i, 0, 0))

    def page_spec(j):
        return pl.BlockSpec((1, 1, prow, dv), lambda i, pt: (layer, pt[i, j], 0, 0))

    grid_spec = pltpu.PrefetchScalarGridSpec(
        num_scalar_prefetch=1,
        grid=(b,),
        in_specs=[cur, cur, cur,
                  pl.BlockSpec(lam.shape, lambda i, pt: (0, 0)),
                  pl.BlockSpec((1, dv), lambda i, pt: (0, 0))]
        + [page_spec(j) for j in range(npages)] * 2,
        out_specs=pl.BlockSpec((1, tq, w), lambda i, pt: (i, 0, 0)),
    )
    return pl.pallas_call(
        kern,
        grid_spec=grid_spec,
        out_shape=jax.ShapeDtypeStruct((b, tq, w), F32),
        compiler_params=_cparams(("parallel",)),
        name="attn_sample",
    )(page_table, q16, k16, v16, lam, norm_w.reshape(1, dv), *([cache_k] * npages), *([cache_v] * npages))


def _merge_kernel(x_ref, oa_ref, ob_ref, oc_ref, g0_ref, g1_ref, g2_ref, wb_ref, wo_ref, post_ref, o_ref):
    y = None
    for i, (o_i, g_i) in enumerate(((oa_ref, g0_ref), (ob_ref, g1_ref), (oc_ref, g2_ref))):
        ys = _dot(o_i[...].astype(BF16), wb_ref[i])
        t = _sigmoid(g_i[...]) * ys
        y = t if y is None else y + t
    y2 = _dot(y.astype(BF16), wo_ref[...])
    o_ref[...] = x_ref[...] + _rms(y2, post_ref[...])


def merge(x, o_a, o_b, o_c, z, w_branch_bf, w_out_bf, post_w, *, layer, tm, gate_blk):
    n, d = x.shape
    bw = o_a.shape[-1]
    nb = w_branch_bf.shape[1]
    ospec = pl.BlockSpec((tm, bw), lambda i: (i, 0))
    gspec = lambda k: pl.BlockSpec((tm, d), lambda i: (i, gate_blk + k))
    return pl.pallas_call(
        _merge_kernel,
        grid=(n // tm,),
        in_specs=[
            pl.BlockSpec((tm, d), lambda i: (i, 0)),
            ospec, ospec, ospec, gspec(0), gspec(1), gspec(2),
            pl.BlockSpec((None, nb, bw, d), lambda i: (layer, 0, 0, 0)),
            pl.BlockSpec((None, d, d), lambda i: (layer, 0, 0)),
            pl.BlockSpec((1, d), lambda i: (0, 0)),
        ],
        out_specs=pl.BlockSpec((tm, d), lambda i: (i, 0)),
        out_shape=jax.ShapeDtypeStruct((n, d), F32),
        compiler_params=_cparams(("parallel",)),
        name="merge",
    )(x, o_a, o_b, o_c, z, z, z, w_branch_bf, w_out_bf, post_w.reshape(1, d))


ROPE_THETA = 500000.0
SCAN_CHUNK = 64
GDN_CHUNKS_PER_STEP = 4
HGRN_CHUNKS_PER_STEP = 4
GDN_INSTANCES = 16
ROW_TILE = 1024
FFN_ROW_TILE = 512
FF_TILE = 256
PROJ_COL_TILES = 4
ATTN_TILE = 512
ATTN_QCHUNK = 1024
PREP_TILE = 512
MERGE_TILE = 512
SMALL_COLS = 128


def _split_w_in(w_in_l, sizes):
    offs = [0]
    for s in sizes:
        offs.append(offs[-1] + s)
    seg = lambda i: w_in_l[:, offs[i]:offs[i + 1]]
    main = jnp.concatenate([seg(11), seg(0), seg(1), seg(4), seg(5), seg(6), seg(7), seg(8), seg(9), seg(10)], axis=1)
    small = jnp.concatenate([seg(2), seg(3)], axis=1)
    small = jnp.pad(small, ((0, 0), (0, SMALL_COLS - small.shape[1])))
    return main.astype(BF16), small.astype(BF16)


def _row_tile(n, pref):
    return pref if n % pref == 0 else n


def kernel(x_prompt, x_sample, state_gdn, state_gdn_conv, state_hgrn, cache_k, cache_v, page_table,
           ffn1_norm_pre, ffn1_norm_post, ffn1_w_in, ffn1_w_out, mix_norm_pre, mix_norm_post, w_in,
           gdn_conv_w, gdn_a_log, gdn_dt_bias, gdn_norm_w, hgrn_lb_raw, hgrn_norm_w, diff_lambda,
           diff_norm_w, w_branch, w_out, ffn2_norm_pre, ffn2_norm_post, ffn2_w_in, ffn2_w_out):
    depth = w_in.shape[0]
    bp, tp, d = x_prompt.shape
    bs, ts, _ = x_sample.shape
    _, _, gh, gdk, gdv = state_gdn.shape
    cch = state_gdn_conv.shape[-1]
    _, _, hh, hdk, hdv = state_hgrn.shape
    _, n_pool, page, ah, adh2 = cache_k.shape
    adh = adh2 // 2
    adv = cache_v.shape[-1]
    aw = ah * adv
    assert ah * adh2 == aw and hh * hdk == aw and hh * hdv == aw and gh * gdv == aw and cch == 3 * aw and d == 2 * aw
    sizes = (cch, gh * gdv, gh, gh, hh * hdk, hh * hdk, hh * hdv, hh * hdv, ah * adh2, ah * adh2, aw, 3 * d)
    gate_blk, qkv_blk, za_blk, hq_blk, aq_blk = 0, (3 * d) // cch, (3 * d + cch) // aw, (3 * d + cch) // aw + 1, (3 * d + cch) // aw + 5
    past_len = page_table.shape[1] * page
    ck = cache_k.reshape(depth, n_pool, page * ah, adh2)
    cv = cache_v.reshape(depth, n_pool, page * ah, adv)
    tabs_p = rope_tables(tp, 0, adh, ROPE_THETA)
    tabs_s = rope_tables(ts, past_len, adh, ROPE_THETA)
    zeros_conv = jnp.zeros((1, bp, 3, cch), F32)
    zeros_gdn = jnp.zeros((1, bp, gh, gdk, gdv), F32)
    zeros_hgrn = jnp.zeros((1, bp, hh, hdk, hdv), F32)

    f1_in, f1_out = ffn1_w_in.astype(BF16), ffn1_w_out.astype(BF16)
    f2_in, f2_out = ffn2_w_in.astype(BF16), ffn2_w_out.astype(BF16)
    wb, wo = w_branch.astype(BF16), w_out.astype(BF16)

    def run_layer(l, x, b, t, conv_state, gdn_state, hgrn_state, sl, tabs, prompt, w_main, w_small, acc):
        n = b * t
        tm = _row_tile(n, ROW_TILE)
        lam_init = 0.8 - 0.6 * math.exp(-0.3 * l)
        tmf = _row_tile(n, FFN_ROW_TILE)
        x = ffn(x, ffn1_norm_pre[l], f1_in, f1_out, ffn1_norm_post[l], layer=l, tm=tmf, tf=FF_TILE)
        z, zs = proj(x, mix_norm_pre[l], w_main, w_small, tm=tm, tn=w_main.shape[1] // PROJ_COL_TILES)
        z3 = z.reshape(b, t, z.shape[-1])
        zs3 = zs.reshape(b, t, SMALL_COLS)
        c = math.gcd(SCAN_CHUNK, t)
        o_a, new_gdn, new_conv = gdn(z3, zs3, conv_state, gdn_state, gdn_conv_w[l], gdn_a_log[l], gdn_dt_bias[l],
                                     gdn_norm_w[l], state_layer=sl, c=c, nc=min(GDN_CHUNKS_PER_STEP, t // c),
                                     bb=math.gcd(b, max(1, GDN_INSTANCES // (gh * min(GDN_CHUNKS_PER_STEP, t // c)))),
                                     qkv_blk=qkv_blk, za_blk=za_blk, layer=l, depth=depth,
                                     prev=None if acc is None else acc[0:1])
        o_b, new_hgrn = hgrn(z3, hgrn_lb_raw, hgrn_state, hgrn_norm_w[l], state_layer=sl, c=c,
                             nc=min(HGRN_CHUNKS_PER_STEP, t // c), layer=l, q_blk=hq_blk,
                             prev=None if acc is None else acc[1:2])
        q16, k32, k16, v32, v16 = qkv_prep(z3, tabs, q_blk=aq_blk, dh=adh, heads=ah, tm=_row_tile(t, PREP_TILE),
                                           v_transposed=prompt, layer=l, depth=depth,
                                           prev=None if acc is None else acc[2:4])
        if prompt:
            o_c = attn_prompt(q16, k16, v16, diff_lambda[l], diff_norm_w[l], tq=_row_tile(t, ATTN_TILE), dh=adh,
                              lam_init=lam_init)
        else:
            o_c = attn_sample(q16, k16, v16, ck, cv, page_table, diff_lambda[l], diff_norm_w[l], layer=l, heads=ah, dh=adh,
                              lam_init=lam_init)
        x = merge(x, o_a.reshape(n, aw), o_b.reshape(n, aw), o_c.reshape(n, aw), z, wb, wo, mix_norm_post[l],
                  layer=l, tm=_row_tile(n, MERGE_TILE), gate_blk=gate_blk)
        x = ffn(x, ffn2_norm_pre[l], f2_in, f2_out, ffn2_norm_post[l], layer=l, tm=tmf, tf=FF_TILE)
        return x, new_conv, (new_gdn, new_hgrn, k32, v32)

    xp = x_prompt.reshape(bp * tp, d)
    xs = x_sample.reshape(bs * ts, d)
    conv_p, conv_s, acc_p, acc_s = [], [], None, None
    for l in range(depth):
        w_main, w_small = _split_w_in(w_in[l], sizes)
        xp, cp, acc_p = run_layer(l, xp, bp, tp, zeros_conv, zeros_gdn, zeros_hgrn, 0, tabs_p, True, w_main, w_small, acc_p)
        conv_p.append(cp)
        xs, cs, acc_s = run_layer(l, xs, bs, ts, state_gdn_conv, state_gdn, state_hgrn, l, tabs_s, False, w_main, w_small,
                                  acc_s)
        conv_s.append(cs)

    def finish(x, b, t, convs, acc):
        new_gdn, new_hgrn, k32, v32 = acc
        return (x.reshape(b, t, d), new_gdn, jnp.stack(convs), new_hgrn,
                k32.reshape(depth, b, t, ah, adh2), v32.reshape(depth, b, t, ah, adv))

    yp, p_gdn, p_conv, p_hgrn, p_k, p_v = finish(xp, bp, tp, conv_p, acc_p)
    ys, s_gdn, s_conv, s_hgrn, s_k, s_v = finish(xs, bs, ts, conv_s, acc_s)
    return (yp, ys, p_gdn, p_conv, p_hgrn, p_k, p_v, s_gdn, s_conv, s_hgrn, s_k, s_v)
```

```python
import functools
import math

import jax
import jax.numpy as jnp
from jax import lax
from jax.experimental import pallas as pl
from jax.experimental.pallas import tpu as pltpu

F32 = jnp.float32
BF16 = jnp.bfloat16
EPS = 1e-6
HI = lax.Precision.HIGHEST

VMEM_LIMIT_BYTES = 56 * 1024 * 1024


def _cparams(sem):
    return pltpu.CompilerParams(dimension_semantics=sem, vmem_limit_bytes=VMEM_LIMIT_BYTES)


def _rms(x, w):
    return x * lax.rsqrt(jnp.mean(x * x, axis=-1, keepdims=True) + EPS) * w


def _sigmoid(x):
    return 1.0 / (1.0 + jnp.exp(-x))


def _silu(x):
    return x * _sigmoid(x)


def _softplus(x):
    return jnp.maximum(x, 0.0) + jnp.log1p(jnp.exp(-jnp.abs(x)))


def _ffn_kernel(x_ref, pre_ref, wi_ref, wo_ref, post_ref, o_ref, acc_scr, *, tf):
    dff = wo_ref.shape[0]
    h = _rms(x_ref[...], pre_ref[...]).astype(BF16)
    for j in range(dff // tf):
        cols = slice(j * tf, (j + 1) * tf)
        g = _dot(h, wi_ref[:, cols])
        u = _dot(h, wi_ref[:, dff + j * tf:dff + (j + 1) * tf])
        part = _dot((_silu(g) * u).astype(BF16), wo_ref[cols, :])
        if j == 0:
            acc_scr[...] = part
        else:
            acc_scr[...] += part
    o_ref[...] = x_ref[...] + 0.5 * _rms(acc_scr[...], post_ref[...])


def ffn(x, pre_w, w_in_bf, w_out_bf, post_w, *, layer, tm, tf):
    n, d = x.shape
    dff = w_out_bf.shape[1]
    resident = dict(pipeline_mode=pl.Buffered(1))
    return pl.pallas_call(
        functools.partial(_ffn_kernel, tf=tf),
        grid=(n // tm,),
        in_specs=[
            pl.BlockSpec((tm, d), lambda i: (i, 0)),
            pl.BlockSpec((1, d), lambda i: (0, 0)),
            pl.BlockSpec((None, d, 2 * dff), lambda i: (layer, 0, 0), **resident),
            pl.BlockSpec((None, dff, d), lambda i: (layer, 0, 0), **resident),
            pl.BlockSpec((1, d), lambda i: (0, 0)),
        ],
        out_specs=pl.BlockSpec((tm, d), lambda i: (i, 0)),
        out_shape=jax.ShapeDtypeStruct((n, d), F32),
        scratch_shapes=[pltpu.VMEM((tm, d), F32)],
        compiler_params=_cparams(("parallel",)),
        name="ffn",
    )(x, pre_w.reshape(1, d), w_in_bf, w_out_bf, post_w.reshape(1, d))


def _proj_kernel(x_ref, pre_ref, w_ref, ws_ref, z_ref, zs_ref, h_scr):
    j = pl.program_id(1)

    @pl.when(j == 0)
    def _():
        h = _rms(x_ref[...], pre_ref[...]).astype(BF16)
        h_scr[...] = h
        zs_ref[...] = jnp.dot(h, ws_ref[...], preferred_element_type=F32)

    z_ref[...] = jnp.dot(h_scr[...], w_ref[...], preferred_element_type=F32)


def proj(x, pre_w, w_main_bf, w_small_bf, *, tm, tn):
    n, d = x.shape
    cols = w_main_bf.shape[1]
    sc = w_small_bf.shape[1]
    return pl.pallas_call(
        _proj_kernel,
        grid=(n // tm, cols // tn),
        in_specs=[
            pl.BlockSpec((tm, d), lambda i, j: (i, 0)),
            pl.BlockSpec((1, d), lambda i, j: (0, 0)),
            pl.BlockSpec((d, tn), lambda i, j: (0, j)),
            pl.BlockSpec((d, sc), lambda i, j: (0, 0)),
        ],
        out_specs=[
            pl.BlockSpec((tm, tn), lambda i, j: (i, j)),
            pl.BlockSpec((tm, sc), lambda i, j: (i, 0)),
        ],
        out_shape=[jax.ShapeDtypeStruct((n, cols), F32), jax.ShapeDtypeStruct((n, sc), F32)],
        scratch_shapes=[pltpu.VMEM((tm, d), BF16)],
        compiler_params=_cparams(("parallel", "arbitrary")),
        name="proj",
    )(x, pre_w.reshape(1, d), w_main_bf, w_small_bf)


def _dot(a, b, precision=None):
    return jnp.dot(a, b, preferred_element_type=F32, precision=precision)


def _dot_nt(a, b, precision=None):
    return lax.dot_general(a, b, (((1,), (1,)), ((), ())), preferred_element_type=F32, precision=precision)


def _dot_tn(a, b, precision=None):
    return lax.dot_general(a, b, (((0,), (0,)), ((), ())), preferred_element_type=F32, precision=precision)


def _iota2(shape, dim):
    return lax.broadcasted_iota(jnp.int32, shape, dim)


def _split_bf16(x):
    hi = x.astype(BF16)
    return hi, (x - hi.astype(F32)).astype(BF16)


def _dot3(a, b):
    ah, al = _split_bf16(a)
    bh, bl = _split_bf16(b)
    return _dot(ah, bh) + (_dot(ah, bl) + _dot(al, bh))


def _tri_inverse_all(lmats, c):
    ri = _iota2((c, c), 0)
    ci = _iota2((c, c), 1)
    eye = jnp.where(ri == ci, 1.0, 0.0).astype(F32)
    nb = min(16, c)
    sh = int(math.log2(nb))
    same = (ri >> sh) == (ci >> sh)
    ps = [jnp.where(same, lm, 0.0) for lm in lmats]
    ts = [eye - p for p in ps]
    k = 2
    while k < nb:
        ps = [_dot3(p, p) for p in ps]
        ts = [_dot3(t, eye + p) for t, p in zip(ts, ps)]
        k *= 2
    blk = nb
    while blk < c:
        s1 = int(math.log2(blk))
        offm = ((ri >> (s1 + 1)) == (ci >> (s1 + 1))) & ((ri >> s1) != (ci >> s1))
        tl = [_dot3(t, jnp.where(offm, lm, 0.0)) for t, lm in zip(ts, lmats)]
        ts = [t - _dot3(x, t) for t, x in zip(ts, tl)]
        blk *= 2
    return ts


def _gdn_kernel(qkv_ref, za_ref, zs_ref, cs_ref, s0_ref, cw_ref, alog_ref, dtb_ref, nw_ref,
                o_ref, sout_ref, cout_ref, xbuf, s_scr, *, c, nc, bb, heads, dk, dv):
    n = pl.program_id(1)
    nlast = pl.num_programs(1) - 1
    nqk = heads * dk
    r = nc * c

    @pl.when(n == 0)
    def _():
        xbuf[:, 5:8, :] = cs_ref[...]
        s_scr[...] = s0_ref[...]

    xbuf[:, 8:8 + r, :] = qkv_ref[...]

    @pl.when(n == nlast)
    def _():
        cout_ref[...] = xbuf[:, 5 + r:8 + r, :]

    ri = _iota2((c, c), 0)
    ci = _iota2((c, c), 1)
    incl = ri >= ci
    strict = ri > ci
    rr = _iota2((r, r), 0)
    rc = _iota2((r, r), 1)
    sh = int(math.log2(c))
    blocktril = jnp.where((rr >= rc) & ((rr >> sh) == (rc >> sh)), 1.0, 0.0).astype(F32)
    nw = nw_ref[...]

    inst = []
    for bi in range(bb):
        y = xbuf[bi, 5:5 + r, :] * cw_ref[0:1, :]
        for j in range(1, 4):
            y = y + xbuf[bi, 5 + j:5 + j + r, :] * cw_ref[j:j + 1, :]
        y = _silu(y)
        zs = zs_ref[bi]
        beta_all = _sigmoid(zs)
        g_all = -jnp.exp(alog_ref[...]) * _softplus(zs + dtb_ref[...])
        gcum_all = _dot3(blocktril, g_all)
        gcum_t = gcum_all.T
        for i in range(nc):
            rows = slice(i * c, (i + 1) * c)
            for h in range(heads):
                inst.append(dict(
                    bi=bi, i=i, h=h,
                    q=y[rows, h * dk:(h + 1) * dk],
                    k=y[rows, nqk + h * dk:nqk + (h + 1) * dk],
                    v=y[rows, 2 * nqk + h * dv:2 * nqk + (h + 1) * dv],
                    beta=beta_all[rows, h:h + 1],
                    gc=gcum_all[rows, heads + h:heads + h + 1],
                    grow=gcum_t[heads + h:heads + h + 1, rows]))
    xbuf[:, 5:8, :] = xbuf[:, 5 + r:8 + r, :]

    for d in inst:
        q, k = d["q"], d["k"]
        d["q"] = q * lax.rsqrt(jnp.sum(q * q, axis=-1, keepdims=True) + EPS) * (dk ** -0.5)
        d["k"] = k * lax.rsqrt(jnp.sum(k * k, axis=-1, keepdims=True) + EPS)
        d["k16"] = d["k"].astype(BF16)
        d["kb"] = d["k"] * d["beta"]
        d["decay"] = jnp.where(incl, jnp.exp(jnp.minimum(d["gc"] - d["grow"], 0.0)), 0.0)
    for d in inst:
        d["lmat"] = jnp.where(strict, _dot_nt(d["kb"].astype(BF16), d["k16"]) * d["decay"], 0.0)
    tinvs = _tri_inverse_all([d["lmat"] for d in inst], c)
    for d, tinv in zip(inst, tinvs):
        eg = jnp.exp(d["gc"])
        sol = _dot3(tinv, jnp.concatenate([d["v"] * d["beta"], d["kb"] * eg], axis=-1))
        d["u"] = sol[:, :dv]
        d["w16"] = sol[:, dv:].astype(BF16)
        d["attn16"] = (_dot_nt(d["q"].astype(BF16), d["k16"]) * d["decay"]).astype(BF16)
        d["qe16"] = (d["q"] * eg).astype(BF16)
        gl = d["gc"][c - 1:c, :]
        d["kdec16"] = (d["k"] * jnp.exp(gl - d["gc"])).astype(BF16)
        d["egl"] = jnp.exp(gl)

    state = {(bi, h): s_scr[bi, h] for bi in range(bb) for h in range(heads)}
    for i in range(nc):
        for d in inst:
            if d["i"] != i:
                continue
            bi, h = d["bi"], d["h"]
            s = state[(bi, h)]
            s16 = s.astype(BF16)
            v_new = d["u"] - _dot(d["w16"], s16)
            vn16 = v_new.astype(BF16)
            o = _dot(d["qe16"], s16) + _dot(d["attn16"], vn16)
            state[(bi, h)] = s * d["egl"] + _dot_tn(d["kdec16"], vn16)
            za = za_ref[bi, i * c:(i + 1) * c, h * dv:(h + 1) * dv]
            o_ref[bi, i * c:(i + 1) * c, h * dv:(h + 1) * dv] = _rms(o, nw) * _silu(za)
    for (bi, h), s in state.items():
        s_scr[bi, h] = s

    @pl.when(n == nlast)
    def _():
        sout_ref[...] = s_scr[...]


def gdn(z3, zs3, conv_state, state, conv_w, a_log, dt_bias, norm_w, *, state_layer, c, nc, bb, qkv_blk, za_blk,
        layer, depth, prev):
    b, t, _ = z3.shape
    _, _, heads, dk, dv = state.shape
    cch = conv_state.shape[-1]
    sc = zs3.shape[-1]
    r = nc * c
    pad = jnp.zeros((1, sc), F32)
    alog_p = lax.dynamic_update_slice(pad, a_log.reshape(1, heads).astype(F32), (0, heads))
    dtb_p = lax.dynamic_update_slice(pad, dt_bias.reshape(1, heads).astype(F32), (0, heads))
    kern = functools.partial(_gdn_kernel, c=c, nc=nc, bb=bb, heads=heads, dk=dk, dv=dv)
    prev_ops, prev_specs, aliases = _alias_plan(prev, 9, (1,))
    return pl.pallas_call(
        _drop_refs(kern, 9, len(prev_ops)),
        grid=(b // bb, t // r),
        in_specs=[
            pl.BlockSpec((bb, r, cch), lambda i, n: (i, n, qkv_blk)),
            pl.BlockSpec((bb, r, heads * dv), lambda i, n: (i, n, za_blk)),
            pl.BlockSpec((bb, r, sc), lambda i, n: (i, n, 0)),
            pl.BlockSpec((None, bb, 3, cch), lambda i, n: (state_layer, i, 0, 0)),
            pl.BlockSpec((None, bb, heads, dk, dv), lambda i, n: (state_layer, i, 0, 0, 0)),
            pl.BlockSpec((4, cch), lambda i, n: (0, 0)),
            pl.BlockSpec((1, sc), lambda i, n: (0, 0)),
            pl.BlockSpec((1, sc), lambda i, n: (0, 0)),
            pl.BlockSpec((1, dv), lambda i, n: (0, 0)),
        ] + prev_specs,
        out_specs=[
            pl.BlockSpec((bb, r, heads * dv), lambda i, n: (i, n, 0)),
            pl.BlockSpec((None, bb, heads, dk, dv), lambda i, n: (layer, i, 0, 0, 0)),
            pl.BlockSpec((bb, 3, cch), lambda i, n: (i, 0, 0)),
        ],
        out_shape=[
            jax.ShapeDtypeStruct((b, t, heads * dv), F32),
            jax.ShapeDtypeStruct((depth, b, heads, dk, dv), F32),
            jax.ShapeDtypeStruct((b, 3, cch), F32),
        ],
        scratch_shapes=[pltpu.VMEM((bb, 8 + r, cch), F32), pltpu.VMEM((bb, heads, dk, dv), F32)],
        input_output_aliases=aliases,
        compiler_params=_cparams(("parallel", "arbitrary")),
        name="gdn",
    )(z3, z3, zs3, conv_state, state, conv_w, alog_p, dtb_p, norm_w.reshape(1, dv), *prev_ops)


SB = 8


def _dot_cumsum(tril, x):
    hi = x.astype(BF16)
    r1 = x - hi.astype(F32)
    mid = r1.astype(BF16)
    lo = (r1 - mid.astype(F32)).astype(BF16)
    t16 = tril.astype(BF16)
    return _dot(t16, hi) + (_dot(t16, mid) + _dot(t16, lo))


def _hgrn_kernel(q_ref, f_ref, i_ref, g_ref, lbraw_ref, s0_ref, nw_ref, o_ref, sout_ref,
                 st_scr, *, c, nc, bb, heads, dk, dv, layer):
    n = pl.program_id(1)
    nlast = pl.num_programs(1) - 1
    bh = [(bi, h) for bi in range(bb) for h in range(heads)]

    @pl.when(n == 0)
    def _():
        for bi, h in bh:
            st_scr[bi * heads + h] = s0_ref[bi, h].T

    raw = lbraw_ref[...]
    e = jnp.exp(raw - jnp.max(raw, axis=0, keepdims=True))
    sm = e / jnp.sum(e, axis=0, keepdims=True)
    lb = jnp.zeros((1, heads * dk), F32)
    for l in range(1, layer + 1):
        lb = lb + sm[l:l + 1, :]

    r = bb * nc * c
    flat = lambda ref: ref[...].reshape(r, ref.shape[-1])
    zf = flat(f_ref)
    logsig = jnp.minimum(zf, 0.0) - jnp.log1p(jnp.exp(-jnp.abs(zf)))
    la = jnp.log(lb)
    lbb = jnp.log1p(-lb) + logsig
    logf = jnp.maximum(la, lbb) + jnp.log1p(jnp.exp(-jnp.abs(la - lbb)))
    kh = (1.0 - lb) * _sigmoid(-zf)
    qh = _silu(flat(q_ref)) * (dk ** -0.5)
    vv = flat(i_ref)
    gate = _sigmoid(flat(g_ref))

    ri = _iota2((r, r), 0)
    ci = _iota2((r, r), 1)
    sh = int(math.log2(c))
    tril = jnp.where((ri >= ci) & ((ri >> sh) == (ci >> sh)), 1.0, 0.0).astype(F32)
    bcum = _dot_cumsum(tril, logf)
    fgate = jnp.exp(logf)
    nsb = c // SB
    sub = _iota2((nsb, SB, dk), 1)
    nw = nw_ref[...]
    g3 = lambda x: x.reshape(nsb, SB, x.shape[-1])
    states = {(bi, h): st_scr[bi * heads + h] for bi, h in bh}

    for ic, bi, h in [(ic, bi, h) for ic in range(nc) for bi, h in bh]:
        rows = slice((bi * nc + ic) * c, (bi * nc + ic + 1) * c)
        ks = slice(h * dk, (h + 1) * dk)
        vs = slice(h * dv, (h + 1) * dv)
        q = qh[rows, ks]
        k = kh[rows, ks]
        b = bcum[rows, ks]
        v = vv[rows, vs]
        st = states[(bi, h)]
        o = _dot_nt((q * jnp.exp(b)).astype(BF16), st.astype(BF16))
        q3, k3, v3, f3 = g3(q), g3(k), g3(v), g3(fgate[rows, ks])
        o3 = jnp.sum(q3 * k3, axis=-1, keepdims=True) * v3
        e = None
        for d in range(1, SB):
            fd = f3 if d == 1 else pltpu.roll(f3, d - 1, 1)
            e = fd if e is None else e * fd
            m = jnp.where(sub >= d, q3 * pltpu.roll(k3, d, 1) * e, 0.0)
            o3 = o3 + jnp.sum(m, axis=-1, keepdims=True) * pltpu.roll(v3, d, 1)
        o = o + o3.reshape(c, dv)
        if nsb > 1:
            b3 = g3(b)
            rend = jnp.broadcast_to(b3[:, SB - 1:SB, :], b3.shape).reshape(c, dk)
            khat = k * jnp.exp(jnp.minimum(rend - b, 0.0))
            zq = jnp.zeros((c, dk), F32)
            qs, kk = [], []
            for jb in range(nsb - 1):
                lo, hi = jb * SB, (jb + 1) * SB
                rj = b[hi - 1:hi, :]
                qpart = q[hi:, :] * jnp.exp(jnp.minimum(b[hi:, :] - rj, 0.0))
                qs.append(jnp.concatenate([zq[:hi], qpart], axis=0))
                kparts = [khat[lo:hi]] if lo == 0 else [zq[:lo], khat[lo:hi]]
                kk.append(jnp.concatenate(kparts + [zq[hi:]], axis=0))
            amat = _dot_nt(jnp.concatenate(qs, axis=-1).astype(BF16), jnp.concatenate(kk, axis=-1).astype(BF16))
            o = o + _dot(amat.astype(BF16), v.astype(BF16))
        bl = b[c - 1:c, :]
        kdec = k * jnp.exp(bl - b)
        states[(bi, h)] = st * jnp.exp(bl) + _dot_tn(v.astype(BF16), kdec.astype(BF16))
        o_ref[bi, ic * c:(ic + 1) * c, vs] = _rms(o, nw) * gate[rows, vs]
    for bi, h in bh:
        st_scr[bi * heads + h] = states[(bi, h)]

    @pl.when(n == nlast)
    def _():
        for bi, h in bh:
            sout_ref[bi, h] = st_scr[bi * heads + h].T


def hgrn(z3, lb_raw, state, norm_w, *, state_layer, c, nc, bb, layer, q_blk, prev):
    b, t, _ = z3.shape
    _, _, heads, dk, dv = state.shape
    depth = lb_raw.shape[0]
    wk = heads * dk
    assert c % SB == 0
    kern = functools.partial(_hgrn_kernel, c=c, nc=nc, bb=bb, heads=heads, dk=dk, dv=dv, layer=layer)
    r = nc * c
    zspec = lambda off: pl.BlockSpec((bb, r, wk), lambda i, n: (i, n, q_blk + off))
    prev_ops, prev_specs, aliases = _alias_plan(prev, 7, (1,))
    return pl.pallas_call(
        _drop_refs(kern, 7, len(prev_ops)),
        grid=(b // bb, t // r),
        in_specs=[
            zspec(0), zspec(1), zspec(2), zspec(3),
            pl.BlockSpec((depth, wk), lambda i, n: (0, 0)),
            pl.BlockSpec((None, bb, heads, dk, dv), lambda i, n: (state_layer, i, 0, 0, 0)),
            pl.BlockSpec((1, dv), lambda i, n: (0, 0)),
        ] + prev_specs,
        out_specs=[
            pl.BlockSpec((bb, r, heads * dv), lambda i, n: (i, n, 0)),
            pl.BlockSpec((None, bb, heads, dk, dv), lambda i, n: (layer, i, 0, 0, 0)),
        ],
        out_shape=[
            jax.ShapeDtypeStruct((b, t, heads * dv), F32),
            jax.ShapeDtypeStruct((depth, b, heads, dk, dv), F32),
        ],
        scratch_shapes=[pltpu.VMEM((bb * heads, dv, dk), F32)],
        input_output_aliases=aliases,
        compiler_params=_cparams(("parallel", "arbitrary")),
        name="hgrn",
    )(z3, z3, z3, z3, lb_raw, state, norm_w.reshape(1, dv), *prev_ops)


def rope_tables(t, pos0, dh, theta):
    rd = dh // 4
    half = rd // 2
    inv = jnp.power(jnp.float32(theta), -jnp.arange(half, dtype=F32) / half)
    ang = (pos0 + jnp.arange(t)).astype(F32)[:, None] * inv
    cos, sin = jnp.cos(ang), jnp.sin(ang)
    d = jnp.arange(128) % dh
    f = d % half
    cc = jnp.where(d < rd, cos[:, f], 1.0)
    s1 = jnp.where(d < half, -sin[:, f], 0.0)
    s2 = jnp.where((d >= half) & (d < rd), sin[:, f], 0.0)
    return cc.astype(F32), s1.astype(F32), s2.astype(F32)


def _prep_kernel(q_ref, k_ref, v_ref, c_ref, s1_ref, s2_ref, q16_ref, k32_ref, k16_ref, v32_ref, v16_ref,
                 *, scale, half, heads, v_transposed):
    tm, w = q_ref.shape[1:]
    hw = w // heads
    reps = w // c_ref.shape[-1]
    cc = jnp.concatenate([c_ref[...]] * reps, axis=-1)
    s1 = jnp.concatenate([s1_ref[...]] * reps, axis=-1)
    s2 = jnp.concatenate([s2_ref[...]] * reps, axis=-1)

    def rope(x):
        return x * cc + pltpu.roll(x, w - half, 1) * s1 + pltpu.roll(x, half, 1) * s2

    q16_ref[0] = (rope(q_ref[0]) * scale).astype(BF16)
    k = rope(k_ref[0])
    k16_ref[0] = k.astype(BF16)
    v = v_ref[0]
    v16_ref[0] = (v.T if v_transposed else v).astype(BF16)
    for h in range(heads):
        k32_ref[0, pl.ds(h, tm, stride=heads), :] = k[:, h * hw:(h + 1) * hw]
        v32_ref[0, pl.ds(h, tm, stride=heads), :] = v[:, h * hw:(h + 1) * hw]


def _drop_refs(kern, start, count):
    def wrapped(*refs):
        return kern(*refs[:start], *refs[start + count:])
    return wrapped


def _alias_plan(prev, n_in, out_idx):
    prev = [] if prev is None else list(prev)
    specs = [pl.BlockSpec(memory_space=pl.ANY)] * len(prev)
    return prev, specs, {n_in + k: out_idx[k] for k in range(len(prev))}


def qkv_prep(z3, tables, *, q_blk, dh, heads, tm, v_transposed, layer, depth, prev):
    b, t, _ = z3.shape
    w = tables[0].shape[-1] * (heads * 2 * dh // tables[0].shape[-1])
    hw = w // heads
    cc, s1, s2 = tables
    kern = functools.partial(_prep_kernel, scale=dh ** -0.5 * math.log2(math.e), half=dh // 8, heads=heads,
                             v_transposed=v_transposed)
    zspec = lambda off: pl.BlockSpec((1, tm, w), lambda i, n: (i, n, q_blk + off))
    tspec = pl.BlockSpec((tm, cc.shape[-1]), lambda i, n: (n, 0))
    ospec = pl.BlockSpec((1, tm, w), lambda i, n: (i, n, 0))
    cspec = pl.BlockSpec((None, 1, tm * heads, hw), lambda i, n: (layer, i, n, 0))
    sd = lambda dt: jax.ShapeDtypeStruct((b, t, w), dt)
    sc = jax.ShapeDtypeStruct((depth, b, t * heads, hw), F32)
    vspec = pl.BlockSpec((1, w, tm), lambda i, n: (i, 0, n)) if v_transposed else ospec
    vsd = jax.ShapeDtypeStruct((b, w, t), BF16) if v_transposed else sd(BF16)
    prev_ops, prev_specs, aliases = _alias_plan(prev, 6, (1, 3))
    return pl.pallas_call(
        _drop_refs(kern, 6, len(prev_ops)),
        grid=(b, t // tm),
        in_specs=[zspec(0), zspec(1), zspec(2), tspec, tspec, tspec] + prev_specs,
        out_specs=[ospec, cspec, ospec, cspec, vspec],
        out_shape=[sd(BF16), sc, sd(BF16), sc, vsd],
        input_output_aliases=aliases,
        compiler_params=_cparams(("parallel", "parallel")),
        name="qkv_prep",
    )(z3, z3, z3, cc, s1, s2, *prev_ops)


def _lambda(lam_ref, lam_init):
    lm = lam_ref[...]
    a = jnp.sum(lm[0:1, :] * lm[1:2, :], axis=-1, keepdims=True)
    b = jnp.sum(lm[2:3, :] * lm[3:4, :], axis=-1, keepdims=True)
    return jnp.exp(a) - jnp.exp(b) + lam_init


ONES_ROWS = 16


def _attn_prompt_kernel(qt_ref, kt_ref, q_ref, k_ref, vt_ref, lam_ref, nw_ref, o_ref,
                        qs_scr, m_scr, acc_scr, *, tq, dh, dv, hpb, qchunk, lam_init):
    p = pl.program_id(2)
    qi = qt_ref[p]
    ki = kt_ref[p]

    @pl.when(ki == 0)
    def _():
        for hh in range(hpb):
            q = q_ref[0, :, hh * dv:(hh + 1) * dv]
            lane = _iota2(q.shape, 1)
            zero = jnp.zeros_like(q)
            qs_scr[hh, 0:tq, :] = jnp.where(lane < dh, q, zero)
            qs_scr[hh, tq:2 * tq, :] = jnp.where(lane >= dh, q, zero)
        m_scr[...] = jnp.full(m_scr.shape, -jnp.inf, F32)
        acc_scr[...] = jnp.zeros_like(acc_scr)

    def update(diag):
        for hh in range(hpb):
            k = k_ref[0, :, hh * dv:(hh + 1) * dv]
            vt = vt_ref[0, hh * dv:(hh + 1) * dv, :]
            vt1 = jnp.concatenate([vt, jnp.ones((ONES_ROWS, vt.shape[1]), BF16)], axis=0)
            for c0 in range(0, 2 * tq, qchunk):
                cs = slice(c0, c0 + qchunk)
                st = _dot_nt(k, qs_scr[hh, cs, :])
                if diag:
                    key = _iota2(st.shape, 0)
                    qry = (_iota2(st.shape, 1) + c0) & (tq - 1)
                    st = jnp.where(key <= qry, st, -jnp.inf)
                m_prev = m_scr[hh, :, cs]
                m_new = jnp.maximum(m_prev, jnp.max(st, axis=0, keepdims=True))
                alpha = jnp.exp2(m_prev - m_new)
                pr = jnp.exp2(st - m_new).astype(BF16)
                acc_scr[hh, :, cs] = alpha * acc_scr[hh, :, cs] + _dot(vt1, pr)
                m_scr[hh, :, cs] = m_new

    @pl.when(ki < qi)
    def _():
        update(False)

    @pl.when(ki == qi)
    def _():
        update(True)
        lam = _lambda(lam_ref, lam_init)
        for hh in range(hpb):
            o1 = acc_scr[hh, 0:dv, 0:tq] / acc_scr[hh, dv:dv + 1, 0:tq]
            o2 = acc_scr[hh, 0:dv, tq:2 * tq] / acc_scr[hh, dv:dv + 1, tq:2 * tq]
            o = (o1 - lam * o2).T
            o_ref[0, :, hh * dv:(hh + 1) * dv] = _rms(o, nw_ref[...]) * (1.0 - lam_init)


def attn_prompt(q16, k16, vt16, lam, norm_w, *, tq, dh, lam_init):
    b, t, w = q16.shape
    dv = norm_w.shape[-1]
    heads = w // dv
    nq = t // tq
    pairs = [(i, j) for i in range(nq) for j in range(i + 1)]
    qt = jnp.asarray([pq for pq, _ in pairs], jnp.int32)
    kt = jnp.asarray([pk for _, pk in pairs], jnp.int32)
    hpb = math.gcd(heads, ATTN_HEADS_PER_STEP)
    kern = functools.partial(_attn_prompt_kernel, tq=tq, dh=dh, dv=dv, hpb=hpb, qchunk=min(ATTN_QCHUNK, 2 * tq),
                             lam_init=lam_init)
    grid_spec = pltpu.PrefetchScalarGridSpec(
        num_scalar_prefetch=2,
        grid=(b, heads // hpb, len(pairs)),
        in_specs=[
            pl.BlockSpec((1, tq, hpb * dv), lambda i, h, p, qt, kt: (i, qt[p], h)),
            pl.BlockSpec((1, tq, hpb * dv), lambda i, h, p, qt, kt: (i, kt[p], h)),
            pl.BlockSpec((1, hpb * dv, tq), lambda i, h, p, qt, kt: (i, h, kt[p])),
            pl.BlockSpec(lam.shape, lambda i, h, p, qt, kt: (0, 0)),
            pl.BlockSpec((1, dv), lambda i, h, p, qt, kt: (0, 0)),
        ],
        out_specs=pl.BlockSpec((1, tq, hpb * dv), lambda i, h, p, qt, kt: (i, qt[p], h)),
        scratch_shapes=[
            pltpu.VMEM((hpb, 2 * tq, dv), BF16),
            pltpu.VMEM((hpb, 1, 2 * tq), F32),
            pltpu.VMEM((hpb, dv + ONES_ROWS, 2 * tq), F32),
        ],
    )
    return pl.pallas_call(
        kern,
        grid_spec=grid_spec,
        out_shape=jax.ShapeDtypeStruct((b, t, w), F32),
        compiler_params=_cparams(("parallel", "parallel", "arbitrary")),
        name="attn_prompt",
    )(qt, kt, q16, k16, vt16, lam, norm_w.reshape(1, dv))


def _attn_sample_kernel(pt_ref, q_ref, kc_ref, vc_ref, lam_ref, nw_ref, *rest,
                        npages, tq, heads, dh, dv, page, lam_init):
    k_refs = rest[:npages]
    v_refs = rest[npages:2 * npages]
    o_ref = rest[2 * npages]
    lam = _lambda(lam_ref, lam_init)
    nw = nw_ref[...]
    q = q_ref[0].astype(F32)
    kc = kc_ref[0]
    vc = vc_ref[0]
    lane = _iota2((tq, dv), 1)
    qidx = _iota2((2 * tq, tq), 0) % tq
    kidx = _iota2((2 * tq, tq), 1)
    hsl = [slice(h * dv, (h + 1) * dv) for h in range(heads)]

    def head_rows(refs, h):
        return jnp.concatenate([r[0, 0, pl.ds(h, page, stride=heads), :].astype(BF16) for r in refs], axis=0)

    qrows = [jnp.concatenate([jnp.where(lane < dh, q[:, hs], 0.0), jnp.where(lane >= dh, q[:, hs], 0.0)],
                             axis=0).astype(BF16) for hs in hsl]
    s_past = [_dot_nt(qrows[h], head_rows(k_refs, h)) for h in range(heads)]
    s_cur = [jnp.where(kidx <= qidx, _dot_nt(qrows[h], kc[:, hsl[h]]), -jnp.inf) for h in range(heads)]
    ms = [jnp.maximum(jnp.max(sp, axis=-1, keepdims=True), jnp.max(sc, axis=-1, keepdims=True))
          for sp, sc in zip(s_past, s_cur)]
    p_past = [jnp.exp2(sp - m) for sp, m in zip(s_past, ms)]
    p_cur = [jnp.exp2(sc - m) for sc, m in zip(s_cur, ms)]
    invs = [1.0 / (jnp.sum(pp, axis=-1, keepdims=True) + jnp.sum(pc, axis=-1, keepdims=True))
            for pp, pc in zip(p_past, p_cur)]

    def diff(pp, inv):
        pn = pp * inv
        return pn[0:tq] - lam * pn[tq:2 * tq]

    for h in range(heads):
        o = (_dot(diff(p_past[h], invs[h]).astype(BF16), head_rows(v_refs, h))
             + _dot(diff(p_cur[h], invs[h]), vc[:, hsl[h]].astype(F32)))
        o_ref[0, :, hsl[h]] = _rms(o, nw) * (1.0 - lam_init)


def attn_sample(q16, k16, v16, cache_k, cache_v, page_table, lam, norm_w, *, layer, heads, dh, lam_init):
    b, tq, w = q16.shape
    dv = norm_w.shape[-1]
    npages = page_table.shape[1]
    prow = cache_k.shape[2]
    kern = functools.partial(_attn_sample_kernel, npages=npages, tq=tq, heads=heads, dh=dh, dv=dv,
                             page=prow // heads, lam_init=lam_init)
    cur = pl.BlockSpec((1, tq, w), lambda i, pt: (i, 0, 0))

    def page_spec(j):
        return pl.BlockSpec((1, 1, prow, dv), lambda i, pt: (layer, pt[i, j], 0, 0))

    grid_spec = pltpu.PrefetchScalarGridSpec(
        num_scalar_prefetch=1,
        grid=(b,),
        in_specs=[cur, cur, cur,
                  pl.BlockSpec(lam.shape, lambda i, pt: (0, 0)),
                  pl.BlockSpec((1, dv), lambda i, pt: (0, 0))]
        + [page_spec(j) for j in range(npages)] * 2,
        out_specs=pl.BlockSpec((1, tq, w), lambda i, pt: (i, 0, 0)),
    )
    return pl.pallas_call(
        kern,
        grid_spec=grid_spec,
        out_shape=jax.ShapeDtypeStruct((b, tq, w), F32),
        compiler_params=_cparams(("parallel",)),
        name="attn_sample",
    )(page_table, q16, k16, v16, lam, norm_w.reshape(1, dv), *([cache_k] * npages), *([cache_v] * npages))


def _merge_kernel(x_ref, oa_ref, ob_ref, oc_ref, g0_ref, g1_ref, g2_ref, wb_ref, wo_ref, post_ref, o_ref):
    y = None
    for i, (o_i, g_i) in enumerate(((oa_ref, g0_ref), (ob_ref, g1_ref), (oc_ref, g2_ref))):
        ys = _dot(o_i[...].astype(BF16), wb_ref[i])
        t = _sigmoid(g_i[...]) * ys
        y = t if y is None else y + t
    y2 = _dot(y.astype(BF16), wo_ref[...])
    o_ref[...] = x_ref[...] + _rms(y2, post_ref[...])


def merge(x, o_a, o_b, o_c, z, w_branch_bf, w_out_bf, post_w, *, layer, tm, gate_blk):
    n, d = x.shape
    bw = o_a.shape[-1]
    nb = w_branch_bf.shape[1]
    ospec = pl.BlockSpec((tm, bw), lambda i: (i, 0))
    gspec = lambda k: pl.BlockSpec((tm, d), lambda i: (i, gate_blk + k))
    return pl.pallas_call(
        _merge_kernel,
        grid=(n // tm,),
        in_specs=[
            pl.BlockSpec((tm, d), lambda i: (i, 0)),
            ospec, ospec, ospec, gspec(0), gspec(1), gspec(2),
            pl.BlockSpec((None, nb, bw, d), lambda i: (layer, 0, 0, 0)),
            pl.BlockSpec((None, d, d), lambda i: (layer, 0, 0)),
            pl.BlockSpec((1, d), lambda i: (0, 0)),
        ],
        out_specs=pl.BlockSpec((tm, d), lambda i: (i, 0)),
        out_shape=jax.ShapeDtypeStruct((n, d), F32),
        compiler_params=_cparams(("parallel",)),
        name="merge",
    )(x, o_a, o_b, o_c, z, z, z, w_branch_bf, w_out_bf, post_w.reshape(1, d))


ROPE_THETA = 500000.0
SCAN_CHUNK = 64
GDN_CHUNKS_PER_STEP = 4
HGRN_CHUNKS_PER_STEP = 4
HGRN_INSTANCES = 32
GDN_INSTANCES = 16
ROW_TILE = 1024
FFN_ROW_TILE = 512
FF_TILE = 256
PROJ_COL_TILES = 4
ATTN_TILE = 512
ATTN_QCHUNK = 1024
ATTN_HEADS_PER_STEP = 2
PREP_TILE = 512
MERGE_TILE = 512
SMALL_COLS = 128


def _split_w_in(w_in_l, sizes):
    offs = [0]
    for s in sizes:
        offs.append(offs[-1] + s)
    seg = lambda i: w_in_l[:, offs[i]:offs[i + 1]]
    main = jnp.concatenate([seg(11), seg(0), seg(1), seg(4), seg(5), seg(6), seg(7), seg(8), seg(9), seg(10)], axis=1)
    small = jnp.concatenate([seg(2), seg(3)], axis=1)
    small = jnp.pad(small, ((0, 0), (0, SMALL_COLS - small.shape[1])))
    return main.astype(BF16), small.astype(BF16)


def _row_tile(n, pref):
    return pref if n % pref == 0 else n


def kernel(x_prompt, x_sample, state_gdn, state_gdn_conv, state_hgrn, cache_k, cache_v, page_table,
           ffn1_norm_pre, ffn1_norm_post, ffn1_w_in, ffn1_w_out, mix_norm_pre, mix_norm_post, w_in,
           gdn_conv_w, gdn_a_log, gdn_dt_bias, gdn_norm_w, hgrn_lb_raw, hgrn_norm_w, diff_lambda,
           diff_norm_w, w_branch, w_out, ffn2_norm_pre, ffn2_norm_post, ffn2_w_in, ffn2_w_out):
    depth = w_in.shape[0]
    bp, tp, d = x_prompt.shape
    bs, ts, _ = x_sample.shape
    _, _, gh, gdk, gdv = state_gdn.shape
    cch = state_gdn_conv.shape[-1]
    _, _, hh, hdk, hdv = state_hgrn.shape
    _, n_pool, page, ah, adh2 = cache_k.shape
    adh = adh2 // 2
    adv = cache_v.shape[-1]
    aw = ah * adv
    assert ah * adh2 == aw and hh * hdk == aw and hh * hdv == aw and gh * gdv == aw and cch == 3 * aw and d == 2 * aw
    sizes = (cch, gh * gdv, gh, gh, hh * hdk, hh * hdk, hh * hdv, hh * hdv, ah * adh2, ah * adh2, aw, 3 * d)
    gate_blk, qkv_blk, za_blk, hq_blk, aq_blk = 0, (3 * d) // cch, (3 * d + cch) // aw, (3 * d + cch) // aw + 1, (3 * d + cch) // aw + 5
    past_len = page_table.shape[1] * page
    ck = cache_k.reshape(depth, n_pool, page * ah, adh2)
    cv = cache_v.reshape(depth, n_pool, page * ah, adv)
    tabs_p = rope_tables(tp, 0, adh, ROPE_THETA)
    tabs_s = tuple(jnp.tile(a, (bs, 1)) for a in rope_tables(ts, past_len, adh, ROPE_THETA))
    zeros_conv = jnp.zeros((1, bp, 3, cch), F32)
    zeros_gdn = jnp.zeros((1, bp, gh, gdk, gdv), F32)
    zeros_hgrn = jnp.zeros((1, bp, hh, hdk, hdv), F32)

    f1_in, f1_out = ffn1_w_in.astype(BF16), ffn1_w_out.astype(BF16)
    f2_in, f2_out = ffn2_w_in.astype(BF16), ffn2_w_out.astype(BF16)
    wb, wo = w_branch.astype(BF16), w_out.astype(BF16)

    def run_layer(l, x, b, t, conv_state, gdn_state, hgrn_state, sl, tabs, prompt, w_main, w_small, acc):
        n = b * t
        tm = _row_tile(n, ROW_TILE)
        lam_init = 0.8 - 0.6 * math.exp(-0.3 * l)
        tmf = _row_tile(n, FFN_ROW_TILE)
        x = ffn(x, ffn1_norm_pre[l], f1_in, f1_out, ffn1_norm_post[l], layer=l, tm=tmf, tf=FF_TILE)
        z, zs = proj(x, mix_norm_pre[l], w_main, w_small, tm=tm, tn=w_main.shape[1] // PROJ_COL_TILES)
        z3 = z.reshape(b, t, z.shape[-1])
        zs3 = zs.reshape(b, t, SMALL_COLS)
        c = math.gcd(SCAN_CHUNK, t)
        o_a, new_gdn, new_conv = gdn(z3, zs3, conv_state, gdn_state, gdn_conv_w[l], gdn_a_log[l], gdn_dt_bias[l],
                                     gdn_norm_w[l], state_layer=sl, c=c, nc=min(GDN_CHUNKS_PER_STEP, t // c),
                                     bb=math.gcd(b, max(1, GDN_INSTANCES // (gh * min(GDN_CHUNKS_PER_STEP, t // c)))),
                                     qkv_blk=qkv_blk, za_blk=za_blk, layer=l, depth=depth,
                                     prev=None if acc is None else acc[0:1])
        o_b, new_hgrn = hgrn(z3, hgrn_lb_raw, hgrn_state, hgrn_norm_w[l], state_layer=sl, c=c,
                             nc=min(HGRN_CHUNKS_PER_STEP, t // c),
                             bb=math.gcd(b, max(1, HGRN_INSTANCES // (hh * min(HGRN_CHUNKS_PER_STEP, t // c)))),
                             layer=l, q_blk=hq_blk,
                             prev=None if acc is None else acc[1:2])
        if prompt:
            q16, k32, k16, v32, v16 = qkv_prep(z3, tabs, q_blk=aq_blk, dh=adh, heads=ah, tm=_row_tile(t, PREP_TILE),
                                               v_transposed=True, layer=l, depth=depth,
                                               prev=None if acc is None else acc[2:4])
        else:
            q16, k32, k16, v32, v16 = qkv_prep(z.reshape(1, n, z.shape[-1]), tabs, q_blk=aq_blk, dh=adh, heads=ah,
                                               tm=_row_tile(n, PREP_TILE), v_transposed=False, layer=l, depth=depth,
                                               prev=None if acc is None else acc[2:4])
            q16, k16, v16 = (a.reshape(b, t, aw) for a in (q16, k16, v16))
        if prompt:
            o_c = attn_prompt(q16, k16, v16, diff_lambda[l], diff_norm_w[l], tq=_row_tile(t, ATTN_TILE), dh=adh,
                              lam_init=lam_init)
        else:
            o_c = attn_sample(q16, k16, v16, ck, cv, page_table, diff_lambda[l], diff_norm_w[l], layer=l, heads=ah, dh=adh,
                              lam_init=lam_init)
        x = merge(x, o_a.reshape(n, aw), o_b.reshape(n, aw), o_c.reshape(n, aw), z, wb, wo, mix_norm_post[l],
                  layer=l, tm=_row_tile(n, MERGE_TILE), gate_blk=gate_blk)
        x = ffn(x, ffn2_norm_pre[l], f2_in, f2_out, ffn2_norm_post[l], layer=l, tm=tmf, tf=FF_TILE)
        return x, new_conv, (new_gdn, new_hgrn, k32, v32)

    xp = x_prompt.reshape(bp * tp, d)
    xs = x_sample.reshape(bs * ts, d)
    conv_p, conv_s, acc_p, acc_s = [], [], None, None
    for l in range(depth):
        w_main, w_small = _split_w_in(w_in[l], sizes)
        xp, cp, acc_p = run_layer(l, xp, bp, tp, zeros_conv, zeros_gdn, zeros_hgrn, 0, tabs_p, True, w_main, w_small, acc_p)
        conv_p.append(cp)
        xs, cs, acc_s = run_layer(l, xs, bs, ts, state_gdn_conv, state_gdn, state_hgrn, l, tabs_s, False, w_main, w_small,
                                  acc_s)
        conv_s.append(cs)

    def finish(x, b, t, convs, acc):
        new_gdn, new_hgrn, k32, v32 = acc
        return (x.reshape(b, t, d), new_gdn, jnp.stack(convs), new_hgrn,
                k32.reshape(depth, b, t, ah, adh2), v32.reshape(depth, b, t, ah, adv))

    yp, p_gdn, p_conv, p_hgrn, p_k, p_v = finish(xp, bp, tp, conv_p, acc_p)
    ys, s_gdn, s_conv, s_hgrn, s_k, s_v = finish(xs, bs, ts, conv_s, acc_s)
    return (yp, ys, p_gdn, p_conv, p_hgrn, p_k, p_v, s_gdn, s_conv, s_hgrn, s_k, s_v)
```

```python
import functools
import math

import jax
import jax.numpy as jnp
from jax import lax
from jax.experimental import pallas as pl
from jax.experimental.pallas import tpu as pltpu

F32 = jnp.float32
BF16 = jnp.bfloat16
EPS = 1e-6
HI = lax.Precision.HIGHEST

VMEM_LIMIT_BYTES = 56 * 1024 * 1024


def _cparams(sem):
    return pltpu.CompilerParams(dimension_semantics=sem, vmem_limit_bytes=VMEM_LIMIT_BYTES)


def _rms(x, w):
    return x * lax.rsqrt(jnp.mean(x * x, axis=-1, keepdims=True) + EPS) * w


def _sigmoid(x):
    return 1.0 / (1.0 + jnp.exp(-x))


def _silu(x):
    return x * _sigmoid(x)


def _softplus(x):
    return jnp.maximum(x, 0.0) + jnp.log1p(jnp.exp(-jnp.abs(x)))


def _ffn_kernel(x_ref, pre_ref, wi_ref, wo_ref, post_ref, o_ref, a_scr, *, tf):
    dff = wo_ref.shape[0]
    h = _rms(x_ref[...], pre_ref[...]).astype(BF16)
    for j in range(dff // tf):
        cols = slice(j * tf, (j + 1) * tf)
        g = _dot(h, wi_ref[:, cols])
        u = _dot(h, wi_ref[:, dff + j * tf:dff + (j + 1) * tf])
        a_scr[:, cols] = (_silu(g) * u).astype(BF16)
    f = _dot(a_scr[...], wo_ref[...])
    o_ref[...] = x_ref[...] + 0.5 * _rms(f, post_ref[...])


def ffn(x, pre_w, w_in_bf, w_out_bf, post_w, *, layer, tm, tf):
    n, d = x.shape
    dff = w_out_bf.shape[1]
    resident = dict(pipeline_mode=pl.Buffered(1))
    return pl.pallas_call(
        functools.partial(_ffn_kernel, tf=tf),
        grid=(n // tm,),
        in_specs=[
            pl.BlockSpec((tm, d), lambda i: (i, 0)),
            pl.BlockSpec((1, d), lambda i: (0, 0)),
            pl.BlockSpec((None, d, 2 * dff), lambda i: (layer, 0, 0), **resident),
            pl.BlockSpec((None, dff, d), lambda i: (layer, 0, 0), **resident),
            pl.BlockSpec((1, d), lambda i: (0, 0)),
        ],
        out_specs=pl.BlockSpec((tm, d), lambda i: (i, 0)),
        out_shape=jax.ShapeDtypeStruct((n, d), F32),
        scratch_shapes=[pltpu.VMEM((tm, dff), BF16)],
        compiler_params=_cparams(("parallel",)),
        name="ffn",
    )(x, pre_w.reshape(1, d), w_in_bf, w_out_bf, post_w.reshape(1, d))


def _proj_kernel(x_ref, pre_ref, w_ref, ws_ref, z_ref, zs_ref, h_scr):
    j = pl.program_id(1)

    @pl.when(j == 0)
    def _():
        h = _rms(x_ref[...], pre_ref[...]).astype(BF16)
        h_scr[...] = h
        zs_ref[...] = jnp.dot(h, ws_ref[...], preferred_element_type=F32)

    z_ref[...] = jnp.dot(h_scr[...], w_ref[...], preferred_element_type=F32)


def proj(x, pre_w, w_main_bf, w_small_bf, *, tm, tn):
    n, d = x.shape
    cols = w_main_bf.shape[1]
    sc = w_small_bf.shape[1]
    return pl.pallas_call(
        _proj_kernel,
        grid=(n // tm, cols // tn),
        in_specs=[
            pl.BlockSpec((tm, d), lambda i, j: (i, 0)),
            pl.BlockSpec((1, d), lambda i, j: (0, 0)),
            pl.BlockSpec((d, tn), lambda i, j: (0, j)),
            pl.BlockSpec((d, sc), lambda i, j: (0, 0)),
        ],
        out_specs=[
            pl.BlockSpec((tm, tn), lambda i, j: (i, j)),
            pl.BlockSpec((tm, sc), lambda i, j: (i, 0)),
        ],
        out_shape=[jax.ShapeDtypeStruct((n, cols), F32), jax.ShapeDtypeStruct((n, sc), F32)],
        scratch_shapes=[pltpu.VMEM((tm, d), BF16)],
        compiler_params=_cparams(("parallel", "arbitrary")),
        name="proj",
    )(x, pre_w.reshape(1, d), w_main_bf, w_small_bf)


def _dot(a, b, precision=None):
    return jnp.dot(a, b, preferred_element_type=F32, precision=precision)


def _dot_nt(a, b, precision=None):
    return lax.dot_general(a, b, (((1,), (1,)), ((), ())), preferred_element_type=F32, precision=precision)


def _dot_tn(a, b, precision=None):
    return lax.dot_general(a, b, (((0,), (0,)), ((), ())), preferred_element_type=F32, precision=precision)


def _iota2(shape, dim):
    return lax.broadcasted_iota(jnp.int32, shape, dim)


def _split_bf16(x):
    hi = x.astype(BF16)
    return hi, (x - hi.astype(F32)).astype(BF16)


def _dot3(a, b):
    ah, al = _split_bf16(a)
    bh, bl = _split_bf16(b)
    return _dot(ah, bh) + (_dot(ah, bl) + _dot(al, bh))


def _tri_inverse_all(lmats, c):
    ri = _iota2((c, c), 0)
    ci = _iota2((c, c), 1)
    eye = jnp.where(ri == ci, 1.0, 0.0).astype(F32)
    nb = min(16, c)
    sh = int(math.log2(nb))
    same = (ri >> sh) == (ci >> sh)
    dot1 = lambda a, b: _dot(a.astype(BF16), b.astype(BF16))
    ps = [jnp.where(same, lm, 0.0) for lm in lmats]
    ts = [eye - p for p in ps]
    k = 2
    while k < nb:
        ps = [dot1(p, p) for p in ps]
        ts = [t + dot1(t, p) for t, p in zip(ts, ps)]
        k *= 2
    blk = nb
    while blk < c:
        s1 = int(math.log2(blk))
        offm = ((ri >> (s1 + 1)) == (ci >> (s1 + 1))) & ((ri >> s1) != (ci >> s1))
        tl = [dot1(t, jnp.where(offm, lm, 0.0)) for t, lm in zip(ts, lmats)]
        ts = [t - dot1(x, t) for t, x in zip(ts, tl)]
        blk *= 2
    return ts


def _gdn_kernel(qkv_ref, za_ref, zs_ref, cs_ref, s0_ref, cw_ref, alog_ref, dtb_ref, nw_ref,
                o_ref, sout_ref, cout_ref, xbuf, s_scr, *, c, nc, bb, heads, dk, dv):
    n = pl.program_id(1)
    nlast = pl.num_programs(1) - 1
    nqk = heads * dk
    r = nc * c

    @pl.when(n == 0)
    def _():
        xbuf[:, 5:8, :] = cs_ref[...]
        s_scr[...] = s0_ref[...]

    xbuf[:, 8:8 + r, :] = qkv_ref[...]

    @pl.when(n == nlast)
    def _():
        cout_ref[...] = xbuf[:, 5 + r:8 + r, :]

    ri = _iota2((c, c), 0)
    ci = _iota2((c, c), 1)
    incl = ri >= ci
    strict = ri > ci
    eye = jnp.where(ri == ci, 1.0, 0.0).astype(F32)
    rr = _iota2((r, r), 0)
    rc = _iota2((r, r), 1)
    sh = int(math.log2(c))
    blocktril = jnp.where((rr >= rc) & ((rr >> sh) == (rc >> sh)), 1.0, 0.0).astype(F32)
    nw = nw_ref[...]

    inst = []
    for bi in range(bb):
        y = xbuf[bi, 5:5 + r, :] * cw_ref[0:1, :]
        for j in range(1, 4):
            y = y + xbuf[bi, 5 + j:5 + j + r, :] * cw_ref[j:j + 1, :]
        y = _silu(y)
        zs = zs_ref[bi]
        beta_all = _sigmoid(zs)
        g_all = -jnp.exp(alog_ref[...]) * _softplus(zs + dtb_ref[...])
        gcum_all = _dot3(blocktril, g_all)
        gcum_t = gcum_all.T
        for i in range(nc):
            rows = slice(i * c, (i + 1) * c)
            for h in range(heads):
                inst.append(dict(
                    bi=bi, i=i, h=h,
                    q=y[rows, h * dk:(h + 1) * dk],
                    k=y[rows, nqk + h * dk:nqk + (h + 1) * dk],
                    v=y[rows, 2 * nqk + h * dv:2 * nqk + (h + 1) * dv],
                    beta=beta_all[rows, h:h + 1],
                    gc=gcum_all[rows, heads + h:heads + h + 1],
                    grow=gcum_t[heads + h:heads + h + 1, rows]))
    xbuf[:, 5:8, :] = xbuf[:, 5 + r:8 + r, :]

    for d in inst:
        q, k = d["q"], d["k"]
        d["q"] = q * lax.rsqrt(jnp.sum(q * q, axis=-1, keepdims=True) + EPS) * (dk ** -0.5)
        d["k"] = k * lax.rsqrt(jnp.sum(k * k, axis=-1, keepdims=True) + EPS)
        d["k16"] = d["k"].astype(BF16)
        d["kb"] = d["k"] * d["beta"]
        d["decay"] = jnp.where(incl, jnp.exp(jnp.minimum(d["gc"] - d["grow"], 0.0)), 0.0)
    for d in inst:
        d["lmat"] = jnp.where(strict, _dot_nt(d["kb"].astype(BF16), d["k16"]) * d["decay"], 0.0)
    tinvs = _tri_inverse_all([d["lmat"] for d in inst], c)
    for d, tinv in zip(inst, tinvs):
        eg = jnp.exp(d["gc"])
        rhs = jnp.concatenate([d["v"] * d["beta"], d["kb"] * eg], axis=-1)
        sol = rhs + _dot((tinv - eye).astype(BF16), rhs.astype(BF16))
        d["u"] = sol[:, :dv]
        d["w16"] = sol[:, dv:].astype(BF16)
        d["attn16"] = (_dot_nt(d["q"].astype(BF16), d["k16"]) * d["decay"]).astype(BF16)
        d["qe16"] = (d["q"] * eg).astype(BF16)
        gl = d["gc"][c - 1:c, :]
        d["kdec16"] = (d["k"] * jnp.exp(gl - d["gc"])).astype(BF16)
        d["egl"] = jnp.exp(gl)

    state = {(bi, h): s_scr[bi, h] for bi in range(bb) for h in range(heads)}
    for i in range(nc):
        for d in inst:
            if d["i"] != i:
                continue
            bi, h = d["bi"], d["h"]
            s = state[(bi, h)]
            s16 = s.astype(BF16)
            v_new = d["u"] - _dot(d["w16"], s16)
            vn16 = v_new.astype(BF16)
            o = _dot(d["qe16"], s16) + _dot(d["attn16"], vn16)
            state[(bi, h)] = s * d["egl"] + _dot_tn(d["kdec16"], vn16)
            za = za_ref[bi, i * c:(i + 1) * c, h * dv:(h + 1) * dv]
            o_ref[bi, i * c:(i + 1) * c, h * dv:(h + 1) * dv] = (_rms(o, nw) * _silu(za)).astype(o_ref.dtype)
    for (bi, h), s in state.items():
        s_scr[bi, h] = s

    @pl.when(n == nlast)
    def _():
        sout_ref[...] = s_scr[...]


def gdn(z3, zs3, conv_state, state, conv_w, a_log, dt_bias, norm_w, *, state_layer, c, nc, bb, qkv_blk, za_blk,
        layer, depth, prev, out_dtype):
    b, t, _ = z3.shape
    _, _, heads, dk, dv = state.shape
    cch = conv_state.shape[-1]
    sc = zs3.shape[-1]
    r = nc * c
    pad = jnp.zeros((1, sc), F32)
    alog_p = lax.dynamic_update_slice(pad, a_log.reshape(1, heads).astype(F32), (0, heads))
    dtb_p = lax.dynamic_update_slice(pad, dt_bias.reshape(1, heads).astype(F32), (0, heads))
    kern = functools.partial(_gdn_kernel, c=c, nc=nc, bb=bb, heads=heads, dk=dk, dv=dv)
    prev_ops, prev_specs, aliases = _alias_plan(prev, 9, (1,))
    return pl.pallas_call(
        _drop_refs(kern, 9, len(prev_ops)),
        grid=(b // bb, t // r),
        in_specs=[
            pl.BlockSpec((bb, r, cch), lambda i, n: (i, n, qkv_blk)),
            pl.BlockSpec((bb, r, heads * dv), lambda i, n: (i, n, za_blk)),
            pl.BlockSpec((bb, r, sc), lambda i, n: (i, n, 0)),
            pl.BlockSpec((None, bb, 3, cch), lambda i, n: (state_layer, i, 0, 0)),
            pl.BlockSpec((None, bb, heads, dk, dv), lambda i, n: (state_layer, i, 0, 0, 0)),
            pl.BlockSpec((4, cch), lambda i, n: (0, 0)),
            pl.BlockSpec((1, sc), lambda i, n: (0, 0)),
            pl.BlockSpec((1, sc), lambda i, n: (0, 0)),
            pl.BlockSpec((1, dv), lambda i, n: (0, 0)),
        ] + prev_specs,
        out_specs=[
            pl.BlockSpec((bb, r, heads * dv), lambda i, n: (i, n, 0)),
            pl.BlockSpec((None, bb, heads, dk, dv), lambda i, n: (layer, i, 0, 0, 0)),
            pl.BlockSpec((bb, 3, cch), lambda i, n: (i, 0, 0)),
        ],
        out_shape=[
            jax.ShapeDtypeStruct((b, t, heads * dv), out_dtype),
            jax.ShapeDtypeStruct((depth, b, heads, dk, dv), F32),
            jax.ShapeDtypeStruct((b, 3, cch), F32),
        ],
        scratch_shapes=[pltpu.VMEM((bb, 8 + r, cch), F32), pltpu.VMEM((bb, heads, dk, dv), F32)],
        input_output_aliases=aliases,
        compiler_params=_cparams(("parallel", "arbitrary")),
        name="gdn",
    )(z3, z3, zs3, conv_state, state, conv_w, alog_p, dtb_p, norm_w.reshape(1, dv), *prev_ops)


SB = 8


def _dot_cumsum(tril, x):
    hi = x.astype(BF16)
    r1 = x - hi.astype(F32)
    mid = r1.astype(BF16)
    lo = (r1 - mid.astype(F32)).astype(BF16)
    t16 = tril.astype(BF16)
    return _dot(t16, hi) + (_dot(t16, mid) + _dot(t16, lo))


def _hgrn_kernel(q_ref, f_ref, i_ref, g_ref, lbraw_ref, s0_ref, nw_ref, o_ref, sout_ref,
                 st_scr, *, c, nc, bb, heads, dk, dv, layer):
    n = pl.program_id(1)
    nlast = pl.num_programs(1) - 1
    bh = [(bi, h) for bi in range(bb) for h in range(heads)]

    @pl.when(n == 0)
    def _():
        for bi, h in bh:
            st_scr[bi * heads + h] = s0_ref[bi, h].T

    raw = lbraw_ref[...]
    e = jnp.exp(raw - jnp.max(raw, axis=0, keepdims=True))
    sm = e / jnp.sum(e, axis=0, keepdims=True)
    lb = jnp.zeros((1, heads * dk), F32)
    for l in range(1, layer + 1):
        lb = lb + sm[l:l + 1, :]

    r = bb * nc * c
    flat = lambda ref: ref[...].reshape(r, ref.shape[-1])
    zf = flat(f_ref)
    logsig = jnp.minimum(zf, 0.0) - jnp.log1p(jnp.exp(-jnp.abs(zf)))
    la = jnp.log(lb)
    lbb = jnp.log1p(-lb) + logsig
    logf = jnp.maximum(la, lbb) + jnp.log1p(jnp.exp(-jnp.abs(la - lbb)))
    kh = (1.0 - lb) * _sigmoid(-zf)
    qh = _silu(flat(q_ref)) * (dk ** -0.5)
    vv = flat(i_ref)
    gate = _sigmoid(flat(g_ref))

    ri = _iota2((r, r), 0)
    ci = _iota2((r, r), 1)
    sh = int(math.log2(c))
    tril = jnp.where((ri >= ci) & ((ri >> sh) == (ci >> sh)), 1.0, 0.0).astype(F32)
    bcum = _dot_cumsum(tril, logf)
    fgate = jnp.exp(logf)
    nsb = c // SB
    sub = _iota2((nsb, SB, dk), 1)
    nw = nw_ref[...]
    g3 = lambda x: x.reshape(nsb, SB, x.shape[-1])
    states = {(bi, h): st_scr[bi * heads + h] for bi, h in bh}

    for ic, bi, h in [(ic, bi, h) for ic in range(nc) for bi, h in bh]:
        rows = slice((bi * nc + ic) * c, (bi * nc + ic + 1) * c)
        ks = slice(h * dk, (h + 1) * dk)
        vs = slice(h * dv, (h + 1) * dv)
        q = qh[rows, ks]
        k = kh[rows, ks]
        b = bcum[rows, ks]
        v = vv[rows, vs]
        st = states[(bi, h)]
        o = _dot_nt((q * jnp.exp(b)).astype(BF16), st.astype(BF16))
        q3, k3, v3, f3 = g3(q), g3(k), g3(v), g3(fgate[rows, ks])
        o3 = jnp.sum(q3 * k3, axis=-1, keepdims=True) * v3
        e = None
        for d in range(1, SB):
            fd = f3 if d == 1 else pltpu.roll(f3, d - 1, 1)
            e = fd if e is None else e * fd
            m = jnp.where(sub >= d, q3 * pltpu.roll(k3, d, 1) * e, 0.0)
            o3 = o3 + jnp.sum(m, axis=-1, keepdims=True) * pltpu.roll(v3, d, 1)
        o = o + o3.reshape(c, dv)
        if nsb > 1:
            b3 = g3(b)
            rend = jnp.broadcast_to(b3[:, SB - 1:SB, :], b3.shape).reshape(c, dk)
            khat = k * jnp.exp(jnp.minimum(rend - b, 0.0))
            zq = jnp.zeros((c, dk), F32)
            qs, kk = [], []
            for jb in range(nsb - 1):
                lo, hi = jb * SB, (jb + 1) * SB
                rj = b[hi - 1:hi, :]
                qpart = q[hi:, :] * jnp.exp(jnp.minimum(b[hi:, :] - rj, 0.0))
                qs.append(jnp.concatenate([zq[:hi], qpart], axis=0))
                kparts = [khat[lo:hi]] if lo == 0 else [zq[:lo], khat[lo:hi]]
                kk.append(jnp.concatenate(kparts + [zq[hi:]], axis=0))
            amat = _dot_nt(jnp.concatenate(qs, axis=-1).astype(BF16), jnp.concatenate(kk, axis=-1).astype(BF16))
            o = o + _dot(amat.astype(BF16), v.astype(BF16))
        bl = b[c - 1:c, :]
        kdec = k * jnp.exp(bl - b)
        states[(bi, h)] = st * jnp.exp(bl) + _dot_tn(v.astype(BF16), kdec.astype(BF16))
        o_ref[bi, ic * c:(ic + 1) * c, vs] = (_rms(o, nw) * gate[rows, vs]).astype(o_ref.dtype)
    for bi, h in bh:
        st_scr[bi * heads + h] = states[(bi, h)]

    @pl.when(n == nlast)
    def _():
        for bi, h in bh:
            sout_ref[bi, h] = st_scr[bi * heads + h].T


def hgrn(z3, lb_raw, state, norm_w, *, state_layer, c, nc, bb, layer, q_blk, prev, out_dtype):
    b, t, _ = z3.shape
    _, _, heads, dk, dv = state.shape
    depth = lb_raw.shape[0]
    wk = heads * dk
    assert c % SB == 0
    kern = functools.partial(_hgrn_kernel, c=c, nc=nc, bb=bb, heads=heads, dk=dk, dv=dv, layer=layer)
    r = nc * c
    zspec = lambda off: pl.BlockSpec((bb, r, wk), lambda i, n: (i, n, q_blk + off))
    prev_ops, prev_specs, aliases = _alias_plan(prev, 7, (1,))
    return pl.pallas_call(
        _drop_refs(kern, 7, len(prev_ops)),
        grid=(b // bb, t // r),
        in_specs=[
            zspec(0), zspec(1), zspec(2), zspec(3),
            pl.BlockSpec((depth, wk), lambda i, n: (0, 0)),
            pl.BlockSpec((None, bb, heads, dk, dv), lambda i, n: (state_layer, i, 0, 0, 0)),
            pl.BlockSpec((1, dv), lambda i, n: (0, 0)),
        ] + prev_specs,
        out_specs=[
            pl.BlockSpec((bb, r, heads * dv), lambda i, n: (i, n, 0)),
            pl.BlockSpec((None, bb, heads, dk, dv), lambda i, n: (layer, i, 0, 0, 0)),
        ],
        out_shape=[
            jax.ShapeDtypeStruct((b, t, heads * dv), out_dtype),
            jax.ShapeDtypeStruct((depth, b, heads, dk, dv), F32),
        ],
        scratch_shapes=[pltpu.VMEM((bb * heads, dv, dk), F32)],
        input_output_aliases=aliases,
        compiler_params=_cparams(("parallel", "arbitrary")),
        name="hgrn",
    )(z3, z3, z3, z3, lb_raw, state, norm_w.reshape(1, dv), *prev_ops)


def rope_tables(t, pos0, dh, theta):
    rd = dh // 4
    half = rd // 2
    inv = jnp.power(jnp.float32(theta), -jnp.arange(half, dtype=F32) / half)
    ang = (pos0 + jnp.arange(t)).astype(F32)[:, None] * inv
    cos, sin = jnp.cos(ang), jnp.sin(ang)
    d = jnp.arange(128) % dh
    f = d % half
    cc = jnp.where(d < rd, cos[:, f], 1.0)
    s1 = jnp.where(d < half, -sin[:, f], 0.0)
    s2 = jnp.where((d >= half) & (d < rd), sin[:, f], 0.0)
    return cc.astype(F32), s1.astype(F32), s2.astype(F32)


def _prep_kernel(q_ref, k_ref, v_ref, c_ref, s1_ref, s2_ref, q16_ref, k32_ref, k16_ref, v32_ref, v16_ref,
                 *, scale, half, heads, v_transposed):
    tm, w = q_ref.shape[1:]
    hw = w // heads
    reps = w // c_ref.shape[-1]
    cc = jnp.concatenate([c_ref[...]] * reps, axis=-1)
    s1 = jnp.concatenate([s1_ref[...]] * reps, axis=-1)
    s2 = jnp.concatenate([s2_ref[...]] * reps, axis=-1)

    def rope(x):
        return x * cc + pltpu.roll(x, w - half, 1) * s1 + pltpu.roll(x, half, 1) * s2

    q16_ref[0] = (rope(q_ref[0]) * scale).astype(BF16)
    k = rope(k_ref[0])
    k16_ref[0] = k.astype(BF16)
    v = v_ref[0]
    v16_ref[0] = (v.T if v_transposed else v).astype(BF16)
    for h in range(heads):
        k32_ref[0, pl.ds(h, tm, stride=heads), :] = k[:, h * hw:(h + 1) * hw]
        v32_ref[0, pl.ds(h, tm, stride=heads), :] = v[:, h * hw:(h + 1) * hw]


def _drop_refs(kern, start, count):
    def wrapped(*refs):
        return kern(*refs[:start], *refs[start + count:])
    return wrapped


def _alias_plan(prev, n_in, out_idx):
    prev = [] if prev is None else list(prev)
    specs = [pl.BlockSpec(memory_space=pl.ANY)] * len(prev)
    return prev, specs, {n_in + k: out_idx[k] for k in range(len(prev))}


def qkv_prep(z3, tables, *, q_blk, dh, heads, tm, v_transposed, layer, depth, prev):
    b, t, _ = z3.shape
    w = tables[0].shape[-1] * (heads * 2 * dh // tables[0].shape[-1])
    hw = w // heads
    cc, s1, s2 = tables
    kern = functools.partial(_prep_kernel, scale=dh ** -0.5 * math.log2(math.e), half=dh // 8, heads=heads,
                             v_transposed=v_transposed)
    zspec = lambda off: pl.BlockSpec((1, tm, w), lambda i, n: (i, n, q_blk + off))
    tspec = pl.BlockSpec((tm, cc.shape[-1]), lambda i, n: (n, 0))
    ospec = pl.BlockSpec((1, tm, w), lambda i, n: (i, n, 0))
    cspec = pl.BlockSpec((None, 1, tm * heads, hw), lambda i, n: (layer, i, n, 0))
    sd = lambda dt: jax.ShapeDtypeStruct((b, t, w), dt)
    sc = jax.ShapeDtypeStruct((depth, b, t * heads, hw), F32)
    vspec = pl.BlockSpec((1, w, tm), lambda i, n: (i, 0, n)) if v_transposed else ospec
    vsd = jax.ShapeDtypeStruct((b, w, t), BF16) if v_transposed else sd(BF16)
    prev_ops, prev_specs, aliases = _alias_plan(prev, 6, (1, 3))
    return pl.pallas_call(
        _drop_refs(kern, 6, len(prev_ops)),
        grid=(b, t // tm),
        in_specs=[zspec(0), zspec(1), zspec(2), tspec, tspec, tspec] + prev_specs,
        out_specs=[ospec, cspec, ospec, cspec, vspec],
        out_shape=[sd(BF16), sc, sd(BF16), sc, vsd],
        input_output_aliases=aliases,
        compiler_params=_cparams(("parallel", "parallel")),
        name="qkv_prep",
    )(z3, z3, z3, cc, s1, s2, *prev_ops)


def _lambda(lam_ref, lam_init):
    lm = lam_ref[...]
    a = jnp.sum(lm[0:1, :] * lm[1:2, :], axis=-1, keepdims=True)
    b = jnp.sum(lm[2:3, :] * lm[3:4, :], axis=-1, keepdims=True)
    return jnp.exp(a) - jnp.exp(b) + lam_init


ONES_ROWS = 16


def _attn_prompt_kernel(qt_ref, kt_ref, q_ref, k_ref, vt_ref, lam_ref, nw_ref, o_ref,
                        qs_scr, m_scr, acc_scr, *, tq, dh, dv, hpb, qchunk, lam_init):
    p = pl.program_id(2)
    qi = qt_ref[p]
    ki = kt_ref[p]

    @pl.when(ki == 0)
    def _():
        for hh in range(hpb):
            q = q_ref[0, :, hh * dv:(hh + 1) * dv]
            lane = _iota2(q.shape, 1)
            zero = jnp.zeros_like(q)
            qs_scr[hh, 0:tq, :] = jnp.where(lane < dh, q, zero)
            qs_scr[hh, tq:2 * tq, :] = jnp.where(lane >= dh, q, zero)
        m_scr[...] = jnp.full(m_scr.shape, -jnp.inf, F32)
        acc_scr[...] = jnp.zeros_like(acc_scr)

    def update(diag):
        for hh in range(hpb):
            k = k_ref[0, :, hh * dv:(hh + 1) * dv]
            vt = vt_ref[0, hh * dv:(hh + 1) * dv, :]
            vt1 = jnp.concatenate([vt, jnp.ones((ONES_ROWS, vt.shape[1]), BF16)], axis=0)
            for c0 in range(0, 2 * tq, qchunk):
                cs = slice(c0, c0 + qchunk)
                st = _dot_nt(k, qs_scr[hh, cs, :])
                if diag:
                    key = _iota2(st.shape, 0)
                    qry = (_iota2(st.shape, 1) + c0) & (tq - 1)
                    st = jnp.where(key <= qry, st, -jnp.inf)
                m_prev = m_scr[hh, :, cs]
                m_new = jnp.maximum(m_prev, jnp.max(st, axis=0, keepdims=True))
                alpha = jnp.exp2(m_prev - m_new)
                pr = jnp.exp2(st - m_new).astype(BF16)
                acc_scr[hh, :, cs] = alpha * acc_scr[hh, :, cs] + _dot(vt1, pr)
                m_scr[hh, :, cs] = m_new

    @pl.when(ki < qi)
    def _():
        update(False)

    @pl.when(ki == qi)
    def _():
        update(True)
        lam = _lambda(lam_ref, lam_init)
        for hh in range(hpb):
            o1 = acc_scr[hh, 0:dv, 0:tq] / acc_scr[hh, dv:dv + 1, 0:tq]
            o2 = acc_scr[hh, 0:dv, tq:2 * tq] / acc_scr[hh, dv:dv + 1, tq:2 * tq]
            o = (o1 - lam * o2).T
            o_ref[0, :, hh * dv:(hh + 1) * dv] = (_rms(o, nw_ref[...]) * (1.0 - lam_init)).astype(o_ref.dtype)


def attn_prompt(q16, k16, vt16, lam, norm_w, *, tq, dh, lam_init):
    b, t, w = q16.shape
    dv = norm_w.shape[-1]
    heads = w // dv
    nq = t // tq
    pairs = [(i, j) for i in range(nq) for j in range(i + 1)]
    qt = jnp.asarray([pq for pq, _ in pairs], jnp.int32)
    kt = jnp.asarray([pk for _, pk in pairs], jnp.int32)
    hpb = math.gcd(heads, ATTN_HEADS_PER_STEP)
    kern = functools.partial(_attn_prompt_kernel, tq=tq, dh=dh, dv=dv, hpb=hpb, qchunk=min(ATTN_QCHUNK, 2 * tq),
                             lam_init=lam_init)
    grid_spec = pltpu.PrefetchScalarGridSpec(
        num_scalar_prefetch=2,
        grid=(b, heads // hpb, len(pairs)),
        in_specs=[
            pl.BlockSpec((1, tq, hpb * dv), lambda i, h, p, qt, kt: (i, qt[p], h)),
            pl.BlockSpec((1, tq, hpb * dv), lambda i, h, p, qt, kt: (i, kt[p], h)),
            pl.BlockSpec((1, hpb * dv, tq), lambda i, h, p, qt, kt: (i, h, kt[p])),
            pl.BlockSpec(lam.shape, lambda i, h, p, qt, kt: (0, 0)),
            pl.BlockSpec((1, dv), lambda i, h, p, qt, kt: (0, 0)),
        ],
        out_specs=pl.BlockSpec((1, tq, hpb * dv), lambda i, h, p, qt, kt: (i, qt[p], h)),
        scratch_shapes=[
            pltpu.VMEM((hpb, 2 * tq, dv), BF16),
            pltpu.VMEM((hpb, 1, 2 * tq), F32),
            pltpu.VMEM((hpb, dv + ONES_ROWS, 2 * tq), F32),
        ],
    )
    return pl.pallas_call(
        kern,
        grid_spec=grid_spec,
        out_shape=jax.ShapeDtypeStruct((b, t, w), BF16),
        compiler_params=_cparams(("parallel", "parallel", "arbitrary")),
        name="attn_prompt",
    )(qt, kt, q16, k16, vt16, lam, norm_w.reshape(1, dv))


def _attn_sample_kernel(pt_ref, q_ref, kc_ref, vc_ref, lam_ref, nw_ref, *rest,
                        npages, tq, heads, dh, dv, page, lam_init):
    k_refs = rest[:npages]
    v_refs = rest[npages:2 * npages]
    o_ref = rest[2 * npages]
    lam = _lambda(lam_ref, lam_init)
    nw = nw_ref[...]
    q = q_ref[0].astype(F32)
    kc = kc_ref[0]
    vc = vc_ref[0]
    lane = _iota2((tq, dv), 1)
    qidx = _iota2((2 * tq, tq), 0) % tq
    kidx = _iota2((2 * tq, tq), 1)
    hsl = [slice(h * dv, (h + 1) * dv) for h in range(heads)]

    def head_rows(refs, h):
        return jnp.concatenate([r[0, 0, pl.ds(h, page, stride=heads), :].astype(BF16) for r in refs], axis=0)

    qrows = [jnp.concatenate([jnp.where(lane < dh, q[:, hs], 0.0), jnp.where(lane >= dh, q[:, hs], 0.0)],
                             axis=0).astype(BF16) for hs in hsl]
    s_past = [_dot_nt(qrows[h], head_rows(k_refs, h)) for h in range(heads)]
    s_cur = [jnp.where(kidx <= qidx, _dot_nt(qrows[h], kc[:, hsl[h]]), -jnp.inf) for h in range(heads)]
    ms = [jnp.maximum(jnp.max(sp, axis=-1, keepdims=True), jnp.max(sc, axis=-1, keepdims=True))
          for sp, sc in zip(s_past, s_cur)]
    p_past = [jnp.exp2(sp - m) for sp, m in zip(s_past, ms)]
    p_cur = [jnp.exp2(sc - m) for sc, m in zip(s_cur, ms)]
    invs = [1.0 / (jnp.sum(pp, axis=-1, keepdims=True) + jnp.sum(pc, axis=-1, keepdims=True))
            for pp, pc in zip(p_past, p_cur)]

    def diff(pp, inv):
        pn = pp * inv
        return pn[0:tq] - lam * pn[tq:2 * tq]

    for h in range(heads):
        o = (_dot(diff(p_past[h], invs[h]).astype(BF16), head_rows(v_refs, h))
             + _dot(diff(p_cur[h], invs[h]), vc[:, hsl[h]].astype(F32)))
        o_ref[0, :, hsl[h]] = _rms(o, nw) * (1.0 - lam_init)


def attn_sample(q16, k16, v16, cache_k, cache_v, page_table, lam, norm_w, *, layer, heads, dh, lam_init):
    b, tq, w = q16.shape
    dv = norm_w.shape[-1]
    npages = page_table.shape[1]
    prow = cache_k.shape[2]
    kern = functools.partial(_attn_sample_kernel, npages=npages, tq=tq, heads=heads, dh=dh, dv=dv,
                             page=prow // heads, lam_init=lam_init)
    cur = pl.BlockSpec((1, tq, w), lambda i, pt: (i, 0, 0))

    def page_spec(j):
        return pl.BlockSpec((1, 1, prow, dv), lambda i, pt: (layer, pt[i, j], 0, 0))

    grid_spec = pltpu.PrefetchScalarGridSpec(
        num_scalar_prefetch=1,
        grid=(b,),
        in_specs=[cur, cur, cur,
                  pl.BlockSpec(lam.shape, lambda i, pt: (0, 0)),
                  pl.BlockSpec((1, dv), lambda i, pt: (0, 0))]
        + [page_spec(j) for j in range(npages)] * 2,
        out_specs=pl.BlockSpec((1, tq, w), lambda i, pt: (i, 0, 0)),
    )
    return pl.pallas_call(
        kern,
        grid_spec=grid_spec,
        out_shape=jax.ShapeDtypeStruct((b, tq, w), F32),
        compiler_params=_cparams(("parallel",)),
        name="attn_sample",
    )(page_table, q16, k16, v16, lam, norm_w.reshape(1, dv), *([cache_k] * npages), *([cache_v] * npages))


def _merge_kernel(x_ref, oa_ref, ob_ref, oc_ref, g0_ref, g1_ref, g2_ref, wb_ref, wo_ref, post_ref, o_ref):
    y = None
    for i, (o_i, g_i) in enumerate(((oa_ref, g0_ref), (ob_ref, g1_ref), (oc_ref, g2_ref))):
        ys = _dot(o_i[...].astype(BF16), wb_ref[i])
        t = _sigmoid(g_i[...]) * ys
        y = t if y is None else y + t
    y2 = _dot(y.astype(BF16), wo_ref[...])
    o_ref[...] = x_ref[...] + _rms(y2, post_ref[...])


def merge(x, o_a, o_b, o_c, z, w_branch_bf, w_out_bf, post_w, *, layer, tm, gate_blk):
    n, d = x.shape
    bw = o_a.shape[-1]
    nb = w_branch_bf.shape[1]
    ospec = pl.BlockSpec((tm, bw), lambda i: (i, 0))
    gspec = lambda k: pl.BlockSpec((tm, d), lambda i: (i, gate_blk + k))
    return pl.pallas_call(
        _merge_kernel,
        grid=(n // tm,),
        in_specs=[
            pl.BlockSpec((tm, d), lambda i: (i, 0)),
            ospec, ospec, ospec, gspec(0), gspec(1), gspec(2),
            pl.BlockSpec((None, nb, bw, d), lambda i: (layer, 0, 0, 0)),
            pl.BlockSpec((None, d, d), lambda i: (layer, 0, 0)),
            pl.BlockSpec((1, d), lambda i: (0, 0)),
        ],
        out_specs=pl.BlockSpec((tm, d), lambda i: (i, 0)),
        out_shape=jax.ShapeDtypeStruct((n, d), F32),
        compiler_params=_cparams(("parallel",)),
        name="merge",
    )(x, o_a, o_b, o_c, z, z, z, w_branch_bf, w_out_bf, post_w.reshape(1, d))


ROPE_THETA = 500000.0
SCAN_CHUNK = 64
GDN_CHUNKS_PER_STEP = 4
HGRN_CHUNKS_PER_STEP = 4
HGRN_INSTANCES = 32
GDN_INSTANCES = 16
ROW_TILE = 1024
FFN_ROW_TILE = 512
FF_TILE = 256
PROJ_COL_TILES = 4
ATTN_TILE = 512
ATTN_QCHUNK = 1024
ATTN_HEADS_PER_STEP = 4
PREP_TILE = 512
MERGE_TILE = 512
SMALL_COLS = 128


def _split_w_in(w_in_l, sizes):
    offs = [0]
    for s in sizes:
        offs.append(offs[-1] + s)
    seg = lambda i: w_in_l[:, offs[i]:offs[i + 1]]
    main = jnp.concatenate([seg(11), seg(0), seg(1), seg(4), seg(5), seg(6), seg(7), seg(8), seg(9), seg(10)], axis=1)
    small = jnp.concatenate([seg(2), seg(3)], axis=1)
    small = jnp.pad(small, ((0, 0), (0, SMALL_COLS - small.shape[1])))
    return main.astype(BF16), small.astype(BF16)


def _row_tile(n, pref):
    return pref if n % pref == 0 else n


def kernel(x_prompt, x_sample, state_gdn, state_gdn_conv, state_hgrn, cache_k, cache_v, page_table,
           ffn1_norm_pre, ffn1_norm_post, ffn1_w_in, ffn1_w_out, mix_norm_pre, mix_norm_post, w_in,
           gdn_conv_w, gdn_a_log, gdn_dt_bias, gdn_norm_w, hgrn_lb_raw, hgrn_norm_w, diff_lambda,
           diff_norm_w, w_branch, w_out, ffn2_norm_pre, ffn2_norm_post, ffn2_w_in, ffn2_w_out):
    depth = w_in.shape[0]
    bp, tp, d = x_prompt.shape
    bs, ts, _ = x_sample.shape
    _, _, gh, gdk, gdv = state_gdn.shape
    cch = state_gdn_conv.shape[-1]
    _, _, hh, hdk, hdv = state_hgrn.shape
    _, n_pool, page, ah, adh2 = cache_k.shape
    adh = adh2 // 2
    adv = cache_v.shape[-1]
    aw = ah * adv
    assert ah * adh2 == aw and hh * hdk == aw and hh * hdv == aw and gh * gdv == aw and cch == 3 * aw and d == 2 * aw
    sizes = (cch, gh * gdv, gh, gh, hh * hdk, hh * hdk, hh * hdv, hh * hdv, ah * adh2, ah * adh2, aw, 3 * d)
    gate_blk, qkv_blk, za_blk, hq_blk, aq_blk = 0, (3 * d) // cch, (3 * d + cch) // aw, (3 * d + cch) // aw + 1, (3 * d + cch) // aw + 5
    past_len = page_table.shape[1] * page
    ck = cache_k.reshape(depth, n_pool, page * ah, adh2)
    cv = cache_v.reshape(depth, n_pool, page * ah, adv)
    tabs_p = rope_tables(tp, 0, adh, ROPE_THETA)
    tabs_s = tuple(jnp.tile(a, (bs, 1)) for a in rope_tables(ts, past_len, adh, ROPE_THETA))
    zeros_conv = jnp.zeros((1, bp, 3, cch), F32)
    zeros_gdn = jnp.zeros((1, bp, gh, gdk, gdv), F32)
    zeros_hgrn = jnp.zeros((1, bp, hh, hdk, hdv), F32)

    f1_in, f1_out = ffn1_w_in.astype(BF16), ffn1_w_out.astype(BF16)
    f2_in, f2_out = ffn2_w_in.astype(BF16), ffn2_w_out.astype(BF16)
    wb, wo = w_branch.astype(BF16), w_out.astype(BF16)

    def run_layer(l, x, b, t, conv_state, gdn_state, hgrn_state, sl, tabs, prompt, w_main, w_small, acc):
        n = b * t
        tm = _row_tile(n, ROW_TILE)
        lam_init = 0.8 - 0.6 * math.exp(-0.3 * l)
        tmf = _row_tile(n, FFN_ROW_TILE)
        x = ffn(x, ffn1_norm_pre[l], f1_in, f1_out, ffn1_norm_post[l], layer=l, tm=tmf, tf=FF_TILE)
        z, zs = proj(x, mix_norm_pre[l], w_main, w_small, tm=tm, tn=w_main.shape[1] // PROJ_COL_TILES)
        z3 = z.reshape(b, t, z.shape[-1])
        zs3 = zs.reshape(b, t, SMALL_COLS)
        c = math.gcd(SCAN_CHUNK, t)
        branch_dtype = BF16 if c % 16 == 0 else F32
        o_a, new_gdn, new_conv = gdn(z3, zs3, conv_state, gdn_state, gdn_conv_w[l], gdn_a_log[l], gdn_dt_bias[l],
                                     gdn_norm_w[l], state_layer=sl, c=c, nc=min(GDN_CHUNKS_PER_STEP, t // c),
                                     bb=math.gcd(b, max(1, GDN_INSTANCES // (gh * min(GDN_CHUNKS_PER_STEP, t // c)))),
                                     qkv_blk=qkv_blk, za_blk=za_blk, layer=l, depth=depth,
                                     prev=None if acc is None else acc[0:1], out_dtype=branch_dtype)
        o_b, new_hgrn = hgrn(z3, hgrn_lb_raw, hgrn_state, hgrn_norm_w[l], state_layer=sl, c=c,
                             nc=min(HGRN_CHUNKS_PER_STEP, t // c),
                             bb=math.gcd(b, max(1, HGRN_INSTANCES // (hh * min(HGRN_CHUNKS_PER_STEP, t // c)))),
                             layer=l, q_blk=hq_blk,
                             prev=None if acc is None else acc[1:2], out_dtype=branch_dtype)
        if prompt:
            q16, k32, k16, v32, v16 = qkv_prep(z3, tabs, q_blk=aq_blk, dh=adh, heads=ah, tm=_row_tile(t, PREP_TILE),
                                               v_transposed=True, layer=l, depth=depth,
                                               prev=None if acc is None else acc[2:4])
        else:
            q16, k32, k16, v32, v16 = qkv_prep(z.reshape(1, n, z.shape[-1]), tabs, q_blk=aq_blk, dh=adh, heads=ah,
                                               tm=_row_tile(n, PREP_TILE), v_transposed=False, layer=l, depth=depth,
                                               prev=None if acc is None else acc[2:4])
            q16, k16, v16 = (a.reshape(b, t, aw) for a in (q16, k16, v16))
        if prompt:
            o_c = attn_prompt(q16, k16, v16, diff_lambda[l], diff_norm_w[l], tq=_row_tile(t, ATTN_TILE), dh=adh,
                              lam_init=lam_init)
        else:
            o_c = attn_sample(q16, k16, v16, ck, cv, page_table, diff_lambda[l], diff_norm_w[l], layer=l, heads=ah, dh=adh,
                              lam_init=lam_init)
        x = merge(x, o_a.reshape(n, aw), o_b.reshape(n, aw), o_c.reshape(n, aw), z, wb, wo, mix_norm_post[l],
                  layer=l, tm=_row_tile(n, MERGE_TILE), gate_blk=gate_blk)
        x = ffn(x, ffn2_norm_pre[l], f2_in, f2_out, ffn2_norm_post[l], layer=l, tm=tmf, tf=FF_TILE)
        return x, new_conv, (new_gdn, new_hgrn, k32, v32)

    xp = x_prompt.reshape(bp * tp, d)
    xs = x_sample.reshape(bs * ts, d)
    conv_p, conv_s, acc_p, acc_s = [], [], None, None
    for l in range(depth):
        w_main, w_small = _split_w_in(w_in[l], sizes)
        xp, cp, acc_p = run_layer(l, xp, bp, tp, zeros_conv, zeros_gdn, zeros_hgrn, 0, tabs_p, True, w_main, w_small, acc_p)
        conv_p.append(cp)
        xs, cs, acc_s = run_layer(l, xs, bs, ts, state_gdn_conv, state_gdn, state_hgrn, l, tabs_s, False, w_main, w_small,
                                  acc_s)
        conv_s.append(cs)

    def finish(x, b, t, convs, acc):
        new_gdn, new_hgrn, k32, v32 = acc
        return (x.reshape(b, t, d), new_gdn, jnp.stack(convs), new_hgrn,
                k32.reshape(depth, b, t, ah, adh2), v32.reshape(depth, b, t, ah, adv))

    yp, p_gdn, p_conv, p_hgrn, p_k, p_v = finish(xp, bp, tp, conv_p, acc_p)
    ys, s_gdn, s_conv, s_hgrn, s_k, s_v = finish(xs, bs, ts, conv_s, acc_s)
    return (yp, ys, p_gdn, p_conv, p_hgrn, p_k, p_v, s_gdn, s_conv, s_hgrn, s_k, s_v)
```

```python
import functools
import math

import jax
import jax.numpy as jnp
from jax import lax
from jax.experimental import pallas as pl
from jax.experimental.pallas import tpu as pltpu

F32 = jnp.float32
BF16 = jnp.bfloat16
EPS = 1e-6
HI = lax.Precision.HIGHEST

VMEM_LIMIT_BYTES = 56 * 1024 * 1024


def _cparams(sem):
    return pltpu.CompilerParams(dimension_semantics=sem, vmem_limit_bytes=VMEM_LIMIT_BYTES)


def _rms(x, w):
    return x * lax.rsqrt(jnp.mean(x * x, axis=-1, keepdims=True) + EPS) * w


def _sigmoid(x):
    return 1.0 / (1.0 + jnp.exp(-x))


def _silu(x):
    return x * _sigmoid(x)


def _softplus(x):
    return jnp.maximum(x, 0.0) + jnp.log1p(jnp.exp(-jnp.abs(x)))


def _ffn_body(x, pre_ref, wi_ref, wo_ref, post_ref, a_scr, tf):
    dff = wo_ref.shape[0]
    h = _rms(x, pre_ref[...]).astype(BF16)
    for j in range(dff // tf):
        cols = slice(j * tf, (j + 1) * tf)
        g = _dot(h, wi_ref[:, cols])
        u = _dot(h, wi_ref[:, dff + j * tf:dff + (j + 1) * tf])
        a_scr[:, cols] = (_silu(g) * u).astype(BF16)
    f = _dot(a_scr[...], wo_ref[...])
    return x + 0.5 * _rms(f, post_ref[...])


def _ffn_kernel(x_ref, pre_ref, wi_ref, wo_ref, post_ref, o_ref, a_scr, *, tf):
    o_ref[...] = _ffn_body(x_ref[...], pre_ref, wi_ref, wo_ref, post_ref, a_scr, tf)


def ffn(x, pre_w, w_in_bf, w_out_bf, post_w, *, layer, tm, tf):
    n, d = x.shape
    dff = w_out_bf.shape[1]
    resident = dict(pipeline_mode=pl.Buffered(1))
    return pl.pallas_call(
        functools.partial(_ffn_kernel, tf=tf),
        grid=(n // tm,),
        in_specs=[
            pl.BlockSpec((tm, d), lambda i: (i, 0)),
            pl.BlockSpec((1, d), lambda i: (0, 0)),
            pl.BlockSpec((None, d, 2 * dff), lambda i: (layer, 0, 0), **resident),
            pl.BlockSpec((None, dff, d), lambda i: (layer, 0, 0), **resident),
            pl.BlockSpec((1, d), lambda i: (0, 0)),
        ],
        out_specs=pl.BlockSpec((tm, d), lambda i: (i, 0)),
        out_shape=jax.ShapeDtypeStruct((n, d), F32),
        scratch_shapes=[pltpu.VMEM((tm, dff), BF16)],
        compiler_params=_cparams(("parallel",)),
        name="ffn",
    )(x, pre_w.reshape(1, d), w_in_bf, w_out_bf, post_w.reshape(1, d))


def _proj_kernel(x_ref, pre_ref, w_ref, ws_ref, z_ref, zs_ref, h_scr):
    j = pl.program_id(1)

    @pl.when(j == 0)
    def _():
        h = _rms(x_ref[...], pre_ref[...]).astype(BF16)
        h_scr[...] = h
        zs_ref[...] = jnp.dot(h, ws_ref[...], preferred_element_type=F32)

    z_ref[...] = jnp.dot(h_scr[...], w_ref[...], preferred_element_type=F32)


def proj(x, pre_w, w_main_bf, w_small_bf, *, tm, tn):
    n, d = x.shape
    cols = w_main_bf.shape[1]
    sc = w_small_bf.shape[1]
    return pl.pallas_call(
        _proj_kernel,
        grid=(n // tm, cols // tn),
        in_specs=[
            pl.BlockSpec((tm, d), lambda i, j: (i, 0)),
            pl.BlockSpec((1, d), lambda i, j: (0, 0)),
            pl.BlockSpec((d, tn), lambda i, j: (0, j)),
            pl.BlockSpec((d, sc), lambda i, j: (0, 0)),
        ],
        out_specs=[
            pl.BlockSpec((tm, tn), lambda i, j: (i, j)),
            pl.BlockSpec((tm, sc), lambda i, j: (i, 0)),
        ],
        out_shape=[jax.ShapeDtypeStruct((n, cols), F32), jax.ShapeDtypeStruct((n, sc), F32)],
        scratch_shapes=[pltpu.VMEM((tm, d), BF16)],
        compiler_params=_cparams(("parallel", "arbitrary")),
        name="proj",
    )(x, pre_w.reshape(1, d), w_main_bf, w_small_bf)


def _dot(a, b, precision=None):
    return jnp.dot(a, b, preferred_element_type=F32, precision=precision)


def _dot_nt(a, b, precision=None):
    return lax.dot_general(a, b, (((1,), (1,)), ((), ())), preferred_element_type=F32, precision=precision)


def _dot_tn(a, b, precision=None):
    return lax.dot_general(a, b, (((0,), (0,)), ((), ())), preferred_element_type=F32, precision=precision)


def _iota2(shape, dim):
    return lax.broadcasted_iota(jnp.int32, shape, dim)


def _split_bf16(x):
    hi = x.astype(BF16)
    return hi, (x - hi.astype(F32)).astype(BF16)


def _dot3(a, b):
    ah, al = _split_bf16(a)
    bh, bl = _split_bf16(b)
    return _dot(ah, bh) + (_dot(ah, bl) + _dot(al, bh))


def _tri_inverse_all(lmats, c):
    ri = _iota2((c, c), 0)
    ci = _iota2((c, c), 1)
    eye = jnp.where(ri == ci, 1.0, 0.0).astype(F32)
    nb = min(16, c)
    sh = int(math.log2(nb))
    same = (ri >> sh) == (ci >> sh)
    dot1 = lambda a, b: _dot(a.astype(BF16), b.astype(BF16))
    ps = [jnp.where(same, lm, 0.0) for lm in lmats]
    ts = [eye - p for p in ps]
    k = 2
    while k < nb:
        ps = [dot1(p, p) for p in ps]
        ts = [t + dot1(t, p) for t, p in zip(ts, ps)]
        k *= 2
    blk = nb
    while blk < c:
        s1 = int(math.log2(blk))
        offm = ((ri >> (s1 + 1)) == (ci >> (s1 + 1))) & ((ri >> s1) != (ci >> s1))
        tl = [dot1(t, jnp.where(offm, lm, 0.0)) for t, lm in zip(ts, lmats)]
        ts = [t - dot1(x, t) for t, x in zip(ts, tl)]
        blk *= 2
    return ts


def _gdn_kernel(qkv_ref, za_ref, zs_ref, cs_ref, s0_ref, cw_ref, alog_ref, dtb_ref, nw_ref,
                o_ref, sout_ref, cout_ref, xbuf, s_scr, *, c, nc, bb, heads, dk, dv):
    n = pl.program_id(1)
    nlast = pl.num_programs(1) - 1
    nqk = heads * dk
    r = nc * c

    @pl.when(n == 0)
    def _():
        xbuf[:, 5:8, :] = cs_ref[...]
        s_scr[...] = s0_ref[...]

    xbuf[:, 8:8 + r, :] = qkv_ref[...]

    @pl.when(n == nlast)
    def _():
        cout_ref[...] = xbuf[:, 5 + r:8 + r, :]

    ri = _iota2((c, c), 0)
    ci = _iota2((c, c), 1)
    incl = ri >= ci
    strict = ri > ci
    eye = jnp.where(ri == ci, 1.0, 0.0).astype(F32)
    rr = _iota2((r, r), 0)
    rc = _iota2((r, r), 1)
    sh = int(math.log2(c))
    blocktril = jnp.where((rr >= rc) & ((rr >> sh) == (rc >> sh)), 1.0, 0.0).astype(F32)
    nw = nw_ref[...]

    inst = []
    for bi in range(bb):
        y = xbuf[bi, 5:5 + r, :] * cw_ref[0:1, :]
        for j in range(1, 4):
            y = y + xbuf[bi, 5 + j:5 + j + r, :] * cw_ref[j:j + 1, :]
        y = _silu(y)
        zs = zs_ref[bi]
        beta_all = _sigmoid(zs)
        g_all = -jnp.exp(alog_ref[...]) * _softplus(zs + dtb_ref[...])
        gcum_all = _dot3(blocktril, g_all)
        gcum_t = gcum_all.T
        for i in range(nc):
            rows = slice(i * c, (i + 1) * c)
            for h in range(heads):
                inst.append(dict(
                    bi=bi, i=i, h=h,
                    q=y[rows, h * dk:(h + 1) * dk],
                    k=y[rows, nqk + h * dk:nqk + (h + 1) * dk],
                    v=y[rows, 2 * nqk + h * dv:2 * nqk + (h + 1) * dv],
                    beta=beta_all[rows, h:h + 1],
                    gc=gcum_all[rows, heads + h:heads + h + 1],
                    grow=gcum_t[heads + h:heads + h + 1, rows]))
    xbuf[:, 5:8, :] = xbuf[:, 5 + r:8 + r, :]

    for d in inst:
        q, k = d["q"], d["k"]
        d["q"] = q * lax.rsqrt(jnp.sum(q * q, axis=-1, keepdims=True) + EPS) * (dk ** -0.5)
        d["k"] = k * lax.rsqrt(jnp.sum(k * k, axis=-1, keepdims=True) + EPS)
        d["k16"] = d["k"].astype(BF16)
        d["kb"] = d["k"] * d["beta"]
        d["decay"] = jnp.where(incl, jnp.exp(jnp.minimum(d["gc"] - d["grow"], 0.0)), 0.0)
    for d in inst:
        d["lmat"] = jnp.where(strict, _dot_nt(d["kb"].astype(BF16), d["k16"]) * d["decay"], 0.0)
    tinvs = _tri_inverse_all([d["lmat"] for d in inst], c)
    for d, tinv in zip(inst, tinvs):
        eg = jnp.exp(d["gc"])
        rhs = jnp.concatenate([d["v"] * d["beta"], d["kb"] * eg], axis=-1)
        sol = rhs + _dot((tinv - eye).astype(BF16), rhs.astype(BF16))
        d["u"] = sol[:, :dv]
        d["w16"] = sol[:, dv:].astype(BF16)
        d["attn16"] = (_dot_nt(d["q"].astype(BF16), d["k16"]) * d["decay"]).astype(BF16)
        d["qe16"] = (d["q"] * eg).astype(BF16)
        gl = d["gc"][c - 1:c, :]
        d["kdec16"] = (d["k"] * jnp.exp(gl - d["gc"])).astype(BF16)
        d["egl"] = jnp.exp(gl)

    state = {(bi, h): s_scr[bi, h] for bi in range(bb) for h in range(heads)}
    for i in range(nc):
        for d in inst:
            if d["i"] != i:
                continue
            bi, h = d["bi"], d["h"]
            s = state[(bi, h)]
            s16 = s.astype(BF16)
            v_new = d["u"] - _dot(d["w16"], s16)
            vn16 = v_new.astype(BF16)
            o = _dot(d["qe16"], s16) + _dot(d["attn16"], vn16)
            state[(bi, h)] = s * d["egl"] + _dot_tn(d["kdec16"], vn16)
            za = za_ref[bi, i * c:(i + 1) * c, h * dv:(h + 1) * dv]
            o_ref[bi, i * c:(i + 1) * c, h * dv:(h + 1) * dv] = (_rms(o, nw) * _silu(za)).astype(o_ref.dtype)
    for (bi, h), s in state.items():
        s_scr[bi, h] = s

    @pl.when(n == nlast)
    def _():
        sout_ref[...] = s_scr[...]


def gdn(z3, zs3, conv_state, state, conv_w, a_log, dt_bias, norm_w, *, state_layer, c, nc, bb, qkv_blk, za_blk,
        layer, depth, prev, out_dtype):
    b, t, _ = z3.shape
    _, _, heads, dk, dv = state.shape
    cch = conv_state.shape[-1]
    sc = zs3.shape[-1]
    r = nc * c
    pad = jnp.zeros((1, sc), F32)
    alog_p = lax.dynamic_update_slice(pad, a_log.reshape(1, heads).astype(F32), (0, heads))
    dtb_p = lax.dynamic_update_slice(pad, dt_bias.reshape(1, heads).astype(F32), (0, heads))
    kern = functools.partial(_gdn_kernel, c=c, nc=nc, bb=bb, heads=heads, dk=dk, dv=dv)
    prev_ops, prev_specs, aliases = _alias_plan(prev, 9, (1,))
    return pl.pallas_call(
        _drop_refs(kern, 9, len(prev_ops)),
        grid=(b // bb, t // r),
        in_specs=[
            pl.BlockSpec((bb, r, cch), lambda i, n: (i, n, qkv_blk)),
            pl.BlockSpec((bb, r, heads * dv), lambda i, n: (i, n, za_blk)),
            pl.BlockSpec((bb, r, sc), lambda i, n: (i, n, 0)),
            pl.BlockSpec((None, bb, 3, cch), lambda i, n: (state_layer, i, 0, 0)),
            pl.BlockSpec((None, bb, heads, dk, dv), lambda i, n: (state_layer, i, 0, 0, 0)),
            pl.BlockSpec((4, cch), lambda i, n: (0, 0)),
            pl.BlockSpec((1, sc), lambda i, n: (0, 0)),
            pl.BlockSpec((1, sc), lambda i, n: (0, 0)),
            pl.BlockSpec((1, dv), lambda i, n: (0, 0)),
        ] + prev_specs,
        out_specs=[
            pl.BlockSpec((bb, r, heads * dv), lambda i, n: (i, n, 0)),
            pl.BlockSpec((None, bb, heads, dk, dv), lambda i, n: (layer, i, 0, 0, 0)),
            pl.BlockSpec((bb, 3, cch), lambda i, n: (i, 0, 0)),
        ],
        out_shape=[
            jax.ShapeDtypeStruct((b, t, heads * dv), out_dtype),
            jax.ShapeDtypeStruct((depth, b, heads, dk, dv), F32),
            jax.ShapeDtypeStruct((b, 3, cch), F32),
        ],
        scratch_shapes=[pltpu.VMEM((bb, 8 + r, cch), F32), pltpu.VMEM((bb, heads, dk, dv), F32)],
        input_output_aliases=aliases,
        compiler_params=_cparams(("parallel", "arbitrary")),
        name="gdn",
    )(z3, z3, zs3, conv_state, state, conv_w, alog_p, dtb_p, norm_w.reshape(1, dv), *prev_ops)


SB = 8


def _dot_cumsum(tril, x):
    hi = x.astype(BF16)
    r1 = x - hi.astype(F32)
    mid = r1.astype(BF16)
    lo = (r1 - mid.astype(F32)).astype(BF16)
    t16 = tril.astype(BF16)
    return _dot(t16, hi) + (_dot(t16, mid) + _dot(t16, lo))


def _hgrn_kernel(q_ref, f_ref, i_ref, g_ref, lbraw_ref, s0_ref, nw_ref, o_ref, sout_ref,
                 st_scr, *, c, nc, bb, heads, dk, dv, layer):
    n = pl.program_id(1)
    nlast = pl.num_programs(1) - 1
    bh = [(bi, h) for bi in range(bb) for h in range(heads)]

    @pl.when(n == 0)
    def _():
        for bi, h in bh:
            st_scr[bi * heads + h] = s0_ref[bi, h].T

    raw = lbraw_ref[...]
    e = jnp.exp(raw - jnp.max(raw, axis=0, keepdims=True))
    sm = e / jnp.sum(e, axis=0, keepdims=True)
    lb = jnp.zeros((1, heads * dk), F32)
    for l in range(1, layer + 1):
        lb = lb + sm[l:l + 1, :]

    r = bb * nc * c
    flat = lambda ref: ref[...].reshape(r, ref.shape[-1])
    zf = flat(f_ref)
    logsig = jnp.minimum(zf, 0.0) - jnp.log1p(jnp.exp(-jnp.abs(zf)))
    la = jnp.log(lb)
    lbb = jnp.log1p(-lb) + logsig
    logf = jnp.maximum(la, lbb) + jnp.log1p(jnp.exp(-jnp.abs(la - lbb)))
    kh = (1.0 - lb) * _sigmoid(-zf)
    qh = _silu(flat(q_ref)) * (dk ** -0.5)
    vv = flat(i_ref)
    gate = _sigmoid(flat(g_ref))

    ri = _iota2((r, r), 0)
    ci = _iota2((r, r), 1)
    sh = int(math.log2(c))
    tril = jnp.where((ri >= ci) & ((ri >> sh) == (ci >> sh)), 1.0, 0.0).astype(F32)
    bcum = _dot_cumsum(tril, logf)
    fgate = jnp.exp(logf)
    nsb = c // SB
    sub = _iota2((nsb, SB, dk), 1)
    nw = nw_ref[...]
    g3 = lambda x: x.reshape(nsb, SB, x.shape[-1])
    states = {(bi, h): st_scr[bi * heads + h] for bi, h in bh}

    for ic, bi, h in [(ic, bi, h) for ic in range(nc) for bi, h in bh]:
        rows = slice((bi * nc + ic) * c, (bi * nc + ic + 1) * c)
        ks = slice(h * dk, (h + 1) * dk)
        vs = slice(h * dv, (h + 1) * dv)
        q = qh[rows, ks]
        k = kh[rows, ks]
        b = bcum[rows, ks]
        v = vv[rows, vs]
        st = states[(bi, h)]
        o = _dot_nt((q * jnp.exp(b)).astype(BF16), st.astype(BF16))
        q3, k3, v3, f3 = g3(q), g3(k), g3(v), g3(fgate[rows, ks])
        o3 = jnp.sum(q3 * k3, axis=-1, keepdims=True) * v3
        e = None
        for d in range(1, SB):
            fd = f3 if d == 1 else pltpu.roll(f3, d - 1, 1)
            e = fd if e is None else e * fd
            m = jnp.where(sub >= d, q3 * pltpu.roll(k3, d, 1) * e, 0.0)
            o3 = o3 + jnp.sum(m, axis=-1, keepdims=True) * pltpu.roll(v3, d, 1)
        o = o + o3.reshape(c, dv)
        if nsb > 1:
            b3 = g3(b)
            rend = jnp.broadcast_to(b3[:, SB - 1:SB, :], b3.shape).reshape(c, dk)
            khat = k * jnp.exp(jnp.minimum(rend - b, 0.0))
            zq = jnp.zeros((c, dk), F32)
            qs, kk = [], []
            for jb in range(nsb - 1):
                lo, hi = jb * SB, (jb + 1) * SB
                rj = b[hi - 1:hi, :]
                qpart = q[hi:, :] * jnp.exp(jnp.minimum(b[hi:, :] - rj, 0.0))
                qs.append(jnp.concatenate([zq[:hi], qpart], axis=0))
                kparts = [khat[lo:hi]] if lo == 0 else [zq[:lo], khat[lo:hi]]
                kk.append(jnp.concatenate(kparts + [zq[hi:]], axis=0))
            amat = _dot_nt(jnp.concatenate(qs, axis=-1).astype(BF16), jnp.concatenate(kk, axis=-1).astype(BF16))
            o = o + _dot(amat.astype(BF16), v.astype(BF16))
        bl = b[c - 1:c, :]
        kdec = k * jnp.exp(bl - b)
        states[(bi, h)] = st * jnp.exp(bl) + _dot_tn(v.astype(BF16), kdec.astype(BF16))
        o_ref[bi, ic * c:(ic + 1) * c, vs] = (_rms(o, nw) * gate[rows, vs]).astype(o_ref.dtype)
    for bi, h in bh:
        st_scr[bi * heads + h] = states[(bi, h)]

    @pl.when(n == nlast)
    def _():
        for bi, h in bh:
            sout_ref[bi, h] = st_scr[bi * heads + h].T


def hgrn(z3, lb_raw, state, norm_w, *, state_layer, c, nc, bb, layer, q_blk, prev, out_dtype):
    b, t, _ = z3.shape
    _, _, heads, dk, dv = state.shape
    depth = lb_raw.shape[0]
    wk = heads * dk
    assert c % SB == 0
    kern = functools.partial(_hgrn_kernel, c=c, nc=nc, bb=bb, heads=heads, dk=dk, dv=dv, layer=layer)
    r = nc * c
    zspec = lambda off: pl.BlockSpec((bb, r, wk), lambda i, n: (i, n, q_blk + off))
    prev_ops, prev_specs, aliases = _alias_plan(prev, 7, (1,))
    return pl.pallas_call(
        _drop_refs(kern, 7, len(prev_ops)),
        grid=(b // bb, t // r),
        in_specs=[
            zspec(0), zspec(1), zspec(2), zspec(3),
            pl.BlockSpec((depth, wk), lambda i, n: (0, 0)),
            pl.BlockSpec((None, bb, heads, dk, dv), lambda i, n: (state_layer, i, 0, 0, 0)),
            pl.BlockSpec((1, dv), lambda i, n: (0, 0)),
        ] + prev_specs,
        out_specs=[
            pl.BlockSpec((bb, r, heads * dv), lambda i, n: (i, n, 0)),
            pl.BlockSpec((None, bb, heads, dk, dv), lambda i, n: (layer, i, 0, 0, 0)),
        ],
        out_shape=[
            jax.ShapeDtypeStruct((b, t, heads * dv), out_dtype),
            jax.ShapeDtypeStruct((depth, b, heads, dk, dv), F32),
        ],
        scratch_shapes=[pltpu.VMEM((bb * heads, dv, dk), F32)],
        input_output_aliases=aliases,
        compiler_params=_cparams(("parallel", "arbitrary")),
        name="hgrn",
    )(z3, z3, z3, z3, lb_raw, state, norm_w.reshape(1, dv), *prev_ops)


def rope_tables(t, pos0, dh, theta):
    rd = dh // 4
    half = rd // 2
    inv = jnp.power(jnp.float32(theta), -jnp.arange(half, dtype=F32) / half)
    ang = (pos0 + jnp.arange(t)).astype(F32)[:, None] * inv
    cos, sin = jnp.cos(ang), jnp.sin(ang)
    d = jnp.arange(128) % dh
    f = d % half
    cc = jnp.where(d < rd, cos[:, f], 1.0)
    s1 = jnp.where(d < half, -sin[:, f], 0.0)
    s2 = jnp.where((d >= half) & (d < rd), sin[:, f], 0.0)
    return cc.astype(F32), s1.astype(F32), s2.astype(F32)


def _prep_kernel(q_ref, k_ref, v_ref, c_ref, s1_ref, s2_ref, q16_ref, k32_ref, k16_ref, v32_ref, v16_ref,
                 *, scale, half, heads, v_transposed):
    tm, w = q_ref.shape[1:]
    hw = w // heads
    reps = w // c_ref.shape[-1]
    cc = jnp.concatenate([c_ref[...]] * reps, axis=-1)
    s1 = jnp.concatenate([s1_ref[...]] * reps, axis=-1)
    s2 = jnp.concatenate([s2_ref[...]] * reps, axis=-1)

    def rope(x):
        return x * cc + pltpu.roll(x, w - half, 1) * s1 + pltpu.roll(x, half, 1) * s2

    q16_ref[0] = (rope(q_ref[0]) * scale).astype(BF16)
    k = rope(k_ref[0])
    k16_ref[0] = k.astype(BF16)
    v = v_ref[0]
    v16_ref[0] = (v.T if v_transposed else v).astype(BF16)
    for h in range(heads):
        k32_ref[0, pl.ds(h, tm, stride=heads), :] = k[:, h * hw:(h + 1) * hw]
        v32_ref[0, pl.ds(h, tm, stride=heads), :] = v[:, h * hw:(h + 1) * hw]


def _drop_refs(kern, start, count):
    def wrapped(*refs):
        return kern(*refs[:start], *refs[start + count:])
    return wrapped


def _alias_plan(prev, n_in, out_idx):
    prev = [] if prev is None else list(prev)
    specs = [pl.BlockSpec(memory_space=pl.ANY)] * len(prev)
    return prev, specs, {n_in + k: out_idx[k] for k in range(len(prev))}


def qkv_prep(z3, tables, *, q_blk, dh, heads, tm, v_transposed, layer, depth, prev):
    b, t, _ = z3.shape
    w = tables[0].shape[-1] * (heads * 2 * dh // tables[0].shape[-1])
    hw = w // heads
    cc, s1, s2 = tables
    kern = functools.partial(_prep_kernel, scale=dh ** -0.5 * math.log2(math.e), half=dh // 8, heads=heads,
                             v_transposed=v_transposed)
    zspec = lambda off: pl.BlockSpec((1, tm, w), lambda i, n: (i, n, q_blk + off))
    tspec = pl.BlockSpec((tm, cc.shape[-1]), lambda i, n: (n, 0))
    ospec = pl.BlockSpec((1, tm, w), lambda i, n: (i, n, 0))
    cspec = pl.BlockSpec((None, 1, tm * heads, hw), lambda i, n: (layer, i, n, 0))
    sd = lambda dt: jax.ShapeDtypeStruct((b, t, w), dt)
    sc = jax.ShapeDtypeStruct((depth, b, t * heads, hw), F32)
    vspec = pl.BlockSpec((1, w, tm), lambda i, n: (i, 0, n)) if v_transposed else ospec
    vsd = jax.ShapeDtypeStruct((b, w, t), BF16) if v_transposed else sd(BF16)
    prev_ops, prev_specs, aliases = _alias_plan(prev, 6, (1, 3))
    return pl.pallas_call(
        _drop_refs(kern, 6, len(prev_ops)),
        grid=(b, t // tm),
        in_specs=[zspec(0), zspec(1), zspec(2), tspec, tspec, tspec] + prev_specs,
        out_specs=[ospec, cspec, ospec, cspec, vspec],
        out_shape=[sd(BF16), sc, sd(BF16), sc, vsd],
        input_output_aliases=aliases,
        compiler_params=_cparams(("parallel", "parallel")),
        name="qkv_prep",
    )(z3, z3, z3, cc, s1, s2, *prev_ops)


def _lambda(lam_ref, lam_init):
    lm = lam_ref[...]
    a = jnp.sum(lm[0:1, :] * lm[1:2, :], axis=-1, keepdims=True)
    b = jnp.sum(lm[2:3, :] * lm[3:4, :], axis=-1, keepdims=True)
    return jnp.exp(a) - jnp.exp(b) + lam_init


ONES_ROWS = 16


def _attn_prompt_kernel(qt_ref, kt_ref, q_ref, k_ref, vt_ref, lam_ref, nw_ref, o_ref,
                        qs_scr, m_scr, acc_scr, *, tq, dh, dv, hpb, qchunk, lam_init):
    p = pl.program_id(2)
    qi = qt_ref[p]
    ki = kt_ref[p]

    @pl.when(ki == 0)
    def _():
        for hh in range(hpb):
            q = q_ref[0, :, hh * dv:(hh + 1) * dv]
            lane = _iota2(q.shape, 1)
            zero = jnp.zeros_like(q)
            qs_scr[hh, 0:tq, :] = jnp.where(lane < dh, q, zero)
            qs_scr[hh, tq:2 * tq, :] = jnp.where(lane >= dh, q, zero)
        m_scr[...] = jnp.full(m_scr.shape, -jnp.inf, F32)
        acc_scr[...] = jnp.zeros_like(acc_scr)

    def update(diag):
        for hh in range(hpb):
            k = k_ref[0, :, hh * dv:(hh + 1) * dv]
            vt = vt_ref[0, hh * dv:(hh + 1) * dv, :]
            vt1 = jnp.concatenate([vt, jnp.ones((ONES_ROWS, vt.shape[1]), BF16)], axis=0)
            for c0 in range(0, 2 * tq, qchunk):
                cs = slice(c0, c0 + qchunk)
                st = _dot_nt(k, qs_scr[hh, cs, :])
                if diag:
                    key = _iota2(st.shape, 0)
                    qry = (_iota2(st.shape, 1) + c0) & (tq - 1)
                    st = jnp.where(key <= qry, st, -jnp.inf)
                m_prev = m_scr[hh, :, cs]
                m_new = jnp.maximum(m_prev, jnp.max(st, axis=0, keepdims=True))
                alpha = jnp.exp2(m_prev - m_new)
                pr = jnp.exp2(st - m_new).astype(BF16)
                acc_scr[hh, :, cs] = alpha * acc_scr[hh, :, cs] + _dot(vt1, pr)
                m_scr[hh, :, cs] = m_new

    @pl.when(ki < qi)
    def _():
        update(False)

    @pl.when(ki == qi)
    def _():
        update(True)
        lam = _lambda(lam_ref, lam_init)
        for hh in range(hpb):
            o1 = acc_scr[hh, 0:dv, 0:tq] / acc_scr[hh, dv:dv + 1, 0:tq]
            o2 = acc_scr[hh, 0:dv, tq:2 * tq] / acc_scr[hh, dv:dv + 1, tq:2 * tq]
            o = (o1 - lam * o2).T
            o_ref[0, :, hh * dv:(hh + 1) * dv] = (_rms(o, nw_ref[...]) * (1.0 - lam_init)).astype(o_ref.dtype)


def attn_prompt(q16, k16, vt16, lam, norm_w, *, tq, dh, lam_init):
    b, t, w = q16.shape
    dv = norm_w.shape[-1]
    heads = w // dv
    nq = t // tq
    pairs = [(i, j) for i in range(nq) for j in range(i + 1)]
    qt = jnp.asarray([pq for pq, _ in pairs], jnp.int32)
    kt = jnp.asarray([pk for _, pk in pairs], jnp.int32)
    hpb = math.gcd(heads, ATTN_HEADS_PER_STEP)
    kern = functools.partial(_attn_prompt_kernel, tq=tq, dh=dh, dv=dv, hpb=hpb, qchunk=min(ATTN_QCHUNK, 2 * tq),
                             lam_init=lam_init)
    grid_spec = pltpu.PrefetchScalarGridSpec(
        num_scalar_prefetch=2,
        grid=(b, heads // hpb, len(pairs)),
        in_specs=[
            pl.BlockSpec((1, tq, hpb * dv), lambda i, h, p, qt, kt: (i, qt[p], h)),
            pl.BlockSpec((1, tq, hpb * dv), lambda i, h, p, qt, kt: (i, kt[p], h)),
            pl.BlockSpec((1, hpb * dv, tq), lambda i, h, p, qt, kt: (i, h, kt[p])),
            pl.BlockSpec(lam.shape, lambda i, h, p, qt, kt: (0, 0)),
            pl.BlockSpec((1, dv), lambda i, h, p, qt, kt: (0, 0)),
        ],
        out_specs=pl.BlockSpec((1, tq, hpb * dv), lambda i, h, p, qt, kt: (i, qt[p], h)),
        scratch_shapes=[
            pltpu.VMEM((hpb, 2 * tq, dv), BF16),
            pltpu.VMEM((hpb, 1, 2 * tq), F32),
            pltpu.VMEM((hpb, dv + ONES_ROWS, 2 * tq), F32),
        ],
    )
    return pl.pallas_call(
        kern,
        grid_spec=grid_spec,
        out_shape=jax.ShapeDtypeStruct((b, t, w), BF16),
        compiler_params=_cparams(("parallel", "parallel", "arbitrary")),
        name="attn_prompt",
    )(qt, kt, q16, k16, vt16, lam, norm_w.reshape(1, dv))


def _attn_sample_kernel(pt_ref, q_ref, kc_ref, vc_ref, lam_ref, nw_ref, *rest,
                        npages, tq, heads, dh, dv, page, lam_init):
    k_refs = rest[:npages]
    v_refs = rest[npages:2 * npages]
    o_ref = rest[2 * npages]
    lam = _lambda(lam_ref, lam_init)
    nw = nw_ref[...]
    q = q_ref[0].astype(F32)
    kc = kc_ref[0]
    vc = vc_ref[0]
    lane = _iota2((tq, dv), 1)
    qidx = _iota2((2 * tq, tq), 0) % tq
    kidx = _iota2((2 * tq, tq), 1)
    hsl = [slice(h * dv, (h + 1) * dv) for h in range(heads)]

    def head_rows(refs, h):
        return jnp.concatenate([r[0, 0, pl.ds(h, page, stride=heads), :].astype(BF16) for r in refs], axis=0)

    qrows = [jnp.concatenate([jnp.where(lane < dh, q[:, hs], 0.0), jnp.where(lane >= dh, q[:, hs], 0.0)],
                             axis=0).astype(BF16) for hs in hsl]
    s_past = [_dot_nt(qrows[h], head_rows(k_refs, h)) for h in range(heads)]
    s_cur = [jnp.where(kidx <= qidx, _dot_nt(qrows[h], kc[:, hsl[h]]), -jnp.inf) for h in range(heads)]
    ms = [jnp.maximum(jnp.max(sp, axis=-1, keepdims=True), jnp.max(sc, axis=-1, keepdims=True))
          for sp, sc in zip(s_past, s_cur)]
    p_past = [jnp.exp2(sp - m) for sp, m in zip(s_past, ms)]
    p_cur = [jnp.exp2(sc - m) for sc, m in zip(s_cur, ms)]
    invs = [1.0 / (jnp.sum(pp, axis=-1, keepdims=True) + jnp.sum(pc, axis=-1, keepdims=True))
            for pp, pc in zip(p_past, p_cur)]

    def diff(pp, inv):
        pn = pp * inv
        return pn[0:tq] - lam * pn[tq:2 * tq]

    for h in range(heads):
        o = (_dot(diff(p_past[h], invs[h]).astype(BF16), head_rows(v_refs, h))
             + _dot(diff(p_cur[h], invs[h]), vc[:, hsl[h]].astype(F32)))
        o_ref[0, :, hsl[h]] = _rms(o, nw) * (1.0 - lam_init)


def attn_sample(q16, k16, v16, cache_k, cache_v, page_table, lam, norm_w, *, layer, heads, dh, lam_init):
    b, tq, w = q16.shape
    dv = norm_w.shape[-1]
    npages = page_table.shape[1]
    prow = cache_k.shape[2]
    kern = functools.partial(_attn_sample_kernel, npages=npages, tq=tq, heads=heads, dh=dh, dv=dv,
                             page=prow // heads, lam_init=lam_init)
    cur = pl.BlockSpec((1, tq, w), lambda i, pt: (i, 0, 0))

    def page_spec(j):
        return pl.BlockSpec((1, 1, prow, dv), lambda i, pt: (layer, pt[i, j], 0, 0))

    grid_spec = pltpu.PrefetchScalarGridSpec(
        num_scalar_prefetch=1,
        grid=(b,),
        in_specs=[cur, cur, cur,
                  pl.BlockSpec(lam.shape, lambda i, pt: (0, 0)),
                  pl.BlockSpec((1, dv), lambda i, pt: (0, 0))]
        + [page_spec(j) for j in range(npages)] * 2,
        out_specs=pl.BlockSpec((1, tq, w), lambda i, pt: (i, 0, 0)),
    )
    return pl.pallas_call(
        kern,
        grid_spec=grid_spec,
        out_shape=jax.ShapeDtypeStruct((b, tq, w), F32),
        compiler_params=_cparams(("parallel",)),
        name="attn_sample",
    )(page_table, q16, k16, v16, lam, norm_w.reshape(1, dv), *([cache_k] * npages), *([cache_v] * npages))


def _merge_ffn_kernel(x_ref, oa_ref, ob_ref, oc_ref, g0_ref, g1_ref, g2_ref, wb_ref, wo_ref, mpost_ref,
                      pre_ref, wi_ref, wo2_ref, post_ref, o_ref, a_scr, *, tf):
    y = None
    for i, (o_i, g_i) in enumerate(((oa_ref, g0_ref), (ob_ref, g1_ref), (oc_ref, g2_ref))):
        ys = _dot(o_i[...].astype(BF16), wb_ref[i])
        t = _sigmoid(g_i[...]) * ys
        y = t if y is None else y + t
    y2 = _dot(y.astype(BF16), wo_ref[...])
    x1 = x_ref[...] + _rms(y2, mpost_ref[...])
    o_ref[...] = _ffn_body(x1, pre_ref, wi_ref, wo2_ref, post_ref, a_scr, tf)


def merge_ffn(x, o_a, o_b, o_c, z, w_branch_bf, w_out_bf, mpost_w, pre_w, w_in_bf, w_out2_bf, post_w, *,
              layer, tm, tf, gate_blk):
    n, d = x.shape
    bw = o_a.shape[-1]
    nb = w_branch_bf.shape[1]
    dff = w_out2_bf.shape[1]
    resident = dict(pipeline_mode=pl.Buffered(1))
    ospec = pl.BlockSpec((tm, bw), lambda i: (i, 0))
    gspec = lambda k: pl.BlockSpec((tm, d), lambda i: (i, gate_blk + k))
    vec = pl.BlockSpec((1, d), lambda i: (0, 0))
    return pl.pallas_call(
        functools.partial(_merge_ffn_kernel, tf=tf),
        grid=(n // tm,),
        in_specs=[
            pl.BlockSpec((tm, d), lambda i: (i, 0)),
            ospec, ospec, ospec, gspec(0), gspec(1), gspec(2),
            pl.BlockSpec((None, nb, bw, d), lambda i: (layer, 0, 0, 0), **resident),
            pl.BlockSpec((None, d, d), lambda i: (layer, 0, 0), **resident),
            vec, vec,
            pl.BlockSpec((None, d, 2 * dff), lambda i: (layer, 0, 0), **resident),
            pl.BlockSpec((None, dff, d), lambda i: (layer, 0, 0), **resident),
            vec,
        ],
        out_specs=pl.BlockSpec((tm, d), lambda i: (i, 0)),
        out_shape=jax.ShapeDtypeStruct((n, d), F32),
        scratch_shapes=[pltpu.VMEM((tm, dff), BF16)],
        compiler_params=_cparams(("parallel",)),
        name="merge_ffn",
    )(x, o_a, o_b, o_c, z, z, z, w_branch_bf, w_out_bf, mpost_w.reshape(1, d), pre_w.reshape(1, d), w_in_bf, w_out2_bf,
      post_w.reshape(1, d))


ROPE_THETA = 500000.0
SCAN_CHUNK = 64
GDN_CHUNKS_PER_STEP = 4
HGRN_CHUNKS_PER_STEP = 4
HGRN_INSTANCES = 32
GDN_INSTANCES = 16
ROW_TILE = 1024
FFN_ROW_TILE = 512
FF_TILE = 256
PROJ_COL_TILES = 4
ATTN_TILE = 512
ATTN_QCHUNK = 1024
ATTN_HEADS_PER_STEP = 4
PREP_TILE = 512
SMALL_COLS = 128


def _split_w_in(w_in_l, sizes):
    offs = [0]
    for s in sizes:
        offs.append(offs[-1] + s)
    seg = lambda i: w_in_l[:, offs[i]:offs[i + 1]]
    main = jnp.concatenate([seg(11), seg(0), seg(1), seg(4), seg(5), seg(6), seg(7), seg(8), seg(9), seg(10)], axis=1)
    small = jnp.concatenate([seg(2), seg(3)], axis=1)
    small = jnp.pad(small, ((0, 0), (0, SMALL_COLS - small.shape[1])))
    return main.astype(BF16), small.astype(BF16)


def _row_tile(n, pref):
    return pref if n % pref == 0 else n


def kernel(x_prompt, x_sample, state_gdn, state_gdn_conv, state_hgrn, cache_k, cache_v, page_table,
           ffn1_norm_pre, ffn1_norm_post, ffn1_w_in, ffn1_w_out, mix_norm_pre, mix_norm_post, w_in,
           gdn_conv_w, gdn_a_log, gdn_dt_bias, gdn_norm_w, hgrn_lb_raw, hgrn_norm_w, diff_lambda,
           diff_norm_w, w_branch, w_out, ffn2_norm_pre, ffn2_norm_post, ffn2_w_in, ffn2_w_out):
    depth = w_in.shape[0]
    bp, tp, d = x_prompt.shape
    bs, ts, _ = x_sample.shape
    _, _, gh, gdk, gdv = state_gdn.shape
    cch = state_gdn_conv.shape[-1]
    _, _, hh, hdk, hdv = state_hgrn.shape
    _, n_pool, page, ah, adh2 = cache_k.shape
    adh = adh2 // 2
    adv = cache_v.shape[-1]
    aw = ah * adv
    assert ah * adh2 == aw and hh * hdk == aw and hh * hdv == aw and gh * gdv == aw and cch == 3 * aw and d == 2 * aw
    sizes = (cch, gh * gdv, gh, gh, hh * hdk, hh * hdk, hh * hdv, hh * hdv, ah * adh2, ah * adh2, aw, 3 * d)
    gate_blk, qkv_blk, za_blk, hq_blk, aq_blk = 0, (3 * d) // cch, (3 * d + cch) // aw, (3 * d + cch) // aw + 1, (3 * d + cch) // aw + 5
    past_len = page_table.shape[1] * page
    ck = cache_k.reshape(depth, n_pool, page * ah, adh2)
    cv = cache_v.reshape(depth, n_pool, page * ah, adv)
    tabs_p = rope_tables(tp, 0, adh, ROPE_THETA)
    tabs_s = tuple(jnp.tile(a, (bs, 1)) for a in rope_tables(ts, past_len, adh, ROPE_THETA))
    zeros_conv = jnp.zeros((1, bp, 3, cch), F32)
    zeros_gdn = jnp.zeros((1, bp, gh, gdk, gdv), F32)
    zeros_hgrn = jnp.zeros((1, bp, hh, hdk, hdv), F32)

    f1_in, f1_out = ffn1_w_in.astype(BF16), ffn1_w_out.astype(BF16)
    f2_in, f2_out = ffn2_w_in.astype(BF16), ffn2_w_out.astype(BF16)
    wb, wo = w_branch.astype(BF16), w_out.astype(BF16)

    def run_layer(l, x, b, t, conv_state, gdn_state, hgrn_state, sl, tabs, prompt, w_main, w_small, acc):
        n = b * t
        tm = _row_tile(n, ROW_TILE)
        lam_init = 0.8 - 0.6 * math.exp(-0.3 * l)
        tmf = _row_tile(n, FFN_ROW_TILE)
        x = ffn(x, ffn1_norm_pre[l], f1_in, f1_out, ffn1_norm_post[l], layer=l, tm=tmf, tf=FF_TILE)
        z, zs = proj(x, mix_norm_pre[l], w_main, w_small, tm=tm, tn=w_main.shape[1] // PROJ_COL_TILES)
        z3 = z.reshape(b, t, z.shape[-1])
        zs3 = zs.reshape(b, t, SMALL_COLS)
        c = math.gcd(SCAN_CHUNK, t)
        branch_dtype = BF16 if c % 16 == 0 else F32
        o_a, new_gdn, new_conv = gdn(z3, zs3, conv_state, gdn_state, gdn_conv_w[l], gdn_a_log[l], gdn_dt_bias[l],
                                     gdn_norm_w[l], state_layer=sl, c=c, nc=min(GDN_CHUNKS_PER_STEP, t // c),
                                     bb=math.gcd(b, max(1, GDN_INSTANCES // (gh * min(GDN_CHUNKS_PER_STEP, t // c)))),
                                     qkv_blk=qkv_blk, za_blk=za_blk, layer=l, depth=depth,
                                     prev=None if acc is None else acc[0:1], out_dtype=branch_dtype)
        o_b, new_hgrn = hgrn(z3, hgrn_lb_raw, hgrn_state, hgrn_norm_w[l], state_layer=sl, c=c,
                             nc=min(HGRN_CHUNKS_PER_STEP, t // c),
                             bb=math.gcd(b, max(1, HGRN_INSTANCES // (hh * min(HGRN_CHUNKS_PER_STEP, t // c)))),
                             layer=l, q_blk=hq_blk,
                             prev=None if acc is None else acc[1:2], out_dtype=branch_dtype)
        if prompt:
            q16, k32, k16, v32, v16 = qkv_prep(z3, tabs, q_blk=aq_blk, dh=adh, heads=ah, tm=_row_tile(t, PREP_TILE),
                                               v_transposed=True, layer=l, depth=depth,
                                               prev=None if acc is None else acc[2:4])
        else:
            q16, k32, k16, v32, v16 = qkv_prep(z.reshape(1, n, z.shape[-1]), tabs, q_blk=aq_blk, dh=adh, heads=ah,
                                               tm=_row_tile(n, PREP_TILE), v_transposed=False, layer=l, depth=depth,
                                               prev=None if acc is None else acc[2:4])
            q16, k16, v16 = (a.reshape(b, t, aw) for a in (q16, k16, v16))
        if prompt:
            o_c = attn_prompt(q16, k16, v16, diff_lambda[l], diff_norm_w[l], tq=_row_tile(t, ATTN_TILE), dh=adh,
                              lam_init=lam_init)
        else:
            o_c = attn_sample(q16, k16, v16, ck, cv, page_table, diff_lambda[l], diff_norm_w[l], layer=l, heads=ah, dh=adh,
                              lam_init=lam_init)
        x = merge_ffn(x, o_a.reshape(n, aw), o_b.reshape(n, aw), o_c.reshape(n, aw), z, wb, wo, mix_norm_post[l],
                      ffn2_norm_pre[l], f2_in, f2_out, ffn2_norm_post[l], layer=l, tm=tmf, tf=FF_TILE, gate_blk=gate_blk)
        return x, new_conv, (new_gdn, new_hgrn, k32, v32)

    xp = x_prompt.reshape(bp * tp, d)
    xs = x_sample.reshape(bs * ts, d)
    conv_p, conv_s, acc_p, acc_s = [], [], None, None
    for l in range(depth):
        w_main, w_small = _split_w_in(w_in[l], sizes)
        xp, cp, acc_p = run_layer(l, xp, bp, tp, zeros_conv, zeros_gdn, zeros_hgrn, 0, tabs_p, True, w_main, w_small, acc_p)
        conv_p.append(cp)
        xs, cs, acc_s = run_layer(l, xs, bs, ts, state_gdn_conv, state_gdn, state_hgrn, l, tabs_s, False, w_main, w_small,
                                  acc_s)
        conv_s.append(cs)

    def finish(x, b, t, convs, acc):
        new_gdn, new_hgrn, k32, v32 = acc
        return (x.reshape(b, t, d), new_gdn, jnp.stack(convs), new_hgrn,
                k32.reshape(depth, b, t, ah, adh2), v32.reshape(depth, b, t, ah, adv))

    yp, p_gdn, p_conv, p_hgrn, p_k, p_v = finish(xp, bp, tp, conv_p, acc_p)
    ys, s_gdn, s_conv, s_hgrn, s_k, s_v = finish(xs, bs, ts, conv_s, acc_s)
    return (yp, ys, p_gdn, p_conv, p_hgrn, p_k, p_v, s_gdn, s_conv, s_hgrn, s_k, s_v)
```

```python
import functools
import math

import jax
import jax.numpy as jnp
from jax import lax
from jax.experimental import pallas as pl
from jax.experimental.pallas import tpu as pltpu

F32 = jnp.float32
BF16 = jnp.bfloat16
EPS = 1e-6

VMEM_LIMIT_BYTES = 56 * 1024 * 1024


def _cparams(sem):
    return pltpu.CompilerParams(dimension_semantics=sem, vmem_limit_bytes=VMEM_LIMIT_BYTES)


def _rms(x, w):
    return x * lax.rsqrt(jnp.mean(x * x, axis=-1, keepdims=True) + EPS) * w


def _sigmoid(x):
    return 1.0 / (1.0 + jnp.exp(-x))


def _silu(x):
    return x * _sigmoid(x)


def _softplus(x):
    return jnp.maximum(x, 0.0) + jnp.log1p(jnp.exp(-jnp.abs(x)))


def _ffn_body(x, pre_ref, wi_ref, wo_ref, post_ref, a_scr, tf):
    dff = wo_ref.shape[0]
    h = _rms(x, pre_ref[...]).astype(BF16)
    for j in range(dff // tf):
        cols = slice(j * tf, (j + 1) * tf)
        g = _dot(h, wi_ref[:, cols])
        u = _dot(h, wi_ref[:, dff + j * tf:dff + (j + 1) * tf])
        a_scr[:, cols] = (_silu(g) * u).astype(BF16)
    f = _dot(a_scr[...], wo_ref[...])
    return x + 0.5 * _rms(f, post_ref[...])


def _ffn_kernel(x_ref, pre_ref, wi_ref, wo_ref, post_ref, o_ref, a_scr, *, tf):
    o_ref[...] = _ffn_body(x_ref[...], pre_ref, wi_ref, wo_ref, post_ref, a_scr, tf)


def ffn(x, pre_w, w_in_bf, w_out_bf, post_w, *, layer, tm, tf):
    n, d = x.shape
    dff = w_out_bf.shape[1]
    resident = dict(pipeline_mode=pl.Buffered(1))
    return pl.pallas_call(
        functools.partial(_ffn_kernel, tf=tf),
        grid=(n // tm,),
        in_specs=[
            pl.BlockSpec((tm, d), lambda i: (i, 0)),
            pl.BlockSpec((1, d), lambda i: (0, 0)),
            pl.BlockSpec((None, d, 2 * dff), lambda i: (layer, 0, 0), **resident),
            pl.BlockSpec((None, dff, d), lambda i: (layer, 0, 0), **resident),
            pl.BlockSpec((1, d), lambda i: (0, 0)),
        ],
        out_specs=pl.BlockSpec((tm, d), lambda i: (i, 0)),
        out_shape=jax.ShapeDtypeStruct((n, d), F32),
        scratch_shapes=[pltpu.VMEM((tm, dff), BF16)],
        compiler_params=_cparams(("parallel",)),
        name="ffn",
    )(x, pre_w.reshape(1, d), w_in_bf, w_out_bf, post_w.reshape(1, d))


def _proj_kernel(x_ref, pre_ref, w_ref, ws_ref, z_ref, zs_ref, h_scr):
    j = pl.program_id(1)

    @pl.when(j == 0)
    def _():
        h = _rms(x_ref[...], pre_ref[...]).astype(BF16)
        h_scr[...] = h
        zs_ref[...] = jnp.dot(h, ws_ref[...], preferred_element_type=F32)

    z_ref[...] = jnp.dot(h_scr[...], w_ref[...], preferred_element_type=F32)


def proj(x, pre_w, w_main_bf, w_small_bf, *, tm, tn):
    n, d = x.shape
    cols = w_main_bf.shape[1]
    sc = w_small_bf.shape[1]
    return pl.pallas_call(
        _proj_kernel,
        grid=(n // tm, cols // tn),
        in_specs=[
            pl.BlockSpec((tm, d), lambda i, j: (i, 0)),
            pl.BlockSpec((1, d), lambda i, j: (0, 0)),
            pl.BlockSpec((d, tn), lambda i, j: (0, j)),
            pl.BlockSpec((d, sc), lambda i, j: (0, 0)),
        ],
        out_specs=[
            pl.BlockSpec((tm, tn), lambda i, j: (i, j)),
            pl.BlockSpec((tm, sc), lambda i, j: (i, 0)),
        ],
        out_shape=[jax.ShapeDtypeStruct((n, cols), F32), jax.ShapeDtypeStruct((n, sc), F32)],
        scratch_shapes=[pltpu.VMEM((tm, d), BF16)],
        compiler_params=_cparams(("parallel", "arbitrary")),
        name="proj",
    )(x, pre_w.reshape(1, d), w_main_bf, w_small_bf)


def _dot(a, b):
    return jnp.dot(a, b, preferred_element_type=F32)


def _dot_nt(a, b):
    return lax.dot_general(a, b, (((1,), (1,)), ((), ())), preferred_element_type=F32)


def _dot_tn(a, b):
    return lax.dot_general(a, b, (((0,), (0,)), ((), ())), preferred_element_type=F32)


def _iota2(shape, dim):
    return lax.broadcasted_iota(jnp.int32, shape, dim)


def _dot_cumsum(tril, x):
    hi = x.astype(BF16)
    r1 = x - hi.astype(F32)
    mid = r1.astype(BF16)
    lo = (r1 - mid.astype(F32)).astype(BF16)
    t16 = tril.astype(BF16)
    return _dot(t16, hi) + (_dot(t16, mid) + _dot(t16, lo))


def _tri_inverse_all(lmats, c):
    ri = _iota2((c, c), 0)
    ci = _iota2((c, c), 1)
    eye = jnp.where(ri == ci, 1.0, 0.0).astype(F32)
    nb = min(16, c)
    sh = int(math.log2(nb))
    same = (ri >> sh) == (ci >> sh)
    dot1 = lambda a, b: _dot(a.astype(BF16), b.astype(BF16))
    ps = [jnp.where(same, lm, 0.0) for lm in lmats]
    ts = [eye - p for p in ps]
    k = 2
    while k < nb:
        ps = [dot1(p, p) for p in ps]
        ts = [t + dot1(t, p) for t, p in zip(ts, ps)]
        k *= 2
    blk = nb
    while blk < c:
        s1 = int(math.log2(blk))
        offm = ((ri >> (s1 + 1)) == (ci >> (s1 + 1))) & ((ri >> s1) != (ci >> s1))
        tl = [dot1(t, jnp.where(offm, lm, 0.0)) for t, lm in zip(ts, lmats)]
        ts = [t - dot1(x, t) for t, x in zip(ts, tl)]
        blk *= 2
    return ts


CONV_TAPS = 4
CONV_TOP = 8
CONV_HIST = CONV_TOP - (CONV_TAPS - 1)


def _gdn_kernel(qkv_ref, za_ref, zs_ref, cs_ref, s0_ref, cw_ref, alog_ref, dtb_ref, nw_ref,
                o_ref, sout_ref, cout_ref, xbuf, s_scr, *, c, nc, bb, heads, dk, dv):
    n = pl.program_id(1)
    nlast = pl.num_programs(1) - 1
    nqk = heads * dk
    r = nc * c

    @pl.when(n == 0)
    def _():
        xbuf[:, CONV_HIST:CONV_TOP, :] = cs_ref[...]
        s_scr[...] = s0_ref[...]

    xbuf[:, CONV_TOP:CONV_TOP + r, :] = qkv_ref[...]

    @pl.when(n == nlast)
    def _():
        cout_ref[...] = xbuf[:, CONV_HIST + r:CONV_TOP + r, :]

    ri = _iota2((c, c), 0)
    ci = _iota2((c, c), 1)
    incl = ri >= ci
    strict = ri > ci
    eye = jnp.where(ri == ci, 1.0, 0.0).astype(F32)
    rr = _iota2((r, r), 0)
    rc = _iota2((r, r), 1)
    sh = int(math.log2(c))
    blocktril = jnp.where((rr >= rc) & ((rr >> sh) == (rc >> sh)), 1.0, 0.0).astype(F32)
    nw = nw_ref[...]

    inst = []
    for bi in range(bb):
        y = xbuf[bi, CONV_HIST:CONV_HIST + r, :] * cw_ref[0:1, :]
        for j in range(1, CONV_TAPS):
            y = y + xbuf[bi, CONV_HIST + j:CONV_HIST + j + r, :] * cw_ref[j:j + 1, :]
        y = _silu(y)
        zs = zs_ref[bi]
        beta_all = _sigmoid(zs)
        g_all = -jnp.exp(alog_ref[...]) * _softplus(zs + dtb_ref[...])
        gcum_all = _dot_cumsum(blocktril, g_all)
        gcum_t = gcum_all.T
        for i in range(nc):
            rows = slice(i * c, (i + 1) * c)
            for h in range(heads):
                inst.append(dict(
                    bi=bi, i=i, h=h,
                    q=y[rows, h * dk:(h + 1) * dk],
                    k=y[rows, nqk + h * dk:nqk + (h + 1) * dk],
                    v=y[rows, 2 * nqk + h * dv:2 * nqk + (h + 1) * dv],
                    beta=beta_all[rows, h:h + 1],
                    gc=gcum_all[rows, heads + h:heads + h + 1],
                    grow=gcum_t[heads + h:heads + h + 1, rows]))
    xbuf[:, CONV_HIST:CONV_TOP, :] = xbuf[:, CONV_HIST + r:CONV_TOP + r, :]

    for d in inst:
        q, k = d["q"], d["k"]
        d["q"] = q * lax.rsqrt(jnp.sum(q * q, axis=-1, keepdims=True) + EPS) * (dk ** -0.5)
        d["k"] = k * lax.rsqrt(jnp.sum(k * k, axis=-1, keepdims=True) + EPS)
        d["k16"] = d["k"].astype(BF16)
        d["kb"] = d["k"] * d["beta"]
        d["decay"] = jnp.where(incl, jnp.exp(jnp.minimum(d["gc"] - d["grow"], 0.0)), 0.0)
    for d in inst:
        d["lmat"] = jnp.where(strict, _dot_nt(d["kb"].astype(BF16), d["k16"]) * d["decay"], 0.0)
    tinvs = _tri_inverse_all([d["lmat"] for d in inst], c)
    for d, tinv in zip(inst, tinvs):
        eg = jnp.exp(d["gc"])
        rhs = jnp.concatenate([d["v"] * d["beta"], d["kb"] * eg], axis=-1)
        sol = rhs + _dot((tinv - eye).astype(BF16), rhs.astype(BF16))
        d["u"] = sol[:, :dv]
        d["w16"] = sol[:, dv:].astype(BF16)
        d["attn16"] = (_dot_nt(d["q"].astype(BF16), d["k16"]) * d["decay"]).astype(BF16)
        d["qe16"] = (d["q"] * eg).astype(BF16)
        gl = d["gc"][c - 1:c, :]
        d["kdec16"] = (d["k"] * jnp.exp(gl - d["gc"])).astype(BF16)
        d["egl"] = jnp.exp(gl)

    state = {(bi, h): s_scr[bi, h] for bi in range(bb) for h in range(heads)}
    for i in range(nc):
        for d in inst:
            if d["i"] != i:
                continue
            bi, h = d["bi"], d["h"]
            s = state[(bi, h)]
            s16 = s.astype(BF16)
            v_new = d["u"] - _dot(d["w16"], s16)
            vn16 = v_new.astype(BF16)
            o = _dot(d["qe16"], s16) + _dot(d["attn16"], vn16)
            state[(bi, h)] = s * d["egl"] + _dot_tn(d["kdec16"], vn16)
            za = za_ref[bi, i * c:(i + 1) * c, h * dv:(h + 1) * dv]
            o_ref[bi, i * c:(i + 1) * c, h * dv:(h + 1) * dv] = (_rms(o, nw) * _silu(za)).astype(o_ref.dtype)
    for (bi, h), s in state.items():
        s_scr[bi, h] = s

    @pl.when(n == nlast)
    def _():
        sout_ref[...] = s_scr[...]


def gdn(z3, zs3, conv_state, state, conv_w, a_log, dt_bias, norm_w, *, state_layer, c, nc, bb, qkv_blk, za_blk,
        layer, depth, prev, out_dtype):
    b, t, _ = z3.shape
    _, _, heads, dk, dv = state.shape
    cch = conv_state.shape[-1]
    sc = zs3.shape[-1]
    r = nc * c
    pad = jnp.zeros((1, sc), F32)
    alog_p = lax.dynamic_update_slice(pad, a_log.reshape(1, heads).astype(F32), (0, heads))
    dtb_p = lax.dynamic_update_slice(pad, dt_bias.reshape(1, heads).astype(F32), (0, heads))
    kern = functools.partial(_gdn_kernel, c=c, nc=nc, bb=bb, heads=heads, dk=dk, dv=dv)
    prev_ops, prev_specs, aliases = _alias_plan(prev, 9, (1,))
    return pl.pallas_call(
        _drop_refs(kern, 9, len(prev_ops)),
        grid=(b // bb, t // r),
        in_specs=[
            pl.BlockSpec((bb, r, cch), lambda i, n: (i, n, qkv_blk)),
            pl.BlockSpec((bb, r, heads * dv), lambda i, n: (i, n, za_blk)),
            pl.BlockSpec((bb, r, sc), lambda i, n: (i, n, 0)),
            pl.BlockSpec((None, bb, CONV_TAPS - 1, cch), lambda i, n: (state_layer, i, 0, 0)),
            pl.BlockSpec((None, bb, heads, dk, dv), lambda i, n: (state_layer, i, 0, 0, 0)),
            pl.BlockSpec((CONV_TAPS, cch), lambda i, n: (0, 0)),
            pl.BlockSpec((1, sc), lambda i, n: (0, 0)),
            pl.BlockSpec((1, sc), lambda i, n: (0, 0)),
            pl.BlockSpec((1, dv), lambda i, n: (0, 0)),
        ] + prev_specs,
        out_specs=[
            pl.BlockSpec((bb, r, heads * dv), lambda i, n: (i, n, 0)),
            pl.BlockSpec((None, bb, heads, dk, dv), lambda i, n: (layer, i, 0, 0, 0)),
            pl.BlockSpec((bb, CONV_TAPS - 1, cch), lambda i, n: (i, 0, 0)),
        ],
        out_shape=[
            jax.ShapeDtypeStruct((b, t, heads * dv), out_dtype),
            jax.ShapeDtypeStruct((depth, b, heads, dk, dv), F32),
            jax.ShapeDtypeStruct((b, 3, cch), F32),
        ],
        scratch_shapes=[pltpu.VMEM((bb, CONV_TOP + r, cch), F32), pltpu.VMEM((bb, heads, dk, dv), F32)],
        input_output_aliases=aliases,
        compiler_params=_cparams(("parallel", "arbitrary")),
        name="gdn",
    )(z3, z3, zs3, conv_state, state, conv_w, alog_p, dtb_p, norm_w.reshape(1, dv), *prev_ops)


SB = 8


def _hgrn_kernel(q_ref, f_ref, i_ref, g_ref, lbraw_ref, s0_ref, nw_ref, o_ref, sout_ref,
                 st_scr, *, c, nc, bb, heads, dk, dv, layer):
    n = pl.program_id(1)
    nlast = pl.num_programs(1) - 1
    bh = [(bi, h) for bi in range(bb) for h in range(heads)]

    @pl.when(n == 0)
    def _():
        for bi, h in bh:
            st_scr[bi * heads + h] = s0_ref[bi, h].T

    raw = lbraw_ref[...]
    e = jnp.exp(raw - jnp.max(raw, axis=0, keepdims=True))
    sm = e / jnp.sum(e, axis=0, keepdims=True)
    lb = jnp.zeros((1, heads * dk), F32)
    for l in range(1, layer + 1):
        lb = lb + sm[l:l + 1, :]

    r = bb * nc * c
    flat = lambda ref: ref[...].reshape(r, ref.shape[-1])
    zf = flat(f_ref)
    logsig = jnp.minimum(zf, 0.0) - jnp.log1p(jnp.exp(-jnp.abs(zf)))
    la = jnp.log(lb)
    lbb = jnp.log1p(-lb) + logsig
    logf = jnp.maximum(la, lbb) + jnp.log1p(jnp.exp(-jnp.abs(la - lbb)))
    kh = (1.0 - lb) * _sigmoid(-zf)
    qh = _silu(flat(q_ref)) * (dk ** -0.5)
    vv = flat(i_ref)
    gate = _sigmoid(flat(g_ref))

    ri = _iota2((r, r), 0)
    ci = _iota2((r, r), 1)
    sh = int(math.log2(c))
    tril = jnp.where((ri >= ci) & ((ri >> sh) == (ci >> sh)), 1.0, 0.0).astype(F32)
    bcum = _dot_cumsum(tril, logf)
    fgate = jnp.exp(logf)
    nsb = c // SB
    sub = _iota2((nsb, SB, dk), 1)
    nw = nw_ref[...]
    g3 = lambda x: x.reshape(nsb, SB, x.shape[-1])
    states = {(bi, h): st_scr[bi * heads + h] for bi, h in bh}

    for ic, bi, h in [(ic, bi, h) for ic in range(nc) for bi, h in bh]:
        rows = slice((bi * nc + ic) * c, (bi * nc + ic + 1) * c)
        ks = slice(h * dk, (h + 1) * dk)
        vs = slice(h * dv, (h + 1) * dv)
        q = qh[rows, ks]
        k = kh[rows, ks]
        b = bcum[rows, ks]
        v = vv[rows, vs]
        st = states[(bi, h)]
        o = _dot_nt((q * jnp.exp(b)).astype(BF16), st.astype(BF16))
        q3, k3, v3, f3 = g3(q), g3(k), g3(v), g3(fgate[rows, ks])
        o3 = jnp.sum(q3 * k3, axis=-1, keepdims=True) * v3
        e = None
        for d in range(1, SB):
            fd = f3 if d == 1 else pltpu.roll(f3, d - 1, 1)
            e = fd if e is None else e * fd
            m = jnp.where(sub >= d, q3 * pltpu.roll(k3, d, 1) * e, 0.0)
            o3 = o3 + jnp.sum(m, axis=-1, keepdims=True) * pltpu.roll(v3, d, 1)
        o = o + o3.reshape(c, dv)
        if nsb > 1:
            b3 = g3(b)
            rend = jnp.broadcast_to(b3[:, SB - 1:SB, :], b3.shape).reshape(c, dk)
            khat = k * jnp.exp(jnp.minimum(rend - b, 0.0))
            zq = jnp.zeros((c, dk), F32)
            qs, kk = [], []
            for jb in range(nsb - 1):
                lo, hi = jb * SB, (jb + 1) * SB
                rj = b[hi - 1:hi, :]
                qpart = q[hi:, :] * jnp.exp(jnp.minimum(b[hi:, :] - rj, 0.0))
                qs.append(jnp.concatenate([zq[:hi], qpart], axis=0))
                kparts = [khat[lo:hi]] if lo == 0 else [zq[:lo], khat[lo:hi]]
                kk.append(jnp.concatenate(kparts + [zq[hi:]], axis=0))
            amat = _dot_nt(jnp.concatenate(qs, axis=-1).astype(BF16), jnp.concatenate(kk, axis=-1).astype(BF16))
            o = o + _dot(amat.astype(BF16), v.astype(BF16))
        bl = b[c - 1:c, :]
        kdec = k * jnp.exp(bl - b)
        states[(bi, h)] = st * jnp.exp(bl) + _dot_tn(v.astype(BF16), kdec.astype(BF16))
        o_ref[bi, ic * c:(ic + 1) * c, vs] = (_rms(o, nw) * gate[rows, vs]).astype(o_ref.dtype)
    for bi, h in bh:
        st_scr[bi * heads + h] = states[(bi, h)]

    @pl.when(n == nlast)
    def _():
        for bi, h in bh:
            sout_ref[bi, h] = st_scr[bi * heads + h].T


def hgrn(z3, lb_raw, state, norm_w, *, state_layer, c, nc, bb, layer, q_blk, prev, out_dtype):
    b, t, _ = z3.shape
    _, _, heads, dk, dv = state.shape
    depth = lb_raw.shape[0]
    wk = heads * dk
    assert c % SB == 0
    kern = functools.partial(_hgrn_kernel, c=c, nc=nc, bb=bb, heads=heads, dk=dk, dv=dv, layer=layer)
    r = nc * c
    zspec = lambda off: pl.BlockSpec((bb, r, wk), lambda i, n: (i, n, q_blk + off))
    prev_ops, prev_specs, aliases = _alias_plan(prev, 7, (1,))
    return pl.pallas_call(
        _drop_refs(kern, 7, len(prev_ops)),
        grid=(b // bb, t // r),
        in_specs=[
            zspec(0), zspec(1), zspec(2), zspec(3),
            pl.BlockSpec((depth, wk), lambda i, n: (0, 0)),
            pl.BlockSpec((None, bb, heads, dk, dv), lambda i, n: (state_layer, i, 0, 0, 0)),
            pl.BlockSpec((1, dv), lambda i, n: (0, 0)),
        ] + prev_specs,
        out_specs=[
            pl.BlockSpec((bb, r, heads * dv), lambda i, n: (i, n, 0)),
            pl.BlockSpec((None, bb, heads, dk, dv), lambda i, n: (layer, i, 0, 0, 0)),
        ],
        out_shape=[
            jax.ShapeDtypeStruct((b, t, heads * dv), out_dtype),
            jax.ShapeDtypeStruct((depth, b, heads, dk, dv), F32),
        ],
        scratch_shapes=[pltpu.VMEM((bb * heads, dv, dk), F32)],
        input_output_aliases=aliases,
        compiler_params=_cparams(("parallel", "arbitrary")),
        name="hgrn",
    )(z3, z3, z3, z3, lb_raw, state, norm_w.reshape(1, dv), *prev_ops)


def rope_tables(t, pos0, dh, theta):
    rd = dh // 4
    half = rd // 2
    inv = jnp.power(jnp.float32(theta), -jnp.arange(half, dtype=F32) / half)
    ang = (pos0 + jnp.arange(t)).astype(F32)[:, None] * inv
    cos, sin = jnp.cos(ang), jnp.sin(ang)
    d = jnp.arange(128) % dh
    f = d % half
    cc = jnp.where(d < rd, cos[:, f], 1.0)
    s1 = jnp.where(d < half, -sin[:, f], 0.0)
    s2 = jnp.where((d >= half) & (d < rd), sin[:, f], 0.0)
    return cc.astype(F32), s1.astype(F32), s2.astype(F32)


def _prep_kernel(q_ref, k_ref, v_ref, c_ref, s1_ref, s2_ref, q16_ref, k32_ref, k16_ref, v32_ref, v16_ref,
                 *, scale, half, heads, v_transposed):
    tm, w = q_ref.shape[1:]
    hw = w // heads
    reps = w // c_ref.shape[-1]
    cc = jnp.concatenate([c_ref[...]] * reps, axis=-1)
    s1 = jnp.concatenate([s1_ref[...]] * reps, axis=-1)
    s2 = jnp.concatenate([s2_ref[...]] * reps, axis=-1)

    def rope(x):
        return x * cc + pltpu.roll(x, w - half, 1) * s1 + pltpu.roll(x, half, 1) * s2

    q16_ref[0] = (rope(q_ref[0]) * scale).astype(BF16)
    k = rope(k_ref[0])
    k16_ref[0] = k.astype(BF16)
    v = v_ref[0]
    v16_ref[0] = (v.T if v_transposed else v).astype(BF16)
    for h in range(heads):
        k32_ref[0, pl.ds(h, tm, stride=heads), :] = k[:, h * hw:(h + 1) * hw]
        v32_ref[0, pl.ds(h, tm, stride=heads), :] = v[:, h * hw:(h + 1) * hw]


def _drop_refs(kern, start, count):
    def wrapped(*refs):
        return kern(*refs[:start], *refs[start + count:])
    return wrapped


def _alias_plan(prev, n_in, out_idx):
    prev = [] if prev is None else list(prev)
    specs = [pl.BlockSpec(memory_space=pl.ANY)] * len(prev)
    return prev, specs, {n_in + k: out_idx[k] for k in range(len(prev))}


def qkv_prep(z3, tables, *, q_blk, dh, heads, tm, v_transposed, layer, depth, prev):
    b, t, _ = z3.shape
    w = tables[0].shape[-1] * (heads * 2 * dh // tables[0].shape[-1])
    hw = w // heads
    cc, s1, s2 = tables
    kern = functools.partial(_prep_kernel, scale=dh ** -0.5 * math.log2(math.e), half=dh // 8, heads=heads,
                             v_transposed=v_transposed)
    zspec = lambda off: pl.BlockSpec((1, tm, w), lambda i, n: (i, n, q_blk + off))
    tspec = pl.BlockSpec((tm, cc.shape[-1]), lambda i, n: (n, 0))
    ospec = pl.BlockSpec((1, tm, w), lambda i, n: (i, n, 0))
    cspec = pl.BlockSpec((None, 1, tm * heads, hw), lambda i, n: (layer, i, n, 0))
    sd = lambda dt: jax.ShapeDtypeStruct((b, t, w), dt)
    sc = jax.ShapeDtypeStruct((depth, b, t * heads, hw), F32)
    vspec = pl.BlockSpec((1, w, tm), lambda i, n: (i, 0, n)) if v_transposed else ospec
    vsd = jax.ShapeDtypeStruct((b, w, t), BF16) if v_transposed else sd(BF16)
    prev_ops, prev_specs, aliases = _alias_plan(prev, 6, (1, 3))
    return pl.pallas_call(
        _drop_refs(kern, 6, len(prev_ops)),
        grid=(b, t // tm),
        in_specs=[zspec(0), zspec(1), zspec(2), tspec, tspec, tspec] + prev_specs,
        out_specs=[ospec, cspec, ospec, cspec, vspec],
        out_shape=[sd(BF16), sc, sd(BF16), sc, vsd],
        input_output_aliases=aliases,
        compiler_params=_cparams(("parallel", "parallel")),
        name="qkv_prep",
    )(z3, z3, z3, cc, s1, s2, *prev_ops)


def _lambda(lam_ref, lam_init):
    lm = lam_ref[...]
    a = jnp.sum(lm[0:1, :] * lm[1:2, :], axis=-1, keepdims=True)
    b = jnp.sum(lm[2:3, :] * lm[3:4, :], axis=-1, keepdims=True)
    return jnp.exp(a) - jnp.exp(b) + lam_init


ONES_ROWS = 16


def _attn_prompt_kernel(qt_ref, kt_ref, q_ref, k_ref, vt_ref, lam_ref, nw_ref, o_ref,
                        qs_scr, m_scr, acc_scr, *, tq, dh, dv, hpb, qchunk, lam_init):
    p = pl.program_id(2)
    qi = qt_ref[p]
    ki = kt_ref[p]

    @pl.when(ki == 0)
    def _():
        for hh in range(hpb):
            q = q_ref[0, :, hh * dv:(hh + 1) * dv]
            lane = _iota2(q.shape, 1)
            zero = jnp.zeros_like(q)
            qs_scr[hh, 0:tq, :] = jnp.where(lane < dh, q, zero)
            qs_scr[hh, tq:2 * tq, :] = jnp.where(lane >= dh, q, zero)
        m_scr[...] = jnp.full(m_scr.shape, -jnp.inf, F32)
        acc_scr[...] = jnp.zeros_like(acc_scr)

    def update(diag):
        for hh in range(hpb):
            k = k_ref[0, :, hh * dv:(hh + 1) * dv]
            vt = vt_ref[0, hh * dv:(hh + 1) * dv, :]
            vt1 = jnp.concatenate([vt, jnp.ones((ONES_ROWS, vt.shape[1]), BF16)], axis=0)
            for c0 in range(0, 2 * tq, qchunk):
                cs = slice(c0, c0 + qchunk)
                st = _dot_nt(k, qs_scr[hh, cs, :])
                if diag:
                    key = _iota2(st.shape, 0)
                    qry = (_iota2(st.shape, 1) + c0) & (tq - 1)
                    st = jnp.where(key <= qry, st, -jnp.inf)
                m_prev = m_scr[hh, :, cs]
                m_new = jnp.maximum(m_prev, jnp.max(st, axis=0, keepdims=True))
                alpha = jnp.exp2(m_prev - m_new)
                pr = jnp.exp2(st - m_new).astype(BF16)
                acc_scr[hh, :, cs] = alpha * acc_scr[hh, :, cs] + _dot(vt1, pr)
                m_scr[hh, :, cs] = m_new

    @pl.when(ki < qi)
    def _():
        update(False)

    @pl.when(ki == qi)
    def _():
        update(True)
        lam = _lambda(lam_ref, lam_init)
        for hh in range(hpb):
            o1 = acc_scr[hh, 0:dv, 0:tq] / acc_scr[hh, dv:dv + 1, 0:tq]
            o2 = acc_scr[hh, 0:dv, tq:2 * tq] / acc_scr[hh, dv:dv + 1, tq:2 * tq]
            o = (o1 - lam * o2).T
            o_ref[0, :, hh * dv:(hh + 1) * dv] = (_rms(o, nw_ref[...]) * (1.0 - lam_init)).astype(o_ref.dtype)


def attn_prompt(q16, k16, vt16, lam, norm_w, *, tq, dh, lam_init):
    b, t, w = q16.shape
    dv = norm_w.shape[-1]
    heads = w // dv
    nq = t // tq
    pairs = [(i, j) for i in range(nq) for j in range(i + 1)]
    qt = jnp.asarray([pq for pq, _ in pairs], jnp.int32)
    kt = jnp.asarray([pk for _, pk in pairs], jnp.int32)
    hpb = math.gcd(heads, ATTN_HEADS_PER_STEP)
    kern = functools.partial(_attn_prompt_kernel, tq=tq, dh=dh, dv=dv, hpb=hpb, qchunk=min(ATTN_QCHUNK, 2 * tq),
                             lam_init=lam_init)
    grid_spec = pltpu.PrefetchScalarGridSpec(
        num_scalar_prefetch=2,
        grid=(b, heads // hpb, len(pairs)),
        in_specs=[
            pl.BlockSpec((1, tq, hpb * dv), lambda i, h, p, qt, kt: (i, qt[p], h)),
            pl.BlockSpec((1, tq, hpb * dv), lambda i, h, p, qt, kt: (i, kt[p], h)),
            pl.BlockSpec((1, hpb * dv, tq), lambda i, h, p, qt, kt: (i, h, kt[p])),
            pl.BlockSpec(lam.shape, lambda i, h, p, qt, kt: (0, 0)),
            pl.BlockSpec((1, dv), lambda i, h, p, qt, kt: (0, 0)),
        ],
        out_specs=pl.BlockSpec((1, tq, hpb * dv), lambda i, h, p, qt, kt: (i, qt[p], h)),
        scratch_shapes=[
            pltpu.VMEM((hpb, 2 * tq, dv), BF16),
            pltpu.VMEM((hpb, 1, 2 * tq), F32),
            pltpu.VMEM((hpb, dv + ONES_ROWS, 2 * tq), F32),
        ],
    )
    return pl.pallas_call(
        kern,
        grid_spec=grid_spec,
        out_shape=jax.ShapeDtypeStruct((b, t, w), BF16),
        compiler_params=_cparams(("parallel", "parallel", "arbitrary")),
        name="attn_prompt",
    )(qt, kt, q16, k16, vt16, lam, norm_w.reshape(1, dv))


def _attn_sample_kernel(pt_ref, q_ref, kc_ref, vc_ref, lam_ref, nw_ref, *rest,
                        npages, tq, heads, dh, dv, page, lam_init):
    k_refs = rest[:npages]
    v_refs = rest[npages:2 * npages]
    o_ref = rest[2 * npages]
    lam = _lambda(lam_ref, lam_init)
    nw = nw_ref[...]
    q = q_ref[0].astype(F32)
    kc = kc_ref[0]
    vc = vc_ref[0]
    lane = _iota2((tq, dv), 1)
    qidx = _iota2((2 * tq, tq), 0) % tq
    kidx = _iota2((2 * tq, tq), 1)
    hsl = [slice(h * dv, (h + 1) * dv) for h in range(heads)]

    def head_rows(refs, h):
        return jnp.concatenate([r[0, 0, pl.ds(h, page, stride=heads), :].astype(BF16) for r in refs], axis=0)

    qrows = [jnp.concatenate([jnp.where(lane < dh, q[:, hs], 0.0), jnp.where(lane >= dh, q[:, hs], 0.0)],
                             axis=0).astype(BF16) for hs in hsl]
    s_past = [_dot_nt(qrows[h], head_rows(k_refs, h)) for h in range(heads)]
    s_cur = [jnp.where(kidx <= qidx, _dot_nt(qrows[h], kc[:, hsl[h]]), -jnp.inf) for h in range(heads)]
    ms = [jnp.maximum(jnp.max(sp, axis=-1, keepdims=True), jnp.max(sc, axis=-1, keepdims=True))
          for sp, sc in zip(s_past, s_cur)]
    p_past = [jnp.exp2(sp - m) for sp, m in zip(s_past, ms)]
    p_cur = [jnp.exp2(sc - m) for sc, m in zip(s_cur, ms)]
    invs = [1.0 / (jnp.sum(pp, axis=-1, keepdims=True) + jnp.sum(pc, axis=-1, keepdims=True))
            for pp, pc in zip(p_past, p_cur)]

    def diff(pp, inv):
        pn = pp * inv
        return pn[0:tq] - lam * pn[tq:2 * tq]

    for h in range(heads):
        o = (_dot(diff(p_past[h], invs[h]).astype(BF16), head_rows(v_refs, h))
             + _dot(diff(p_cur[h], invs[h]), vc[:, hsl[h]].astype(F32)))
        o_ref[0, :, hsl[h]] = _rms(o, nw) * (1.0 - lam_init)


def attn_sample(q16, k16, v16, cache_k, cache_v, page_table, lam, norm_w, *, layer, heads, dh, lam_init):
    b, tq, w = q16.shape
    dv = norm_w.shape[-1]
    npages = page_table.shape[1]
    prow = cache_k.shape[2]
    kern = functools.partial(_attn_sample_kernel, npages=npages, tq=tq, heads=heads, dh=dh, dv=dv,
                             page=prow // heads, lam_init=lam_init)
    cur = pl.BlockSpec((1, tq, w), lambda i, pt: (i, 0, 0))

    def page_spec(j):
        return pl.BlockSpec((1, 1, prow, dv), lambda i, pt: (layer, pt[i, j], 0, 0))

    grid_spec = pltpu.PrefetchScalarGridSpec(
        num_scalar_prefetch=1,
        grid=(b,),
        in_specs=[cur, cur, cur,
                  pl.BlockSpec(lam.shape, lambda i, pt: (0, 0)),
                  pl.BlockSpec((1, dv), lambda i, pt: (0, 0))]
        + [page_spec(j) for j in range(npages)] * 2,
        out_specs=pl.BlockSpec((1, tq, w), lambda i, pt: (i, 0, 0)),
    )
    return pl.pallas_call(
        kern,
        grid_spec=grid_spec,
        out_shape=jax.ShapeDtypeStruct((b, tq, w), F32),
        compiler_params=_cparams(("parallel",)),
        name="attn_sample",
    )(page_table, q16, k16, v16, lam, norm_w.reshape(1, dv), *([cache_k] * npages), *([cache_v] * npages))


def _merge_ffn_kernel(x_ref, oa_ref, ob_ref, oc_ref, g0_ref, g1_ref, g2_ref, wb_ref, wo_ref, mpost_ref,
                      pre_ref, wi_ref, wo2_ref, post_ref, o_ref, a_scr, *, tf):
    y = None
    for i, (o_i, g_i) in enumerate(((oa_ref, g0_ref), (ob_ref, g1_ref), (oc_ref, g2_ref))):
        ys = _dot(o_i[...].astype(BF16), wb_ref[i])
        t = _sigmoid(g_i[...]) * ys
        y = t if y is None else y + t
    y2 = _dot(y.astype(BF16), wo_ref[...])
    x1 = x_ref[...] + _rms(y2, mpost_ref[...])
    o_ref[...] = _ffn_body(x1, pre_ref, wi_ref, wo2_ref, post_ref, a_scr, tf)


def merge_ffn(x, o_a, o_b, o_c, z, w_branch_bf, w_out_bf, mpost_w, pre_w, w_in_bf, w_out2_bf, post_w, *,
              layer, tm, tf, gate_blk):
    n, d = x.shape
    bw = o_a.shape[-1]
    nb = w_branch_bf.shape[1]
    dff = w_out2_bf.shape[1]
    resident = dict(pipeline_mode=pl.Buffered(1))
    ospec = pl.BlockSpec((tm, bw), lambda i: (i, 0))
    gspec = lambda k: pl.BlockSpec((tm, d), lambda i: (i, gate_blk + k))
    vec = pl.BlockSpec((1, d), lambda i: (0, 0))
    return pl.pallas_call(
        functools.partial(_merge_ffn_kernel, tf=tf),
        grid=(n // tm,),
        in_specs=[
            pl.BlockSpec((tm, d), lambda i: (i, 0)),
            ospec, ospec, ospec, gspec(0), gspec(1), gspec(2),
            pl.BlockSpec((None, nb, bw, d), lambda i: (layer, 0, 0, 0), **resident),
            pl.BlockSpec((None, d, d), lambda i: (layer, 0, 0), **resident),
            vec, vec,
            pl.BlockSpec((None, d, 2 * dff), lambda i: (layer, 0, 0), **resident),
            pl.BlockSpec((None, dff, d), lambda i: (layer, 0, 0), **resident),
            vec,
        ],
        out_specs=pl.BlockSpec((tm, d), lambda i: (i, 0)),
        out_shape=jax.ShapeDtypeStruct((n, d), F32),
        scratch_shapes=[pltpu.VMEM((tm, dff), BF16)],
        compiler_params=_cparams(("parallel",)),
        name="merge_ffn",
    )(x, o_a, o_b, o_c, z, z, z, w_branch_bf, w_out_bf, mpost_w.reshape(1, d), pre_w.reshape(1, d), w_in_bf, w_out2_bf,
      post_w.reshape(1, d))


ROPE_THETA = 500000.0
SCAN_CHUNK = 64
GDN_CHUNKS_PER_STEP = 4
HGRN_CHUNKS_PER_STEP = 4
HGRN_INSTANCES = 32
GDN_INSTANCES = 32
ROW_TILE = 1024
FFN_ROW_TILE = 512
FF_TILE = 256
PROJ_COL_TILES = 4
ATTN_TILE = 512
ATTN_QCHUNK = 1024
ATTN_HEADS_PER_STEP = 4
PREP_TILE = 512
SMALL_COLS = 128


def _split_w_in(w_in_l, sizes):
    offs = [0]
    for s in sizes:
        offs.append(offs[-1] + s)
    seg = lambda i: w_in_l[:, offs[i]:offs[i + 1]]
    main = jnp.concatenate([seg(11), seg(0), seg(1), seg(4), seg(5), seg(6), seg(7), seg(8), seg(9), seg(10)], axis=1)
    small = jnp.concatenate([seg(2), seg(3)], axis=1)
    small = jnp.pad(small, ((0, 0), (0, SMALL_COLS - small.shape[1])))
    return main.astype(BF16), small.astype(BF16)


def _row_tile(n, pref):
    return pref if n % pref == 0 else n


def kernel(x_prompt, x_sample, state_gdn, state_gdn_conv, state_hgrn, cache_k, cache_v, page_table,
           ffn1_norm_pre, ffn1_norm_post, ffn1_w_in, ffn1_w_out, mix_norm_pre, mix_norm_post, w_in,
           gdn_conv_w, gdn_a_log, gdn_dt_bias, gdn_norm_w, hgrn_lb_raw, hgrn_norm_w, diff_lambda,
           diff_norm_w, w_branch, w_out, ffn2_norm_pre, ffn2_norm_post, ffn2_w_in, ffn2_w_out):
    depth = w_in.shape[0]
    bp, tp, d = x_prompt.shape
    bs, ts, _ = x_sample.shape
    _, _, gh, gdk, gdv = state_gdn.shape
    cch = state_gdn_conv.shape[-1]
    _, _, hh, hdk, hdv = state_hgrn.shape
    _, n_pool, page, ah, adh2 = cache_k.shape
    adh = adh2 // 2
    adv = cache_v.shape[-1]
    aw = ah * adv
    assert ah * adh2 == aw and hh * hdk == aw and hh * hdv == aw and gh * gdv == aw and cch == 3 * aw and d == 2 * aw
    sizes = (cch, gh * gdv, gh, gh, hh * hdk, hh * hdk, hh * hdv, hh * hdv, ah * adh2, ah * adh2, aw, 3 * d)
    gate_blk, qkv_blk, za_blk, hq_blk, aq_blk = 0, (3 * d) // cch, (3 * d + cch) // aw, (3 * d + cch) // aw + 1, (3 * d + cch) // aw + 5
    past_len = page_table.shape[1] * page
    ck = cache_k.reshape(depth, n_pool, page * ah, adh2)
    cv = cache_v.reshape(depth, n_pool, page * ah, adv)
    tabs_p = rope_tables(tp, 0, adh, ROPE_THETA)
    tabs_s = tuple(jnp.tile(a, (bs, 1)) for a in rope_tables(ts, past_len, adh, ROPE_THETA))
    assert gdn_conv_w.shape[1] == CONV_TAPS and state_gdn_conv.shape[2] == CONV_TAPS - 1
    zeros_conv = jnp.zeros((1, bp, CONV_TAPS - 1, cch), F32)
    zeros_gdn = jnp.zeros((1, bp, gh, gdk, gdv), F32)
    zeros_hgrn = jnp.zeros((1, bp, hh, hdk, hdv), F32)

    f1_in, f1_out = ffn1_w_in.astype(BF16), ffn1_w_out.astype(BF16)
    f2_in, f2_out = ffn2_w_in.astype(BF16), ffn2_w_out.astype(BF16)
    wb, wo = w_branch.astype(BF16), w_out.astype(BF16)

    def run_layer(l, x, b, t, conv_state, gdn_state, hgrn_state, sl, tabs, prompt, w_main, w_small, acc):
        n = b * t
        tm = _row_tile(n, ROW_TILE)
        lam_init = 0.8 - 0.6 * math.exp(-0.3 * l)
        tmf = _row_tile(n, FFN_ROW_TILE)
        x = ffn(x, ffn1_norm_pre[l], f1_in, f1_out, ffn1_norm_post[l], layer=l, tm=tmf, tf=FF_TILE)
        z, zs = proj(x, mix_norm_pre[l], w_main, w_small, tm=tm, tn=w_main.shape[1] // PROJ_COL_TILES)
        z3 = z.reshape(b, t, z.shape[-1])
        zs3 = zs.reshape(b, t, SMALL_COLS)
        c = math.gcd(SCAN_CHUNK, t)
        branch_dtype = BF16 if c % 16 == 0 else F32
        o_a, new_gdn, new_conv = gdn(z3, zs3, conv_state, gdn_state, gdn_conv_w[l], gdn_a_log[l], gdn_dt_bias[l],
                                     gdn_norm_w[l], state_layer=sl, c=c, nc=min(GDN_CHUNKS_PER_STEP, t // c),
                                     bb=math.gcd(b, max(1, GDN_INSTANCES // (gh * min(GDN_CHUNKS_PER_STEP, t // c)))),
                                     qkv_blk=qkv_blk, za_blk=za_blk, layer=l, depth=depth,
                                     prev=None if acc is None else acc[0:1], out_dtype=branch_dtype)
        o_b, new_hgrn = hgrn(z3, hgrn_lb_raw, hgrn_state, hgrn_norm_w[l], state_layer=sl, c=c,
                             nc=min(HGRN_CHUNKS_PER_STEP, t // c),
                             bb=math.gcd(b, max(1, HGRN_INSTANCES // (hh * min(HGRN_CHUNKS_PER_STEP, t // c)))),
                             layer=l, q_blk=hq_blk,
                             prev=None if acc is None else acc[1:2], out_dtype=branch_dtype)
        if prompt:
            q16, k32, k16, v32, v16 = qkv_prep(z3, tabs, q_blk=aq_blk, dh=adh, heads=ah, tm=_row_tile(t, PREP_TILE),
                                               v_transposed=True, layer=l, depth=depth,
                                               prev=None if acc is None else acc[2:4])
        else:
            q16, k32, k16, v32, v16 = qkv_prep(z.reshape(1, n, z.shape[-1]), tabs, q_blk=aq_blk, dh=adh, heads=ah,
                                               tm=_row_tile(n, PREP_TILE), v_transposed=False, layer=l, depth=depth,
                                               prev=None if acc is None else acc[2:4])
            q16, k16, v16 = (a.reshape(b, t, aw) for a in (q16, k16, v16))
        if prompt:
            o_c = attn_prompt(q16, k16, v16, diff_lambda[l], diff_norm_w[l], tq=_row_tile(t, ATTN_TILE), dh=adh,
                              lam_init=lam_init)
        else:
            o_c = attn_sample(q16, k16, v16, ck, cv, page_table, diff_lambda[l], diff_norm_w[l], layer=l, heads=ah, dh=adh,
                              lam_init=lam_init)
        x = merge_ffn(x, o_a.reshape(n, aw), o_b.reshape(n, aw), o_c.reshape(n, aw), z, wb, wo, mix_norm_post[l],
                      ffn2_norm_pre[l], f2_in, f2_out, ffn2_norm_post[l], layer=l, tm=tmf, tf=FF_TILE, gate_blk=gate_blk)
        return x, new_conv, (new_gdn, new_hgrn, k32, v32)

    xp = x_prompt.reshape(bp * tp, d)
    xs = x_sample.reshape(bs * ts, d)
    conv_p, conv_s, acc_p, acc_s = [], [], None, None
    for l in range(depth):
        w_main, w_small = _split_w_in(w_in[l], sizes)
        xp, cp, acc_p = run_layer(l, xp, bp, tp, zeros_conv, zeros_gdn, zeros_hgrn, 0, tabs_p, True, w_main, w_small, acc_p)
        conv_p.append(cp)
        xs, cs, acc_s = run_layer(l, xs, bs, ts, state_gdn_conv, state_gdn, state_hgrn, l, tabs_s, False, w_main, w_small,
                                  acc_s)
        conv_s.append(cs)

    def finish(x, b, t, convs, acc):
        new_gdn, new_hgrn, k32, v32 = acc
        return (x.reshape(b, t, d), new_gdn, jnp.stack(convs), new_hgrn,
                k32.reshape(depth, b, t, ah, adh2), v32.reshape(depth, b, t, ah, adv))

    yp, p_gdn, p_conv, p_hgrn, p_k, p_v = finish(xp, bp, tp, conv_p, acc_p)
    ys, s_gdn, s_conv, s_hgrn, s_k, s_v = finish(xs, bs, ts, conv_s, acc_s)
    return (yp, ys, p_gdn, p_conv, p_hgrn, p_k, p_v, s_gdn, s_conv, s_hgrn, s_k, s_v)
```

```python
import functools
import math

import jax
import jax.numpy as jnp
from jax import lax
from jax.experimental import pallas as pl
from jax.experimental.pallas import tpu as pltpu

F32 = jnp.float32
BF16 = jnp.bfloat16
EPS = 1e-6

VMEM_LIMIT_BYTES = 56 * 1024 * 1024


def _cparams(sem):
    return pltpu.CompilerParams(dimension_semantics=sem, vmem_limit_bytes=VMEM_LIMIT_BYTES)


def _rms(x, w):
    return x * lax.rsqrt(jnp.mean(x * x, axis=-1, keepdims=True) + EPS) * w


def _sigmoid(x):
    return 1.0 / (1.0 + jnp.exp(-x))


def _silu(x):
    return x * _sigmoid(x)


def _softplus(x):
    return jnp.maximum(x, 0.0) + jnp.log1p(jnp.exp(-jnp.abs(x)))


def _ffn_body(x, pre_ref, wi_ref, wo_ref, post_ref, a_scr, tf):
    dff = wo_ref.shape[0]
    h = _rms(x, pre_ref[...]).astype(BF16)
    for j in range(dff // tf):
        cols = slice(j * tf, (j + 1) * tf)
        g = _dot(h, wi_ref[:, cols])
        u = _dot(h, wi_ref[:, dff + j * tf:dff + (j + 1) * tf])
        a_scr[:, cols] = (_silu(g) * u).astype(BF16)
    f = _dot(a_scr[...], wo_ref[...])
    return x + 0.5 * _rms(f, post_ref[...])


def _ffn_kernel(x_ref, pre_ref, wi_ref, wo_ref, post_ref, o_ref, a_scr, *, tf):
    o_ref[...] = _ffn_body(x_ref[...], pre_ref, wi_ref, wo_ref, post_ref, a_scr, tf)


def ffn(x, pre_w, w_in_bf, w_out_bf, post_w, *, layer, tm, tf):
    n, d = x.shape
    dff = w_out_bf.shape[1]
    resident = dict(pipeline_mode=pl.Buffered(1))
    return pl.pallas_call(
        functools.partial(_ffn_kernel, tf=tf),
        grid=(n // tm,),
        in_specs=[
            pl.BlockSpec((tm, d), lambda i: (i, 0)),
            pl.BlockSpec((1, d), lambda i: (0, 0)),
            pl.BlockSpec((None, d, 2 * dff), lambda i: (layer, 0, 0), **resident),
            pl.BlockSpec((None, dff, d), lambda i: (layer, 0, 0), **resident),
            pl.BlockSpec((1, d), lambda i: (0, 0)),
        ],
        out_specs=pl.BlockSpec((tm, d), lambda i: (i, 0)),
        out_shape=jax.ShapeDtypeStruct((n, d), F32),
        scratch_shapes=[pltpu.VMEM((tm, dff), BF16)],
        compiler_params=_cparams(("parallel",)),
        name="ffn",
    )(x, pre_w.reshape(1, d), w_in_bf, w_out_bf, post_w.reshape(1, d))


def _proj_kernel(x_ref, pre_ref, w_ref, ws_ref, z_ref, zs_ref, h_scr):
    j = pl.program_id(1)

    @pl.when(j == 0)
    def _():
        h = _rms(x_ref[...], pre_ref[...]).astype(BF16)
        h_scr[...] = h
        zs_ref[...] = jnp.dot(h, ws_ref[...], preferred_element_type=F32)

    z_ref[...] = jnp.dot(h_scr[...], w_ref[...], preferred_element_type=F32)


def proj(x, pre_w, w_main_bf, w_small_bf, *, tm, tn):
    n, d = x.shape
    cols = w_main_bf.shape[1]
    sc = w_small_bf.shape[1]
    return pl.pallas_call(
        _proj_kernel,
        grid=(n // tm, cols // tn),
        in_specs=[
            pl.BlockSpec((tm, d), lambda i, j: (i, 0)),
            pl.BlockSpec((1, d), lambda i, j: (0, 0)),
            pl.BlockSpec((d, tn), lambda i, j: (0, j)),
            pl.BlockSpec((d, sc), lambda i, j: (0, 0)),
        ],
        out_specs=[
            pl.BlockSpec((tm, tn), lambda i, j: (i, j)),
            pl.BlockSpec((tm, sc), lambda i, j: (i, 0)),
        ],
        out_shape=[jax.ShapeDtypeStruct((n, cols), F32), jax.ShapeDtypeStruct((n, sc), F32)],
        scratch_shapes=[pltpu.VMEM((tm, d), BF16)],
        compiler_params=_cparams(("parallel", "arbitrary")),
        name="proj",
    )(x, pre_w.reshape(1, d), w_main_bf, w_small_bf)


def _dot(a, b):
    return jnp.dot(a, b, preferred_element_type=F32)


def _dot_nt(a, b):
    return lax.dot_general(a, b, (((1,), (1,)), ((), ())), preferred_element_type=F32)


def _dot_tn(a, b):
    return lax.dot_general(a, b, (((0,), (0,)), ((), ())), preferred_element_type=F32)


def _iota2(shape, dim):
    return lax.broadcasted_iota(jnp.int32, shape, dim)


def _dot_cumsum(tril, x):
    hi = x.astype(BF16)
    r1 = x - hi.astype(F32)
    mid = r1.astype(BF16)
    lo = (r1 - mid.astype(F32)).astype(BF16)
    t16 = tril.astype(BF16)
    return _dot(t16, hi) + (_dot(t16, mid) + _dot(t16, lo))


def _tri_inverse_all(lmats, c):
    ri = _iota2((c, c), 0)
    ci = _iota2((c, c), 1)
    eye = jnp.where(ri == ci, 1.0, 0.0).astype(F32)
    nb = min(16, c)
    sh = int(math.log2(nb))
    same = (ri >> sh) == (ci >> sh)
    dot1 = lambda a, b: _dot(a.astype(BF16), b.astype(BF16))
    ps = [jnp.where(same, lm, 0.0) for lm in lmats]
    ts = [eye - p for p in ps]
    k = 2
    while k < nb:
        ps = [dot1(p, p) for p in ps]
        ts = [t + dot1(t, p) for t, p in zip(ts, ps)]
        k *= 2
    blk = nb
    while blk < c:
        s1 = int(math.log2(blk))
        offm = ((ri >> (s1 + 1)) == (ci >> (s1 + 1))) & ((ri >> s1) != (ci >> s1))
        tl = [dot1(t, jnp.where(offm, lm, 0.0)) for t, lm in zip(ts, lmats)]
        ts = [t - dot1(x, t) for t, x in zip(ts, tl)]
        blk *= 2
    return ts


CONV_TAPS = 4
CONV_TOP = 8
CONV_HIST = CONV_TOP - (CONV_TAPS - 1)


def _gdn_kernel(qkv_ref, za_ref, zs_ref, cs_ref, s0_ref, cw_ref, alog_ref, dtb_ref, nw_ref,
                o_ref, sout_ref, cout_ref, xbuf, s_scr, *, c, nc, bb, heads, dk, dv):
    n = pl.program_id(1)
    nlast = pl.num_programs(1) - 1
    nqk = heads * dk
    r = nc * c

    @pl.when(n == 0)
    def _():
        xbuf[:, CONV_HIST:CONV_TOP, :] = cs_ref[...]
        s_scr[...] = s0_ref[...]

    xbuf[:, CONV_TOP:CONV_TOP + r, :] = qkv_ref[...]

    @pl.when(n == nlast)
    def _():
        cout_ref[...] = xbuf[:, CONV_HIST + r:CONV_TOP + r, :]

    ri = _iota2((c, c), 0)
    ci = _iota2((c, c), 1)
    incl = ri >= ci
    strict = ri > ci
    eye = jnp.where(ri == ci, 1.0, 0.0).astype(F32)
    rr = _iota2((r, r), 0)
    rc = _iota2((r, r), 1)
    sh = int(math.log2(c))
    blocktril = jnp.where((rr >= rc) & ((rr >> sh) == (rc >> sh)), 1.0, 0.0).astype(F32)
    nw = nw_ref[...]

    inst = []
    for bi in range(bb):
        y = xbuf[bi, CONV_HIST:CONV_HIST + r, :] * cw_ref[0:1, :]
        for j in range(1, CONV_TAPS):
            y = y + xbuf[bi, CONV_HIST + j:CONV_HIST + j + r, :] * cw_ref[j:j + 1, :]
        y = _silu(y)
        zs = zs_ref[bi]
        beta_all = _sigmoid(zs)
        g_all = -jnp.exp(alog_ref[...]) * _softplus(zs + dtb_ref[...])
        gcum_all = _dot_cumsum(blocktril, g_all)
        gcum_t = gcum_all.T
        for i in range(nc):
            rows = slice(i * c, (i + 1) * c)
            for h in range(heads):
                inst.append(dict(
                    bi=bi, i=i, h=h,
                    q=y[rows, h * dk:(h + 1) * dk],
                    k=y[rows, nqk + h * dk:nqk + (h + 1) * dk],
                    v=y[rows, 2 * nqk + h * dv:2 * nqk + (h + 1) * dv],
                    beta=beta_all[rows, h:h + 1],
                    gc=gcum_all[rows, heads + h:heads + h + 1],
                    grow=gcum_t[heads + h:heads + h + 1, rows]))
    xbuf[:, CONV_HIST:CONV_TOP, :] = xbuf[:, CONV_HIST + r:CONV_TOP + r, :]

    for d in inst:
        q, k = d["q"], d["k"]
        d["q"] = q * lax.rsqrt(jnp.sum(q * q, axis=-1, keepdims=True) + EPS) * (dk ** -0.5)
        d["k"] = k * lax.rsqrt(jnp.sum(k * k, axis=-1, keepdims=True) + EPS)
        d["k16"] = d["k"].astype(BF16)
        d["kb"] = d["k"] * d["beta"]
        d["decay"] = jnp.where(incl, jnp.exp(jnp.minimum(d["gc"] - d["grow"], 0.0)), 0.0)
    for d in inst:
        d["lmat"] = jnp.where(strict, _dot_nt(d["kb"].astype(BF16), d["k16"]) * d["decay"], 0.0)
    tinvs = _tri_inverse_all([d["lmat"] for d in inst], c)
    for d, tinv in zip(inst, tinvs):
        eg = jnp.exp(d["gc"])
        rhs = jnp.concatenate([d["v"] * d["beta"], d["kb"] * eg], axis=-1)
        sol = rhs + _dot((tinv - eye).astype(BF16), rhs.astype(BF16))
        d["u"] = sol[:, :dv]
        d["w16"] = sol[:, dv:].astype(BF16)
        d["attn16"] = (_dot_nt(d["q"].astype(BF16), d["k16"]) * d["decay"]).astype(BF16)
        d["qe16"] = (d["q"] * eg).astype(BF16)
        gl = d["gc"][c - 1:c, :]
        d["kdec16"] = (d["k"] * jnp.exp(gl - d["gc"])).astype(BF16)
        d["egl"] = jnp.exp(gl)

    state = {(bi, h): s_scr[bi, h] for bi in range(bb) for h in range(heads)}
    for i in range(nc):
        for d in inst:
            if d["i"] != i:
                continue
            bi, h = d["bi"], d["h"]
            s = state[(bi, h)]
            s16 = s.astype(BF16)
            v_new = d["u"] - _dot(d["w16"], s16)
            vn16 = v_new.astype(BF16)
            o = _dot(d["qe16"], s16) + _dot(d["attn16"], vn16)
            state[(bi, h)] = s * d["egl"] + _dot_tn(d["kdec16"], vn16)
            za = za_ref[bi, i * c:(i + 1) * c, h * dv:(h + 1) * dv]
            o_ref[bi, i * c:(i + 1) * c, h * dv:(h + 1) * dv] = (_rms(o, nw) * _silu(za)).astype(o_ref.dtype)
    for (bi, h), s in state.items():
        s_scr[bi, h] = s

    @pl.when(n == nlast)
    def _():
        sout_ref[...] = s_scr[...]


def gdn(z3, zs3, conv_state, state, conv_w, a_log, dt_bias, norm_w, *, state_layer, c, nc, bb, qkv_blk, za_blk,
        layer, depth, prev, out_dtype):
    b, t, _ = z3.shape
    _, _, heads, dk, dv = state.shape
    cch = conv_state.shape[-1]
    sc = zs3.shape[-1]
    r = nc * c
    pad = jnp.zeros((1, sc), F32)
    alog_p = lax.dynamic_update_slice(pad, a_log.reshape(1, heads).astype(F32), (0, heads))
    dtb_p = lax.dynamic_update_slice(pad, dt_bias.reshape(1, heads).astype(F32), (0, heads))
    kern = functools.partial(_gdn_kernel, c=c, nc=nc, bb=bb, heads=heads, dk=dk, dv=dv)
    prev_ops, prev_specs, aliases = _alias_plan(prev, 9, (1,))
    return pl.pallas_call(
        _drop_refs(kern, 9, len(prev_ops)),
        grid=(b // bb, t // r),
        in_specs=[
            pl.BlockSpec((bb, r, cch), lambda i, n: (i, n, qkv_blk)),
            pl.BlockSpec((bb, r, heads * dv), lambda i, n: (i, n, za_blk)),
            pl.BlockSpec((bb, r, sc), lambda i, n: (i, n, 0)),
            pl.BlockSpec((None, bb, CONV_TAPS - 1, cch), lambda i, n: (state_layer, i, 0, 0)),
            pl.BlockSpec((None, bb, heads, dk, dv), lambda i, n: (state_layer, i, 0, 0, 0)),
            pl.BlockSpec((CONV_TAPS, cch), lambda i, n: (0, 0)),
            pl.BlockSpec((1, sc), lambda i, n: (0, 0)),
            pl.BlockSpec((1, sc), lambda i, n: (0, 0)),
            pl.BlockSpec((1, dv), lambda i, n: (0, 0)),
        ] + prev_specs,
        out_specs=[
            pl.BlockSpec((bb, r, heads * dv), lambda i, n: (i, n, 0)),
            pl.BlockSpec((None, bb, heads, dk, dv), lambda i, n: (layer, i, 0, 0, 0)),
            pl.BlockSpec((bb, CONV_TAPS - 1, cch), lambda i, n: (i, 0, 0)),
        ],
        out_shape=[
            jax.ShapeDtypeStruct((b, t, heads * dv), out_dtype),
            jax.ShapeDtypeStruct((depth, b, heads, dk, dv), F32),
            jax.ShapeDtypeStruct((b, 3, cch), F32),
        ],
        scratch_shapes=[pltpu.VMEM((bb, CONV_TOP + r, cch), F32), pltpu.VMEM((bb, heads, dk, dv), F32)],
        input_output_aliases=aliases,
        compiler_params=_cparams(("parallel", "arbitrary")),
        name="gdn",
    )(z3, z3, zs3, conv_state, state, conv_w, alog_p, dtb_p, norm_w.reshape(1, dv), *prev_ops)


SB = 8


def _hgrn_kernel(q_ref, f_ref, i_ref, g_ref, lbraw_ref, s0_ref, nw_ref, o_ref, sout_ref,
                 st_scr, *, c, nc, bb, heads, dk, dv, layer):
    n = pl.program_id(1)
    nlast = pl.num_programs(1) - 1
    bh = [(bi, h) for bi in range(bb) for h in range(heads)]

    @pl.when(n == 0)
    def _():
        for bi, h in bh:
            st_scr[bi * heads + h] = s0_ref[bi, h].T

    raw = lbraw_ref[...]
    e = jnp.exp(raw - jnp.max(raw, axis=0, keepdims=True))
    sm = e / jnp.sum(e, axis=0, keepdims=True)
    lb = jnp.zeros((1, heads * dk), F32)
    for l in range(1, layer + 1):
        lb = lb + sm[l:l + 1, :]

    r = bb * nc * c
    flat = lambda ref: ref[...].reshape(r, ref.shape[-1])
    zf = flat(f_ref)
    logsig = jnp.minimum(zf, 0.0) - jnp.log1p(jnp.exp(-jnp.abs(zf)))
    la = jnp.log(lb)
    lbb = jnp.log1p(-lb) + logsig
    logf = jnp.maximum(la, lbb) + jnp.log1p(jnp.exp(-jnp.abs(la - lbb)))
    kh = (1.0 - lb) * _sigmoid(-zf)
    qh = _silu(flat(q_ref)) * (dk ** -0.5)
    vv = flat(i_ref)
    gate = _sigmoid(flat(g_ref))

    ri = _iota2((r, r), 0)
    ci = _iota2((r, r), 1)
    sh = int(math.log2(c))
    tril = jnp.where((ri >= ci) & ((ri >> sh) == (ci >> sh)), 1.0, 0.0).astype(F32)
    bcum = _dot_cumsum(tril, logf)
    fgate = jnp.exp(logf)
    nsb = c // SB
    sub = _iota2((nsb, SB, dk), 1)
    nw = nw_ref[...]
    g3 = lambda x: x.reshape(nsb, SB, x.shape[-1])
    states = {(bi, h): st_scr[bi * heads + h] for bi, h in bh}

    for ic, bi, h in [(ic, bi, h) for ic in range(nc) for bi, h in bh]:
        rows = slice((bi * nc + ic) * c, (bi * nc + ic + 1) * c)
        ks = slice(h * dk, (h + 1) * dk)
        vs = slice(h * dv, (h + 1) * dv)
        q = qh[rows, ks]
        k = kh[rows, ks]
        b = bcum[rows, ks]
        v = vv[rows, vs]
        st = states[(bi, h)]
        o = _dot_nt((q * jnp.exp(b)).astype(BF16), st.astype(BF16))
        q3, k3, v3, f3 = g3(q), g3(k), g3(v), g3(fgate[rows, ks])
        o3 = jnp.sum(q3 * k3, axis=-1, keepdims=True) * v3
        e = None
        for d in range(1, SB):
            fd = f3 if d == 1 else pltpu.roll(f3, d - 1, 1)
            e = fd if e is None else e * fd
            m = jnp.where(sub >= d, q3 * pltpu.roll(k3, d, 1) * e, 0.0)
            o3 = o3 + jnp.sum(m, axis=-1, keepdims=True) * pltpu.roll(v3, d, 1)
        o = o + o3.reshape(c, dv)
        if nsb > 1:
            b3 = g3(b)
            rend = jnp.broadcast_to(b3[:, SB - 1:SB, :], b3.shape).reshape(c, dk)
            khat = k * jnp.exp(jnp.minimum(rend - b, 0.0))
            zq = jnp.zeros((c, dk), F32)
            qs, kk = [], []
            for jb in range(nsb - 1):
                lo, hi = jb * SB, (jb + 1) * SB
                rj = b[hi - 1:hi, :]
                qpart = q[hi:, :] * jnp.exp(jnp.minimum(b[hi:, :] - rj, 0.0))
                qs.append(jnp.concatenate([zq[:hi], qpart], axis=0))
                kparts = [khat[lo:hi]] if lo == 0 else [zq[:lo], khat[lo:hi]]
                kk.append(jnp.concatenate(kparts + [zq[hi:]], axis=0))
            amat = _dot_nt(jnp.concatenate(qs, axis=-1).astype(BF16), jnp.concatenate(kk, axis=-1).astype(BF16))
            o = o + _dot(amat.astype(BF16), v.astype(BF16))
        bl = b[c - 1:c, :]
        kdec = k * jnp.exp(bl - b)
        states[(bi, h)] = st * jnp.exp(bl) + _dot_tn(v.astype(BF16), kdec.astype(BF16))
        o_ref[bi, ic * c:(ic + 1) * c, vs] = (_rms(o, nw) * gate[rows, vs]).astype(o_ref.dtype)
    for bi, h in bh:
        st_scr[bi * heads + h] = states[(bi, h)]

    @pl.when(n == nlast)
    def _():
        for bi, h in bh:
            sout_ref[bi, h] = st_scr[bi * heads + h].T


def hgrn(z3, lb_raw, state, norm_w, *, state_layer, c, nc, bb, layer, q_blk, prev, out_dtype):
    b, t, _ = z3.shape
    _, _, heads, dk, dv = state.shape
    depth = lb_raw.shape[0]
    wk = heads * dk
    assert c % SB == 0
    kern = functools.partial(_hgrn_kernel, c=c, nc=nc, bb=bb, heads=heads, dk=dk, dv=dv, layer=layer)
    r = nc * c
    zspec = lambda off: pl.BlockSpec((bb, r, wk), lambda i, n: (i, n, q_blk + off))
    prev_ops, prev_specs, aliases = _alias_plan(prev, 7, (1,))
    return pl.pallas_call(
        _drop_refs(kern, 7, len(prev_ops)),
        grid=(b // bb, t // r),
        in_specs=[
            zspec(0), zspec(1), zspec(2), zspec(3),
            pl.BlockSpec((depth, wk), lambda i, n: (0, 0)),
            pl.BlockSpec((None, bb, heads, dk, dv), lambda i, n: (state_layer, i, 0, 0, 0)),
            pl.BlockSpec((1, dv), lambda i, n: (0, 0)),
        ] + prev_specs,
        out_specs=[
            pl.BlockSpec((bb, r, heads * dv), lambda i, n: (i, n, 0)),
            pl.BlockSpec((None, bb, heads, dk, dv), lambda i, n: (layer, i, 0, 0, 0)),
        ],
        out_shape=[
            jax.ShapeDtypeStruct((b, t, heads * dv), out_dtype),
            jax.ShapeDtypeStruct((depth, b, heads, dk, dv), F32),
        ],
        scratch_shapes=[pltpu.VMEM((bb * heads, dv, dk), F32)],
        input_output_aliases=aliases,
        compiler_params=_cparams(("parallel", "arbitrary")),
        name="hgrn",
    )(z3, z3, z3, z3, lb_raw, state, norm_w.reshape(1, dv), *prev_ops)


def rope_tables(t, pos0, dh, theta):
    rd = dh // 4
    half = rd // 2
    inv = jnp.power(jnp.float32(theta), -jnp.arange(half, dtype=F32) / half)
    ang = (pos0 + jnp.arange(t)).astype(F32)[:, None] * inv
    cos, sin = jnp.cos(ang), jnp.sin(ang)
    d = jnp.arange(128) % dh
    f = d % half
    cc = jnp.where(d < rd, cos[:, f], 1.0)
    s1 = jnp.where(d < half, -sin[:, f], 0.0)
    s2 = jnp.where((d >= half) & (d < rd), sin[:, f], 0.0)
    return cc.astype(F32), s1.astype(F32), s2.astype(F32)


def _prep_kernel(q_ref, k_ref, v_ref, c_ref, s1_ref, s2_ref, q16_ref, k32_ref, k16_ref, v32_ref, v16_ref,
                 *, scale, half, heads, v_transposed):
    tm, w = q_ref.shape[1:]
    hw = w // heads
    reps = w // c_ref.shape[-1]
    cc = jnp.concatenate([c_ref[...]] * reps, axis=-1)
    s1 = jnp.concatenate([s1_ref[...]] * reps, axis=-1)
    s2 = jnp.concatenate([s2_ref[...]] * reps, axis=-1)

    def rope(x):
        return x * cc + pltpu.roll(x, w - half, 1) * s1 + pltpu.roll(x, half, 1) * s2

    q16_ref[0] = (rope(q_ref[0]) * scale).astype(BF16)
    k = rope(k_ref[0])
    k16_ref[0] = k.astype(BF16)
    v = v_ref[0]
    v16_ref[0] = (v.T if v_transposed else v).astype(BF16)
    for h in range(heads):
        k32_ref[0, pl.ds(h, tm, stride=heads), :] = k[:, h * hw:(h + 1) * hw]
        v32_ref[0, pl.ds(h, tm, stride=heads), :] = v[:, h * hw:(h + 1) * hw]


def _drop_refs(kern, start, count):
    def wrapped(*refs):
        return kern(*refs[:start], *refs[start + count:])
    return wrapped


def _alias_plan(prev, n_in, out_idx):
    prev = [] if prev is None else list(prev)
    specs = [pl.BlockSpec(memory_space=pl.ANY)] * len(prev)
    return prev, specs, {n_in + k: out_idx[k] for k in range(len(prev))}


def qkv_prep(z3, tables, *, q_blk, dh, heads, tm, v_transposed, layer, depth, prev):
    b, t, _ = z3.shape
    w = tables[0].shape[-1] * (heads * 2 * dh // tables[0].shape[-1])
    hw = w // heads
    cc, s1, s2 = tables
    kern = functools.partial(_prep_kernel, scale=dh ** -0.5 * math.log2(math.e), half=dh // 8, heads=heads,
                             v_transposed=v_transposed)
    zspec = lambda off: pl.BlockSpec((1, tm, w), lambda i, n: (i, n, q_blk + off))
    tspec = pl.BlockSpec((tm, cc.shape[-1]), lambda i, n: (n, 0))
    ospec = pl.BlockSpec((1, tm, w), lambda i, n: (i, n, 0))
    cspec = pl.BlockSpec((None, 1, tm * heads, hw), lambda i, n: (layer, i, n, 0))
    sd = lambda dt: jax.ShapeDtypeStruct((b, t, w), dt)
    sc = jax.ShapeDtypeStruct((depth, b, t * heads, hw), F32)
    vspec = pl.BlockSpec((1, w, tm), lambda i, n: (i, 0, n)) if v_transposed else ospec
    vsd = jax.ShapeDtypeStruct((b, w, t), BF16) if v_transposed else sd(BF16)
    prev_ops, prev_specs, aliases = _alias_plan(prev, 6, (1, 3))
    return pl.pallas_call(
        _drop_refs(kern, 6, len(prev_ops)),
        grid=(b, t // tm),
        in_specs=[zspec(0), zspec(1), zspec(2), tspec, tspec, tspec] + prev_specs,
        out_specs=[ospec, cspec, ospec, cspec, vspec],
        out_shape=[sd(BF16), sc, sd(BF16), sc, vsd],
        input_output_aliases=aliases,
        compiler_params=_cparams(("parallel", "parallel")),
        name="qkv_prep",
    )(z3, z3, z3, cc, s1, s2, *prev_ops)


def _lambda(lam_ref, lam_init):
    lm = lam_ref[...]
    a = jnp.sum(lm[0:1, :] * lm[1:2, :], axis=-1, keepdims=True)
    b = jnp.sum(lm[2:3, :] * lm[3:4, :], axis=-1, keepdims=True)
    return jnp.exp(a) - jnp.exp(b) + lam_init


ONES_ROWS = 16


def _attn_prompt_kernel(qt_ref, kt_ref, q_ref, k_ref, vt_ref, lam_ref, nw_ref, o_ref,
                        qs_scr, m_scr, acc_scr, *, tq, dh, dv, hpb, qchunk, lam_init):
    p = pl.program_id(2)
    qi = qt_ref[p]
    ki = kt_ref[p]

    @pl.when(ki == 0)
    def _():
        for hh in range(hpb):
            q = q_ref[0, :, hh * dv:(hh + 1) * dv]
            lane = _iota2(q.shape, 1)
            zero = jnp.zeros_like(q)
            qs_scr[hh, 0:tq, :] = jnp.where(lane < dh, q, zero)
            qs_scr[hh, tq:2 * tq, :] = jnp.where(lane >= dh, q, zero)
        m_scr[...] = jnp.full(m_scr.shape, -jnp.inf, F32)
        acc_scr[...] = jnp.zeros_like(acc_scr)

    def update(diag):
        for hh in range(hpb):
            k = k_ref[0, :, hh * dv:(hh + 1) * dv]
            vt = vt_ref[0, hh * dv:(hh + 1) * dv, :]
            vt1 = jnp.concatenate([vt, jnp.ones((ONES_ROWS, vt.shape[1]), BF16)], axis=0)
            for c0 in range(0, 2 * tq, qchunk):
                cs = slice(c0, c0 + qchunk)
                st = _dot_nt(k, qs_scr[hh, cs, :])
                if diag:
                    key = _iota2(st.shape, 0)
                    qry = (_iota2(st.shape, 1) + c0) & (tq - 1)
                    st = jnp.where(key <= qry, st, -jnp.inf)
                m_prev = m_scr[hh, :, cs]
                m_new = jnp.maximum(m_prev, jnp.max(st, axis=0, keepdims=True))
                alpha = jnp.exp2(m_prev - m_new)
                pr = jnp.exp2(st - m_new).astype(BF16)
                acc_scr[hh, :, cs] = alpha * acc_scr[hh, :, cs] + _dot(vt1, pr)
                m_scr[hh, :, cs] = m_new

    @pl.when(ki < qi)
    def _():
        update(False)

    @pl.when(ki == qi)
    def _():
        update(True)
        lam = _lambda(lam_ref, lam_init)
        for hh in range(hpb):
            o1 = acc_scr[hh, 0:dv, 0:tq] / acc_scr[hh, dv:dv + 1, 0:tq]
            o2 = acc_scr[hh, 0:dv, tq:2 * tq] / acc_scr[hh, dv:dv + 1, tq:2 * tq]
            o = (o1 - lam * o2).T
            o_ref[0, :, hh * dv:(hh + 1) * dv] = (_rms(o, nw_ref[...]) * (1.0 - lam_init)).astype(o_ref.dtype)


def attn_prompt(q16, k16, vt16, lam, norm_w, *, tq, dh, lam_init):
    b, t, w = q16.shape
    dv = norm_w.shape[-1]
    heads = w // dv
    nq = t // tq
    pairs = [(i, j) for i in range(nq) for j in range(i + 1)]
    qt = jnp.asarray([pq for pq, _ in pairs], jnp.int32)
    kt = jnp.asarray([pk for _, pk in pairs], jnp.int32)
    hpb = math.gcd(heads, ATTN_HEADS_PER_STEP)
    kern = functools.partial(_attn_prompt_kernel, tq=tq, dh=dh, dv=dv, hpb=hpb, qchunk=min(ATTN_QCHUNK, 2 * tq),
                             lam_init=lam_init)
    grid_spec = pltpu.PrefetchScalarGridSpec(
        num_scalar_prefetch=2,
        grid=(b, heads // hpb, len(pairs)),
        in_specs=[
            pl.BlockSpec((1, tq, hpb * dv), lambda i, h, p, qt, kt: (i, qt[p], h)),
            pl.BlockSpec((1, tq, hpb * dv), lambda i, h, p, qt, kt: (i, kt[p], h)),
            pl.BlockSpec((1, hpb * dv, tq), lambda i, h, p, qt, kt: (i, h, kt[p])),
            pl.BlockSpec(lam.shape, lambda i, h, p, qt, kt: (0, 0)),
            pl.BlockSpec((1, dv), lambda i, h, p, qt, kt: (0, 0)),
        ],
        out_specs=pl.BlockSpec((1, tq, hpb * dv), lambda i, h, p, qt, kt: (i, qt[p], h)),
        scratch_shapes=[
            pltpu.VMEM((hpb, 2 * tq, dv), BF16),
            pltpu.VMEM((hpb, 1, 2 * tq), F32),
            pltpu.VMEM((hpb, dv + ONES_ROWS, 2 * tq), F32),
        ],
    )
    return pl.pallas_call(
        kern,
        grid_spec=grid_spec,
        out_shape=jax.ShapeDtypeStruct((b, t, w), BF16),
        compiler_params=_cparams(("parallel", "parallel", "arbitrary")),
        name="attn_prompt",
    )(qt, kt, q16, k16, vt16, lam, norm_w.reshape(1, dv))


def _attn_sample_kernel(pt_ref, q_ref, kc_ref, vc_ref, lam_ref, nw_ref, *rest,
                        npages, tq, heads, dh, dv, page, lam_init):
    k_refs = rest[:npages]
    v_refs = rest[npages:2 * npages]
    o_ref = rest[2 * npages]
    lam = _lambda(lam_ref, lam_init)
    nw = nw_ref[...]
    q = q_ref[0].astype(F32)
    kc = kc_ref[0]
    vc = vc_ref[0]
    lane = _iota2((tq, dv), 1)
    qidx = _iota2((2 * tq, tq), 0) % tq
    kidx = _iota2((2 * tq, tq), 1)
    hsl = [slice(h * dv, (h + 1) * dv) for h in range(heads)]

    def head_rows(refs, h):
        return jnp.concatenate([r[0, 0, pl.ds(h, page, stride=heads), :].astype(BF16) for r in refs], axis=0)

    qrows = [jnp.concatenate([jnp.where(lane < dh, q[:, hs], 0.0), jnp.where(lane >= dh, q[:, hs], 0.0)],
                             axis=0).astype(BF16) for hs in hsl]
    s_past = [_dot_nt(qrows[h], head_rows(k_refs, h)) for h in range(heads)]
    s_cur = [jnp.where(kidx <= qidx, _dot_nt(qrows[h], kc[:, hsl[h]]), -jnp.inf) for h in range(heads)]
    ms = [jnp.maximum(jnp.max(sp, axis=-1, keepdims=True), jnp.max(sc, axis=-1, keepdims=True))
          for sp, sc in zip(s_past, s_cur)]
    p_past = [jnp.exp2(sp - m) for sp, m in zip(s_past, ms)]
    p_cur = [jnp.exp2(sc - m) for sc, m in zip(s_cur, ms)]
    invs = [1.0 / (jnp.sum(pp, axis=-1, keepdims=True) + jnp.sum(pc, axis=-1, keepdims=True))
            for pp, pc in zip(p_past, p_cur)]

    def diff(pp, inv):
        pn = pp * inv
        return pn[0:tq] - lam * pn[tq:2 * tq]

    for h in range(heads):
        o = (_dot(diff(p_past[h], invs[h]).astype(BF16), head_rows(v_refs, h))
             + _dot(diff(p_cur[h], invs[h]), vc[:, hsl[h]].astype(F32)))
        o_ref[0, :, hsl[h]] = _rms(o, nw) * (1.0 - lam_init)


def attn_sample(q16, k16, v16, cache_k, cache_v, page_table, lam, norm_w, *, layer, heads, dh, lam_init):
    b, tq, w = q16.shape
    dv = norm_w.shape[-1]
    npages = page_table.shape[1]
    prow = cache_k.shape[2]
    kern = functools.partial(_attn_sample_kernel, npages=npages, tq=tq, heads=heads, dh=dh, dv=dv,
                             page=prow // heads, lam_init=lam_init)
    cur = pl.BlockSpec((1, tq, w), lambda i, pt: (i, 0, 0))

    def page_spec(j):
        return pl.BlockSpec((1, 1, prow, dv), lambda i, pt: (layer, pt[i, j], 0, 0))

    grid_spec = pltpu.PrefetchScalarGridSpec(
        num_scalar_prefetch=1,
        grid=(b,),
        in_specs=[cur, cur, cur,
                  pl.BlockSpec(lam.shape, lambda i, pt: (0, 0)),
                  pl.BlockSpec((1, dv), lambda i, pt: (0, 0))]
        + [page_spec(j) for j in range(npages)] * 2,
        out_specs=pl.BlockSpec((1, tq, w), lambda i, pt: (i, 0, 0)),
    )
    return pl.pallas_call(
        kern,
        grid_spec=grid_spec,
        out_shape=jax.ShapeDtypeStruct((b, tq, w), F32),
        compiler_params=_cparams(("parallel",)),
        name="attn_sample",
    )(page_table, q16, k16, v16, lam, norm_w.reshape(1, dv), *([cache_k] * npages), *([cache_v] * npages))


def _merge_ffn_kernel(x_ref, oa_ref, ob_ref, oc_ref, g0_ref, g1_ref, g2_ref, wb_ref, wo_ref, mpost_ref,
                      pre_ref, wi_ref, wo2_ref, post_ref, o_ref, a_scr, *, tf):
    y = None
    for i, (o_i, g_i) in enumerate(((oa_ref, g0_ref), (ob_ref, g1_ref), (oc_ref, g2_ref))):
        ys = _dot(o_i[...].astype(BF16), wb_ref[i])
        t = _sigmoid(g_i[...]) * ys
        y = t if y is None else y + t
    y2 = _dot(y.astype(BF16), wo_ref[...])
    x1 = x_ref[...] + _rms(y2, mpost_ref[...])
    o_ref[...] = _ffn_body(x1, pre_ref, wi_ref, wo2_ref, post_ref, a_scr, tf)


def merge_ffn(x, o_a, o_b, o_c, z, w_branch_bf, w_out_bf, mpost_w, pre_w, w_in_bf, w_out2_bf, post_w, *,
              layer, tm, tf, gate_blk):
    n, d = x.shape
    bw = o_a.shape[-1]
    nb = w_branch_bf.shape[1]
    dff = w_out2_bf.shape[1]
    resident = dict(pipeline_mode=pl.Buffered(1))
    ospec = pl.BlockSpec((tm, bw), lambda i: (i, 0))
    gspec = lambda k: pl.BlockSpec((tm, d), lambda i: (i, gate_blk + k))
    vec = pl.BlockSpec((1, d), lambda i: (0, 0))
    return pl.pallas_call(
        functools.partial(_merge_ffn_kernel, tf=tf),
        grid=(n // tm,),
        in_specs=[
            pl.BlockSpec((tm, d), lambda i: (i, 0)),
            ospec, ospec, ospec, gspec(0), gspec(1), gspec(2),
            pl.BlockSpec((None, nb, bw, d), lambda i: (layer, 0, 0, 0), **resident),
            pl.BlockSpec((None, d, d), lambda i: (layer, 0, 0), **resident),
            vec, vec,
            pl.BlockSpec((None, d, 2 * dff), lambda i: (layer, 0, 0), **resident),
            pl.BlockSpec((None, dff, d), lambda i: (layer, 0, 0), **resident),
            vec,
        ],
        out_specs=pl.BlockSpec((tm, d), lambda i: (i, 0)),
        out_shape=jax.ShapeDtypeStruct((n, d), F32),
        scratch_shapes=[pltpu.VMEM((tm, dff), BF16)],
        compiler_params=_cparams(("parallel",)),
        name="merge_ffn",
    )(x, o_a, o_b, o_c, z, z, z, w_branch_bf, w_out_bf, mpost_w.reshape(1, d), pre_w.reshape(1, d), w_in_bf, w_out2_bf,
      post_w.reshape(1, d))


ROPE_THETA = 500000.0
SCAN_CHUNK = 64
GDN_CHUNKS_PER_STEP = 4
HGRN_CHUNKS_PER_STEP = 4
HGRN_INSTANCES = 32
GDN_INSTANCES = 64
ROW_TILE = 1024
FFN_ROW_TILE = 512
FF_TILE = 256
PROJ_COL_TILES = 4
ATTN_TILE = 512
ATTN_QCHUNK = 1024
ATTN_HEADS_PER_STEP = 4
PREP_TILE = 512
SMALL_COLS = 128


def _split_w_in(w_in_l, sizes):
    offs = [0]
    for s in sizes:
        offs.append(offs[-1] + s)
    seg = lambda i: w_in_l[:, offs[i]:offs[i + 1]]
    main = jnp.concatenate([seg(11), seg(0), seg(1), seg(4), seg(5), seg(6), seg(7), seg(8), seg(9), seg(10)], axis=1)
    small = jnp.concatenate([seg(2), seg(3)], axis=1)
    small = jnp.pad(small, ((0, 0), (0, SMALL_COLS - small.shape[1])))
    return main.astype(BF16), small.astype(BF16)


def _row_tile(n, pref):
    return pref if n % pref == 0 else n


def kernel(x_prompt, x_sample, state_gdn, state_gdn_conv, state_hgrn, cache_k, cache_v, page_table,
           ffn1_norm_pre, ffn1_norm_post, ffn1_w_in, ffn1_w_out, mix_norm_pre, mix_norm_post, w_in,
           gdn_conv_w, gdn_a_log, gdn_dt_bias, gdn_norm_w, hgrn_lb_raw, hgrn_norm_w, diff_lambda,
           diff_norm_w, w_branch, w_out, ffn2_norm_pre, ffn2_norm_post, ffn2_w_in, ffn2_w_out):
    depth = w_in.shape[0]
    bp, tp, d = x_prompt.shape
    bs, ts, _ = x_sample.shape
    _, _, gh, gdk, gdv = state_gdn.shape
    cch = state_gdn_conv.shape[-1]
    _, _, hh, hdk, hdv = state_hgrn.shape
    _, n_pool, page, ah, adh2 = cache_k.shape
    adh = adh2 // 2
    adv = cache_v.shape[-1]
    aw = ah * adv
    assert ah * adh2 == aw and hh * hdk == aw and hh * hdv == aw and gh * gdv == aw and cch == 3 * aw and d == 2 * aw
    sizes = (cch, gh * gdv, gh, gh, hh * hdk, hh * hdk, hh * hdv, hh * hdv, ah * adh2, ah * adh2, aw, 3 * d)
    gate_blk, qkv_blk, za_blk, hq_blk, aq_blk = 0, (3 * d) // cch, (3 * d + cch) // aw, (3 * d + cch) // aw + 1, (3 * d + cch) // aw + 5
    past_len = page_table.shape[1] * page
    ck = cache_k.reshape(depth, n_pool, page * ah, adh2)
    cv = cache_v.reshape(depth, n_pool, page * ah, adv)
    tabs_p = rope_tables(tp, 0, adh, ROPE_THETA)
    tabs_s = tuple(jnp.tile(a, (bs, 1)) for a in rope_tables(ts, past_len, adh, ROPE_THETA))
    assert gdn_conv_w.shape[1] == CONV_TAPS and state_gdn_conv.shape[2] == CONV_TAPS - 1
    zeros_conv = jnp.zeros((1, bp, CONV_TAPS - 1, cch), F32)
    zeros_gdn = jnp.zeros((1, bp, gh, gdk, gdv), F32)
    zeros_hgrn = jnp.zeros((1, bp, hh, hdk, hdv), F32)

    f1_in, f1_out = ffn1_w_in.astype(BF16), ffn1_w_out.astype(BF16)
    f2_in, f2_out = ffn2_w_in.astype(BF16), ffn2_w_out.astype(BF16)
    wb, wo = w_branch.astype(BF16), w_out.astype(BF16)

    def run_layer(l, x, b, t, conv_state, gdn_state, hgrn_state, sl, tabs, prompt, w_main, w_small, acc):
        n = b * t
        tm = _row_tile(n, ROW_TILE)
        lam_init = 0.8 - 0.6 * math.exp(-0.3 * l)
        tmf = _row_tile(n, FFN_ROW_TILE)
        x = ffn(x, ffn1_norm_pre[l], f1_in, f1_out, ffn1_norm_post[l], layer=l, tm=tmf, tf=FF_TILE)
        z, zs = proj(x, mix_norm_pre[l], w_main, w_small, tm=tm, tn=w_main.shape[1] // PROJ_COL_TILES)
        z3 = z.reshape(b, t, z.shape[-1])
        zs3 = zs.reshape(b, t, SMALL_COLS)
        c = math.gcd(SCAN_CHUNK, t)
        branch_dtype = BF16 if c % 16 == 0 else F32
        o_a, new_gdn, new_conv = gdn(z3, zs3, conv_state, gdn_state, gdn_conv_w[l], gdn_a_log[l], gdn_dt_bias[l],
                                     gdn_norm_w[l], state_layer=sl, c=c, nc=min(GDN_CHUNKS_PER_STEP, t // c),
                                     bb=math.gcd(b, max(1, GDN_INSTANCES // (gh * min(GDN_CHUNKS_PER_STEP, t // c)))),
                                     qkv_blk=qkv_blk, za_blk=za_blk, layer=l, depth=depth,
                                     prev=None if acc is None else acc[0:1], out_dtype=branch_dtype)
        o_b, new_hgrn = hgrn(z3, hgrn_lb_raw, hgrn_state, hgrn_norm_w[l], state_layer=sl, c=c,
                             nc=min(HGRN_CHUNKS_PER_STEP, t // c),
                             bb=math.gcd(b, max(1, HGRN_INSTANCES // (hh * min(HGRN_CHUNKS_PER_STEP, t // c)))),
                             layer=l, q_blk=hq_blk,
                             prev=None if acc is None else acc[1:2], out_dtype=branch_dtype)
        if prompt:
            q16, k32, k16, v32, v16 = qkv_prep(z3, tabs, q_blk=aq_blk, dh=adh, heads=ah, tm=_row_tile(t, PREP_TILE),
                                               v_transposed=True, layer=l, depth=depth,
                                               prev=None if acc is None else acc[2:4])
        else:
            q16, k32, k16, v32, v16 = qkv_prep(z.reshape(1, n, z.shape[-1]), tabs, q_blk=aq_blk, dh=adh, heads=ah,
                                               tm=_row_tile(n, PREP_TILE), v_transposed=False, layer=l, depth=depth,
                                               prev=None if acc is None else acc[2:4])
            q16, k16, v16 = (a.reshape(b, t, aw) for a in (q16, k16, v16))
        if prompt:
            o_c = attn_prompt(q16, k16, v16, diff_lambda[l], diff_norm_w[l], tq=_row_tile(t, ATTN_TILE), dh=adh,
                              lam_init=lam_init)
        else:
            o_c = attn_sample(q16, k16, v16, ck, cv, page_table, diff_lambda[l], diff_norm_w[l], layer=l, heads=ah, dh=adh,
                              lam_init=lam_init)
        x = merge_ffn(x, o_a.reshape(n, aw), o_b.reshape(n, aw), o_c.reshape(n, aw), z, wb, wo, mix_norm_post[l],
                      ffn2_norm_pre[l], f2_in, f2_out, ffn2_norm_post[l], layer=l, tm=tmf, tf=FF_TILE, gate_blk=gate_blk)
        return x, new_conv, (new_gdn, new_hgrn, k32, v32)

    xp = x_prompt.reshape(bp * tp, d)
    xs = x_sample.reshape(bs * ts, d)
    conv_p, conv_s, acc_p, acc_s = [], [], None, None
    for l in range(depth):
        w_main, w_small = _split_w_in(w_in[l], sizes)
        xp, cp, acc_p = run_layer(l, xp, bp, tp, zeros_conv, zeros_gdn, zeros_hgrn, 0, tabs_p, True, w_main, w_small, acc_p)
        conv_p.append(cp)
        xs, cs, acc_s = run_layer(l, xs, bs, ts, state_gdn_conv, state_gdn, state_hgrn, l, tabs_s, False, w_main, w_small,
                                  acc_s)
        conv_s.append(cs)

    def finish(x, b, t, convs, acc):
        new_gdn, new_hgrn, k32, v32 = acc
        return (x.reshape(b, t, d), new_gdn, jnp.stack(convs), new_hgrn,
                k32.reshape(depth, b, t, ah, adh2), v32.reshape(depth, b, t, ah, adv))

    yp, p_gdn, p_conv, p_hgrn, p_k, p_v = finish(xp, bp, tp, conv_p, acc_p)
    ys, s_gdn, s_conv, s_hgrn, s_k, s_v = finish(xs, bs, ts, conv_s, acc_s)
    return (yp, ys, p_gdn, p_conv, p_hgrn, p_k, p_v, s_gdn, s_conv, s_hgrn, s_k, s_v)
```

```python
import functools
import math

import jax
import jax.numpy as jnp
from jax import lax
from jax.experimental import pallas as pl
from jax.experimental.pallas import tpu as pltpu

F32 = jnp.float32
BF16 = jnp.bfloat16
EPS = 1e-6

VMEM_LIMIT_BYTES = 56 * 1024 * 1024


def _cparams(sem):
    return pltpu.CompilerParams(dimension_semantics=sem, vmem_limit_bytes=VMEM_LIMIT_BYTES)


def _rms(x, w):
    return x * lax.rsqrt(jnp.mean(x * x, axis=-1, keepdims=True) + EPS) * w


def _sigmoid(x):
    return 1.0 / (1.0 + jnp.exp(-x))


def _silu(x):
    return x * _sigmoid(x)


def _softplus(x):
    return jnp.maximum(x, 0.0) + jnp.log1p(jnp.exp(-jnp.abs(x)))


def _ffn_body(x, pre_ref, wi_ref, wo_ref, post_ref, a_scr, tf):
    dff = wo_ref.shape[0]
    h = _rms(x, pre_ref[...]).astype(BF16)
    for j in range(dff // tf):
        cols = slice(j * tf, (j + 1) * tf)
        g = _dot(h, wi_ref[:, cols])
        u = _dot(h, wi_ref[:, dff + j * tf:dff + (j + 1) * tf])
        a_scr[:, cols] = (_silu(g) * u).astype(BF16)
    f = _dot(a_scr[...], wo_ref[...])
    return x + 0.5 * _rms(f, post_ref[...])


def _ffn_kernel(x_ref, pre_ref, wi_ref, wo_ref, post_ref, o_ref, a_scr, *, tf):
    o_ref[...] = _ffn_body(x_ref[...], pre_ref, wi_ref, wo_ref, post_ref, a_scr, tf)


def ffn(x, pre_w, w_in_bf, w_out_bf, post_w, *, layer, tm, tf):
    n, d = x.shape
    dff = w_out_bf.shape[1]
    resident = dict(pipeline_mode=pl.Buffered(1))
    return pl.pallas_call(
        functools.partial(_ffn_kernel, tf=tf),
        grid=(n // tm,),
        in_specs=[
            pl.BlockSpec((tm, d), lambda i: (i, 0)),
            pl.BlockSpec((1, d), lambda i: (0, 0)),
            pl.BlockSpec((None, d, 2 * dff), lambda i: (layer, 0, 0), **resident),
            pl.BlockSpec((None, dff, d), lambda i: (layer, 0, 0), **resident),
            pl.BlockSpec((1, d), lambda i: (0, 0)),
        ],
        out_specs=pl.BlockSpec((tm, d), lambda i: (i, 0)),
        out_shape=jax.ShapeDtypeStruct((n, d), F32),
        scratch_shapes=[pltpu.VMEM((tm, dff), BF16)],
        compiler_params=_cparams(("parallel",)),
        name="ffn",
    )(x, pre_w.reshape(1, d), w_in_bf, w_out_bf, post_w.reshape(1, d))


def _proj_kernel(x_ref, pre_ref, w_ref, ws_ref, z_ref, zs_ref, h_scr):
    j = pl.program_id(1)

    @pl.when(j == 0)
    def _():
        h = _rms(x_ref[...], pre_ref[...]).astype(BF16)
        h_scr[...] = h
        zs_ref[...] = jnp.dot(h, ws_ref[...], preferred_element_type=F32)

    z_ref[...] = jnp.dot(h_scr[...], w_ref[...], preferred_element_type=F32)


def proj(x, pre_w, w_main_bf, w_small_bf, *, layer, tm, tn):
    n, d = x.shape
    cols = w_main_bf.shape[2]
    sc = w_small_bf.shape[2]
    return pl.pallas_call(
        _proj_kernel,
        grid=(n // tm, cols // tn),
        in_specs=[
            pl.BlockSpec((tm, d), lambda i, j: (i, 0)),
            pl.BlockSpec((1, d), lambda i, j: (0, 0)),
            pl.BlockSpec((None, d, tn), lambda i, j: (layer, 0, j)),
            pl.BlockSpec((None, d, sc), lambda i, j: (layer, 0, 0)),
        ],
        out_specs=[
            pl.BlockSpec((tm, tn), lambda i, j: (i, j)),
            pl.BlockSpec((tm, sc), lambda i, j: (i, 0)),
        ],
        out_shape=[jax.ShapeDtypeStruct((n, cols), F32), jax.ShapeDtypeStruct((n, sc), F32)],
        scratch_shapes=[pltpu.VMEM((tm, d), BF16)],
        compiler_params=_cparams(("parallel", "arbitrary")),
        name="proj",
    )(x, pre_w.reshape(1, d), w_main_bf, w_small_bf)


def _dot(a, b):
    return jnp.dot(a, b, preferred_element_type=F32)


def _dot_nt(a, b):
    return lax.dot_general(a, b, (((1,), (1,)), ((), ())), preferred_element_type=F32)


def _dot_tn(a, b):
    return lax.dot_general(a, b, (((0,), (0,)), ((), ())), preferred_element_type=F32)


def _iota2(shape, dim):
    return lax.broadcasted_iota(jnp.int32, shape, dim)


def _dot_cumsum(tril, x):
    hi = x.astype(BF16)
    r1 = x - hi.astype(F32)
    mid = r1.astype(BF16)
    lo = (r1 - mid.astype(F32)).astype(BF16)
    t16 = tril.astype(BF16)
    return _dot(t16, hi) + (_dot(t16, mid) + _dot(t16, lo))


def _tri_inverse_all(lmats, c):
    ri = _iota2((c, c), 0)
    ci = _iota2((c, c), 1)
    eye = jnp.where(ri == ci, 1.0, 0.0).astype(F32)
    nb = min(16, c)
    sh = int(math.log2(nb))
    same = (ri >> sh) == (ci >> sh)
    dot1 = lambda a, b: _dot(a.astype(BF16), b.astype(BF16))
    ps = [jnp.where(same, lm, 0.0) for lm in lmats]
    ts = [eye - p for p in ps]
    k = 2
    while k < nb:
        ps = [dot1(p, p) for p in ps]
        ts = [t + dot1(t, p) for t, p in zip(ts, ps)]
        k *= 2
    blk = nb
    while blk < c:
        s1 = int(math.log2(blk))
        offm = ((ri >> (s1 + 1)) == (ci >> (s1 + 1))) & ((ri >> s1) != (ci >> s1))
        tl = [dot1(t, jnp.where(offm, lm, 0.0)) for t, lm in zip(ts, lmats)]
        ts = [t - dot1(x, t) for t, x in zip(ts, tl)]
        blk *= 2
    return ts


CONV_TAPS = 4
CONV_TOP = 8
CONV_HIST = CONV_TOP - (CONV_TAPS - 1)


def _gdn_kernel(qkv_ref, za_ref, zs_ref, cs_ref, s0_ref, cw_ref, alog_ref, dtb_ref, nw_ref,
                o_ref, sout_ref, cout_ref, xbuf, s_scr, *, c, nc, bb, heads, dk, dv):
    n = pl.program_id(1)
    nlast = pl.num_programs(1) - 1
    nqk = heads * dk
    r = nc * c

    @pl.when(n == 0)
    def _():
        xbuf[:, CONV_HIST:CONV_TOP, :] = cs_ref[...]
        s_scr[...] = s0_ref[...]

    xbuf[:, CONV_TOP:CONV_TOP + r, :] = qkv_ref[...]

    @pl.when(n == nlast)
    def _():
        cout_ref[...] = xbuf[:, CONV_HIST + r:CONV_TOP + r, :]

    ri = _iota2((c, c), 0)
    ci = _iota2((c, c), 1)
    incl = ri >= ci
    strict = ri > ci
    eye = jnp.where(ri == ci, 1.0, 0.0).astype(F32)
    rr = _iota2((r, r), 0)
    rc = _iota2((r, r), 1)
    sh = int(math.log2(c))
    blocktril = jnp.where((rr >= rc) & ((rr >> sh) == (rc >> sh)), 1.0, 0.0).astype(F32)
    nw = nw_ref[...]

    inst = []
    for bi in range(bb):
        y = xbuf[bi, CONV_HIST:CONV_HIST + r, :] * cw_ref[0:1, :]
        for j in range(1, CONV_TAPS):
            y = y + xbuf[bi, CONV_HIST + j:CONV_HIST + j + r, :] * cw_ref[j:j + 1, :]
        y = _silu(y)
        zs = zs_ref[bi]
        beta_all = _sigmoid(zs)
        g_all = -jnp.exp(alog_ref[...]) * _softplus(zs + dtb_ref[...])
        gcum_all = _dot_cumsum(blocktril, g_all)
        gcum_t = gcum_all.T
        for i in range(nc):
            rows = slice(i * c, (i + 1) * c)
            for h in range(heads):
                inst.append(dict(
                    bi=bi, i=i, h=h,
                    q=y[rows, h * dk:(h + 1) * dk],
                    k=y[rows, nqk + h * dk:nqk + (h + 1) * dk],
                    v=y[rows, 2 * nqk + h * dv:2 * nqk + (h + 1) * dv],
                    beta=beta_all[rows, h:h + 1],
                    gc=gcum_all[rows, heads + h:heads + h + 1],
                    grow=gcum_t[heads + h:heads + h + 1, rows]))
    xbuf[:, CONV_HIST:CONV_TOP, :] = xbuf[:, CONV_HIST + r:CONV_TOP + r, :]

    for d in inst:
        q, k = d["q"], d["k"]
        d["q"] = q * lax.rsqrt(jnp.sum(q * q, axis=-1, keepdims=True) + EPS) * (dk ** -0.5)
        d["k"] = k * lax.rsqrt(jnp.sum(k * k, axis=-1, keepdims=True) + EPS)
        d["k16"] = d["k"].astype(BF16)
        d["kb"] = d["k"] * d["beta"]
        d["decay"] = jnp.where(incl, jnp.exp(jnp.minimum(d["gc"] - d["grow"], 0.0)), 0.0)
    for d in inst:
        d["lmat"] = jnp.where(strict, _dot_nt(d["kb"].astype(BF16), d["k16"]) * d["decay"], 0.0)
    tinvs = _tri_inverse_all([d["lmat"] for d in inst], c)
    for d, tinv in zip(inst, tinvs):
        eg = jnp.exp(d["gc"])
        rhs = jnp.concatenate([d["v"] * d["beta"], d["kb"] * eg], axis=-1)
        sol = rhs + _dot((tinv - eye).astype(BF16), rhs.astype(BF16))
        d["u"] = sol[:, :dv]
        d["w16"] = sol[:, dv:].astype(BF16)
        d["attn16"] = (_dot_nt(d["q"].astype(BF16), d["k16"]) * d["decay"]).astype(BF16)
        d["qe16"] = (d["q"] * eg).astype(BF16)
        gl = d["gc"][c - 1:c, :]
        d["kdec16"] = (d["k"] * jnp.exp(gl - d["gc"])).astype(BF16)
        d["egl"] = jnp.exp(gl)

    state = {(bi, h): s_scr[bi, h] for bi in range(bb) for h in range(heads)}
    for i in range(nc):
        for d in inst:
            if d["i"] != i:
                continue
            bi, h = d["bi"], d["h"]
            s = state[(bi, h)]
            s16 = s.astype(BF16)
            v_new = d["u"] - _dot(d["w16"], s16)
            vn16 = v_new.astype(BF16)
            o = _dot(d["qe16"], s16) + _dot(d["attn16"], vn16)
            state[(bi, h)] = s * d["egl"] + _dot_tn(d["kdec16"], vn16)
            za = za_ref[bi, i * c:(i + 1) * c, h * dv:(h + 1) * dv]
            o_ref[bi, i * c:(i + 1) * c, h * dv:(h + 1) * dv] = (_rms(o, nw) * _silu(za)).astype(o_ref.dtype)
    for (bi, h), s in state.items():
        s_scr[bi, h] = s

    @pl.when(n == nlast)
    def _():
        sout_ref[...] = s_scr[...]


def gdn(z3, zs3, conv_state, state, conv_w, a_log, dt_bias, norm_w, *, state_layer, c, nc, bb, qkv_blk, za_blk,
        layer, depth, prev, out_dtype):
    b, t, _ = z3.shape
    _, _, heads, dk, dv = state.shape
    cch = conv_state.shape[-1]
    sc = zs3.shape[-1]
    r = nc * c
    pad = jnp.zeros((1, sc), F32)
    alog_p = lax.dynamic_update_slice(pad, a_log.reshape(1, heads).astype(F32), (0, heads))
    dtb_p = lax.dynamic_update_slice(pad, dt_bias.reshape(1, heads).astype(F32), (0, heads))
    kern = functools.partial(_gdn_kernel, c=c, nc=nc, bb=bb, heads=heads, dk=dk, dv=dv)
    prev_ops, prev_specs, aliases = _alias_plan(prev, 9, (1,))
    return pl.pallas_call(
        _drop_refs(kern, 9, len(prev_ops)),
        grid=(b // bb, t // r),
        in_specs=[
            pl.BlockSpec((bb, r, cch), lambda i, n: (i, n, qkv_blk)),
            pl.BlockSpec((bb, r, heads * dv), lambda i, n: (i, n, za_blk)),
            pl.BlockSpec((bb, r, sc), lambda i, n: (i, n, 0)),
            pl.BlockSpec((None, bb, CONV_TAPS - 1, cch), lambda i, n: (state_layer, i, 0, 0)),
            pl.BlockSpec((None, bb, heads, dk, dv), lambda i, n: (state_layer, i, 0, 0, 0)),
            pl.BlockSpec((CONV_TAPS, cch), lambda i, n: (0, 0)),
            pl.BlockSpec((1, sc), lambda i, n: (0, 0)),
            pl.BlockSpec((1, sc), lambda i, n: (0, 0)),
            pl.BlockSpec((1, dv), lambda i, n: (0, 0)),
        ] + prev_specs,
        out_specs=[
            pl.BlockSpec((bb, r, heads * dv), lambda i, n: (i, n, 0)),
            pl.BlockSpec((None, bb, heads, dk, dv), lambda i, n: (layer, i, 0, 0, 0)),
            pl.BlockSpec((bb, CONV_TAPS - 1, cch), lambda i, n: (i, 0, 0)),
        ],
        out_shape=[
            jax.ShapeDtypeStruct((b, t, heads * dv), out_dtype),
            jax.ShapeDtypeStruct((depth, b, heads, dk, dv), F32),
            jax.ShapeDtypeStruct((b, 3, cch), F32),
        ],
        scratch_shapes=[pltpu.VMEM((bb, CONV_TOP + r, cch), F32), pltpu.VMEM((bb, heads, dk, dv), F32)],
        input_output_aliases=aliases,
        compiler_params=_cparams(("parallel", "arbitrary")),
        name="gdn",
    )(z3, z3, zs3, conv_state, state, conv_w, alog_p, dtb_p, norm_w.reshape(1, dv), *prev_ops)


SB = 8


def _hgrn_kernel(q_ref, f_ref, i_ref, g_ref, lbraw_ref, s0_ref, nw_ref, o_ref, sout_ref,
                 st_scr, *, c, nc, bb, heads, dk, dv, layer):
    n = pl.program_id(1)
    nlast = pl.num_programs(1) - 1
    bh = [(bi, h) for bi in range(bb) for h in range(heads)]

    @pl.when(n == 0)
    def _():
        for bi, h in bh:
            st_scr[bi * heads + h] = s0_ref[bi, h].T

    raw = lbraw_ref[...]
    e = jnp.exp(raw - jnp.max(raw, axis=0, keepdims=True))
    sm = e / jnp.sum(e, axis=0, keepdims=True)
    lb = jnp.zeros((1, heads * dk), F32)
    for l in range(1, layer + 1):
        lb = lb + sm[l:l + 1, :]

    r = bb * nc * c
    flat = lambda ref: ref[...].reshape(r, ref.shape[-1])
    zf = flat(f_ref)
    logsig = jnp.minimum(zf, 0.0) - jnp.log1p(jnp.exp(-jnp.abs(zf)))
    la = jnp.log(lb)
    lbb = jnp.log1p(-lb) + logsig
    logf = jnp.maximum(la, lbb) + jnp.log1p(jnp.exp(-jnp.abs(la - lbb)))
    kh = (1.0 - lb) * _sigmoid(-zf)
    qh = _silu(flat(q_ref)) * (dk ** -0.5)
    vv = flat(i_ref)
    gate = _sigmoid(flat(g_ref))

    ri = _iota2((r, r), 0)
    ci = _iota2((r, r), 1)
    sh = int(math.log2(c))
    tril = jnp.where((ri >= ci) & ((ri >> sh) == (ci >> sh)), 1.0, 0.0).astype(F32)
    bcum = _dot_cumsum(tril, logf)
    fgate = jnp.exp(logf)
    nsb = c // SB
    sub = _iota2((nsb, SB, dk), 1)
    nw = nw_ref[...]
    g3 = lambda x: x.reshape(nsb, SB, x.shape[-1])
    states = {(bi, h): st_scr[bi * heads + h] for bi, h in bh}

    for ic, bi, h in [(ic, bi, h) for ic in range(nc) for bi, h in bh]:
        rows = slice((bi * nc + ic) * c, (bi * nc + ic + 1) * c)
        ks = slice(h * dk, (h + 1) * dk)
        vs = slice(h * dv, (h + 1) * dv)
        q = qh[rows, ks]
        k = kh[rows, ks]
        b = bcum[rows, ks]
        v = vv[rows, vs]
        st = states[(bi, h)]
        o = _dot_nt((q * jnp.exp(b)).astype(BF16), st.astype(BF16))
        q3, k3, v3, f3 = g3(q), g3(k), g3(v), g3(fgate[rows, ks])
        o3 = jnp.sum(q3 * k3, axis=-1, keepdims=True) * v3
        e = None
        for d in range(1, SB):
            fd = f3 if d == 1 else pltpu.roll(f3, d - 1, 1)
            e = fd if e is None else e * fd
            m = jnp.where(sub >= d, q3 * pltpu.roll(k3, d, 1) * e, 0.0)
            o3 = o3 + jnp.sum(m, axis=-1, keepdims=True) * pltpu.roll(v3, d, 1)
        o = o + o3.reshape(c, dv)
        if nsb > 1:
            b3 = g3(b)
            rend = jnp.broadcast_to(b3[:, SB - 1:SB, :], b3.shape).reshape(c, dk)
            khat = k * jnp.exp(jnp.minimum(rend - b, 0.0))
            zq = jnp.zeros((c, dk), F32)
            qs, kk = [], []
            for jb in range(nsb - 1):
                lo, hi = jb * SB, (jb + 1) * SB
                rj = b[hi - 1:hi, :]
                qpart = q[hi:, :] * jnp.exp(jnp.minimum(b[hi:, :] - rj, 0.0))
                qs.append(jnp.concatenate([zq[:hi], qpart], axis=0))
                kparts = [khat[lo:hi]] if lo == 0 else [zq[:lo], khat[lo:hi]]
                kk.append(jnp.concatenate(kparts + [zq[hi:]], axis=0))
            amat = _dot_nt(jnp.concatenate(qs, axis=-1).astype(BF16), jnp.concatenate(kk, axis=-1).astype(BF16))
            o = o + _dot(amat.astype(BF16), v.astype(BF16))
        bl = b[c - 1:c, :]
        kdec = k * jnp.exp(bl - b)
        states[(bi, h)] = st * jnp.exp(bl) + _dot_tn(v.astype(BF16), kdec.astype(BF16))
        o_ref[bi, ic * c:(ic + 1) * c, vs] = (_rms(o, nw) * gate[rows, vs]).astype(o_ref.dtype)
    for bi, h in bh:
        st_scr[bi * heads + h] = states[(bi, h)]

    @pl.when(n == nlast)
    def _():
        for bi, h in bh:
            sout_ref[bi, h] = st_scr[bi * heads + h].T


def hgrn(z3, lb_raw, state, norm_w, *, state_layer, c, nc, bb, layer, q_blk, prev, out_dtype):
    b, t, _ = z3.shape
    _, _, heads, dk, dv = state.shape
    depth = lb_raw.shape[0]
    wk = heads * dk
    assert c % SB == 0
    kern = functools.partial(_hgrn_kernel, c=c, nc=nc, bb=bb, heads=heads, dk=dk, dv=dv, layer=layer)
    r = nc * c
    zspec = lambda off: pl.BlockSpec((bb, r, wk), lambda i, n: (i, n, q_blk + off))
    prev_ops, prev_specs, aliases = _alias_plan(prev, 7, (1,))
    return pl.pallas_call(
        _drop_refs(kern, 7, len(prev_ops)),
        grid=(b // bb, t // r),
        in_specs=[
            zspec(0), zspec(1), zspec(2), zspec(3),
            pl.BlockSpec((depth, wk), lambda i, n: (0, 0)),
            pl.BlockSpec((None, bb, heads, dk, dv), lambda i, n: (state_layer, i, 0, 0, 0)),
            pl.BlockSpec((1, dv), lambda i, n: (0, 0)),
        ] + prev_specs,
        out_specs=[
            pl.BlockSpec((bb, r, heads * dv), lambda i, n: (i, n, 0)),
            pl.BlockSpec((None, bb, heads, dk, dv), lambda i, n: (layer, i, 0, 0, 0)),
        ],
        out_shape=[
            jax.ShapeDtypeStruct((b, t, heads * dv), out_dtype),
            jax.ShapeDtypeStruct((depth, b, heads, dk, dv), F32),
        ],
        scratch_shapes=[pltpu.VMEM((bb * heads, dv, dk), F32)],
        input_output_aliases=aliases,
        compiler_params=_cparams(("parallel", "arbitrary")),
        name="hgrn",
    )(z3, z3, z3, z3, lb_raw, state, norm_w.reshape(1, dv), *prev_ops)


def rope_tables(t, pos0, dh, theta):
    rd = dh // 4
    half = rd // 2
    inv = jnp.power(jnp.float32(theta), -jnp.arange(half, dtype=F32) / half)
    ang = (pos0 + jnp.arange(t)).astype(F32)[:, None] * inv
    cos, sin = jnp.cos(ang), jnp.sin(ang)
    d = jnp.arange(128) % dh
    f = d % half
    cc = jnp.where(d < rd, cos[:, f], 1.0)
    s1 = jnp.where(d < half, -sin[:, f], 0.0)
    s2 = jnp.where((d >= half) & (d < rd), sin[:, f], 0.0)
    return cc.astype(F32), s1.astype(F32), s2.astype(F32)


def _prep_kernel(q_ref, k_ref, v_ref, c_ref, s1_ref, s2_ref, q16_ref, k32_ref, k16_ref, v32_ref, v16_ref,
                 *, scale, half, heads, v_transposed):
    tm, w = q_ref.shape[1:]
    hw = w // heads
    reps = w // c_ref.shape[-1]
    cc = jnp.concatenate([c_ref[...]] * reps, axis=-1)
    s1 = jnp.concatenate([s1_ref[...]] * reps, axis=-1)
    s2 = jnp.concatenate([s2_ref[...]] * reps, axis=-1)

    def rope(x):
        return x * cc + pltpu.roll(x, w - half, 1) * s1 + pltpu.roll(x, half, 1) * s2

    q16_ref[0] = (rope(q_ref[0]) * scale).astype(BF16)
    k = rope(k_ref[0])
    k16_ref[0] = k.astype(BF16)
    v = v_ref[0]
    v16_ref[0] = (v.T if v_transposed else v).astype(BF16)
    for h in range(heads):
        k32_ref[0, pl.ds(h, tm, stride=heads), :] = k[:, h * hw:(h + 1) * hw]
        v32_ref[0, pl.ds(h, tm, stride=heads), :] = v[:, h * hw:(h + 1) * hw]


def _drop_refs(kern, start, count):
    def wrapped(*refs):
        return kern(*refs[:start], *refs[start + count:])
    return wrapped


def _alias_plan(prev, n_in, out_idx):
    prev = [] if prev is None else list(prev)
    specs = [pl.BlockSpec(memory_space=pl.ANY)] * len(prev)
    return prev, specs, {n_in + k: out_idx[k] for k in range(len(prev))}


def qkv_prep(z3, tables, *, q_blk, dh, heads, tm, v_transposed, layer, depth, prev):
    b, t, _ = z3.shape
    w = tables[0].shape[-1] * (heads * 2 * dh // tables[0].shape[-1])
    hw = w // heads
    cc, s1, s2 = tables
    kern = functools.partial(_prep_kernel, scale=dh ** -0.5 * math.log2(math.e), half=dh // 8, heads=heads,
                             v_transposed=v_transposed)
    zspec = lambda off: pl.BlockSpec((1, tm, w), lambda i, n: (i, n, q_blk + off))
    tspec = pl.BlockSpec((tm, cc.shape[-1]), lambda i, n: (n, 0))
    ospec = pl.BlockSpec((1, tm, w), lambda i, n: (i, n, 0))
    cspec = pl.BlockSpec((None, 1, tm * heads, hw), lambda i, n: (layer, i, n, 0))
    sd = lambda dt: jax.ShapeDtypeStruct((b, t, w), dt)
    sc = jax.ShapeDtypeStruct((depth, b, t * heads, hw), F32)
    vspec = pl.BlockSpec((1, w, tm), lambda i, n: (i, 0, n)) if v_transposed else ospec
    vsd = jax.ShapeDtypeStruct((b, w, t), BF16) if v_transposed else sd(BF16)
    prev_ops, prev_specs, aliases = _alias_plan(prev, 6, (1, 3))
    return pl.pallas_call(
        _drop_refs(kern, 6, len(prev_ops)),
        grid=(b, t // tm),
        in_specs=[zspec(0), zspec(1), zspec(2), tspec, tspec, tspec] + prev_specs,
        out_specs=[ospec, cspec, ospec, cspec, vspec],
        out_shape=[sd(BF16), sc, sd(BF16), sc, vsd],
        input_output_aliases=aliases,
        compiler_params=_cparams(("parallel", "parallel")),
        name="qkv_prep",
    )(z3, z3, z3, cc, s1, s2, *prev_ops)


def _lambda(lam_ref, lam_init):
    lm = lam_ref[...]
    a = jnp.sum(lm[0:1, :] * lm[1:2, :], axis=-1, keepdims=True)
    b = jnp.sum(lm[2:3, :] * lm[3:4, :], axis=-1, keepdims=True)
    return jnp.exp(a) - jnp.exp(b) + lam_init


ONES_ROWS = 16


def _attn_prompt_kernel(qt_ref, kt_ref, q_ref, k_ref, vt_ref, lam_ref, nw_ref, o_ref,
                        qs_scr, m_scr, acc_scr, *, tq, dh, dv, hpb, qchunk, lam_init):
    p = pl.program_id(2)
    qi = qt_ref[p]
    ki = kt_ref[p]

    @pl.when(ki == 0)
    def _():
        for hh in range(hpb):
            q = q_ref[0, :, hh * dv:(hh + 1) * dv]
            lane = _iota2(q.shape, 1)
            zero = jnp.zeros_like(q)
            qs_scr[hh, 0:tq, :] = jnp.where(lane < dh, q, zero)
            qs_scr[hh, tq:2 * tq, :] = jnp.where(lane >= dh, q, zero)
        m_scr[...] = jnp.full(m_scr.shape, -jnp.inf, F32)
        acc_scr[...] = jnp.zeros_like(acc_scr)

    def update(diag):
        for hh in range(hpb):
            k = k_ref[0, :, hh * dv:(hh + 1) * dv]
            vt = vt_ref[0, hh * dv:(hh + 1) * dv, :]
            vt1 = jnp.concatenate([vt, jnp.ones((ONES_ROWS, vt.shape[1]), BF16)], axis=0)
            for c0 in range(0, 2 * tq, qchunk):
                cs = slice(c0, c0 + qchunk)
                st = _dot_nt(k, qs_scr[hh, cs, :])
                if diag:
                    key = _iota2(st.shape, 0)
                    qry = (_iota2(st.shape, 1) + c0) & (tq - 1)
                    st = jnp.where(key <= qry, st, -jnp.inf)
                m_prev = m_scr[hh, :, cs]
                m_new = jnp.maximum(m_prev, jnp.max(st, axis=0, keepdims=True))
                alpha = jnp.exp2(m_prev - m_new)
                pr = jnp.exp2(st - m_new).astype(BF16)
                acc_scr[hh, :, cs] = alpha * acc_scr[hh, :, cs] + _dot(vt1, pr)
                m_scr[hh, :, cs] = m_new

    @pl.when(ki < qi)
    def _():
        update(False)

    @pl.when(ki == qi)
    def _():
        update(True)
        lam = _lambda(lam_ref, lam_init)
        for hh in range(hpb):
            o1 = acc_scr[hh, 0:dv, 0:tq] / acc_scr[hh, dv:dv + 1, 0:tq]
            o2 = acc_scr[hh, 0:dv, tq:2 * tq] / acc_scr[hh, dv:dv + 1, tq:2 * tq]
            o = (o1 - lam * o2).T
            o_ref[0, :, hh * dv:(hh + 1) * dv] = (_rms(o, nw_ref[...]) * (1.0 - lam_init)).astype(o_ref.dtype)


def attn_prompt(q16, k16, vt16, lam, norm_w, *, tq, dh, lam_init):
    b, t, w = q16.shape
    dv = norm_w.shape[-1]
    heads = w // dv
    nq = t // tq
    pairs = [(i, j) for i in range(nq) for j in range(i + 1)]
    qt = jnp.asarray([pq for pq, _ in pairs], jnp.int32)
    kt = jnp.asarray([pk for _, pk in pairs], jnp.int32)
    hpb = math.gcd(heads, ATTN_HEADS_PER_STEP)
    kern = functools.partial(_attn_prompt_kernel, tq=tq, dh=dh, dv=dv, hpb=hpb, qchunk=min(ATTN_QCHUNK, 2 * tq),
                             lam_init=lam_init)
    grid_spec = pltpu.PrefetchScalarGridSpec(
        num_scalar_prefetch=2,
        grid=(b, heads // hpb, len(pairs)),
        in_specs=[
            pl.BlockSpec((1, tq, hpb * dv), lambda i, h, p, qt, kt: (i, qt[p], h)),
            pl.BlockSpec((1, tq, hpb * dv), lambda i, h, p, qt, kt: (i, kt[p], h)),
            pl.BlockSpec((1, hpb * dv, tq), lambda i, h, p, qt, kt: (i, h, kt[p])),
            pl.BlockSpec(lam.shape, lambda i, h, p, qt, kt: (0, 0)),
            pl.BlockSpec((1, dv), lambda i, h, p, qt, kt: (0, 0)),
        ],
        out_specs=pl.BlockSpec((1, tq, hpb * dv), lambda i, h, p, qt, kt: (i, qt[p], h)),
        scratch_shapes=[
            pltpu.VMEM((hpb, 2 * tq, dv), BF16),
            pltpu.VMEM((hpb, 1, 2 * tq), F32),
            pltpu.VMEM((hpb, dv + ONES_ROWS, 2 * tq), F32),
        ],
    )
    return pl.pallas_call(
        kern,
        grid_spec=grid_spec,
        out_shape=jax.ShapeDtypeStruct((b, t, w), BF16),
        compiler_params=_cparams(("parallel", "parallel", "arbitrary")),
        name="attn_prompt",
    )(qt, kt, q16, k16, vt16, lam, norm_w.reshape(1, dv))


def _attn_sample_kernel(pt_ref, q_ref, kc_ref, vc_ref, lam_ref, nw_ref, *rest,
                        npages, bpb, tq, heads, dh, dv, page, lam_init):
    k_refs = rest[:bpb * npages]
    v_refs = rest[bpb * npages:2 * bpb * npages]
    o_ref = rest[2 * bpb * npages]
    lam = _lambda(lam_ref, lam_init)
    nw = nw_ref[...]
    lane = _iota2((tq, dv), 1)
    qidx = _iota2((2 * tq, tq), 0) % tq
    kidx = _iota2((2 * tq, tq), 1)
    hsl = [slice(h * dv, (h + 1) * dv) for h in range(heads)]
    inst = [(bi, h) for bi in range(bpb) for h in range(heads)]

    def head_rows(refs, bi, h):
        return jnp.concatenate([r[0, 0, pl.ds(h, page, stride=heads), :].astype(BF16)
                                for r in refs[bi * npages:(bi + 1) * npages]], axis=0)

    qs = [q_ref[bi].astype(F32) for bi in range(bpb)]
    qrows = [jnp.concatenate([jnp.where(lane < dh, qs[bi][:, hsl[h]], 0.0), jnp.where(lane >= dh, qs[bi][:, hsl[h]], 0.0)],
                             axis=0).astype(BF16) for bi, h in inst]
    s_past = [_dot_nt(qr, head_rows(k_refs, bi, h)) for qr, (bi, h) in zip(qrows, inst)]
    s_cur = [jnp.where(kidx <= qidx, _dot_nt(qr, kc_ref[bi, :, hsl[h]]), -jnp.inf) for qr, (bi, h) in zip(qrows, inst)]
    ms = [jnp.maximum(jnp.max(sp, axis=-1, keepdims=True), jnp.max(sc, axis=-1, keepdims=True))
          for sp, sc in zip(s_past, s_cur)]
    p_past = [jnp.exp2(sp - m) for sp, m in zip(s_past, ms)]
    p_cur = [jnp.exp2(sc - m) for sc, m in zip(s_cur, ms)]
    invs = [1.0 / (jnp.sum(pp, axis=-1, keepdims=True) + jnp.sum(pc, axis=-1, keepdims=True))
            for pp, pc in zip(p_past, p_cur)]

    def diff(pp, inv):
        pn = pp * inv
        return pn[0:tq] - lam * pn[tq:2 * tq]

    for j, (bi, h) in enumerate(inst):
        o = (_dot(diff(p_past[j], invs[j]).astype(BF16), head_rows(v_refs, bi, h))
             + _dot(diff(p_cur[j], invs[j]), vc_ref[bi, :, hsl[h]].astype(F32)))
        o_ref[bi, :, hsl[h]] = _rms(o, nw) * (1.0 - lam_init)


def attn_sample(q16, k16, v16, cache_k, cache_v, page_table, lam, norm_w, *, layer, heads, dh, lam_init):
    b, tq, w = q16.shape
    dv = norm_w.shape[-1]
    npages = page_table.shape[1]
    prow = cache_k.shape[2]
    bpb = math.gcd(b, SAMPLE_ROWS_PER_STEP)
    kern = functools.partial(_attn_sample_kernel, npages=npages, bpb=bpb, tq=tq, heads=heads, dh=dh, dv=dv,
                             page=prow // heads, lam_init=lam_init)
    cur = pl.BlockSpec((bpb, tq, w), lambda i, pt: (i, 0, 0))

    def page_spec(bi, j):
        return pl.BlockSpec((1, 1, prow, dv), lambda i, pt: (layer, pt[i * bpb + bi, j], 0, 0))

    pages = [page_spec(bi, j) for bi in range(bpb) for j in range(npages)]
    grid_spec = pltpu.PrefetchScalarGridSpec(
        num_scalar_prefetch=1,
        grid=(b // bpb,),
        in_specs=[cur, cur, cur,
                  pl.BlockSpec(lam.shape, lambda i, pt: (0, 0)),
                  pl.BlockSpec((1, dv), lambda i, pt: (0, 0))]
        + pages * 2,
        out_specs=pl.BlockSpec((bpb, tq, w), lambda i, pt: (i, 0, 0)),
    )
    return pl.pallas_call(
        kern,
        grid_spec=grid_spec,
        out_shape=jax.ShapeDtypeStruct((b, tq, w), F32),
        compiler_params=_cparams(("parallel",)),
        name="attn_sample",
    )(page_table, q16, k16, v16, lam, norm_w.reshape(1, dv), *([cache_k] * len(pages)), *([cache_v] * len(pages)))


def _merge_ffn_kernel(x_ref, oa_ref, ob_ref, oc_ref, g0_ref, g1_ref, g2_ref, wb_ref, wo_ref, mpost_ref,
                      pre_ref, wi_ref, wo2_ref, post_ref, o_ref, a_scr, *, tf):
    y = None
    for i, (o_i, g_i) in enumerate(((oa_ref, g0_ref), (ob_ref, g1_ref), (oc_ref, g2_ref))):
        ys = _dot(o_i[...].astype(BF16), wb_ref[i])
        t = _sigmoid(g_i[...]) * ys
        y = t if y is None else y + t
    y2 = _dot(y.astype(BF16), wo_ref[...])
    x1 = x_ref[...] + _rms(y2, mpost_ref[...])
    o_ref[...] = _ffn_body(x1, pre_ref, wi_ref, wo2_ref, post_ref, a_scr, tf)


def merge_ffn(x, o_a, o_b, o_c, z, w_branch_bf, w_out_bf, mpost_w, pre_w, w_in_bf, w_out2_bf, post_w, *,
              layer, tm, tf, gate_blk):
    n, d = x.shape
    bw = o_a.shape[-1]
    nb = w_branch_bf.shape[1]
    dff = w_out2_bf.shape[1]
    resident = dict(pipeline_mode=pl.Buffered(1))
    ospec = pl.BlockSpec((tm, bw), lambda i: (i, 0))
    gspec = lambda k: pl.BlockSpec((tm, d), lambda i: (i, gate_blk + k))
    vec = pl.BlockSpec((1, d), lambda i: (0, 0))
    return pl.pallas_call(
        functools.partial(_merge_ffn_kernel, tf=tf),
        grid=(n // tm,),
        in_specs=[
            pl.BlockSpec((tm, d), lambda i: (i, 0)),
            ospec, ospec, ospec, gspec(0), gspec(1), gspec(2),
            pl.BlockSpec((None, nb, bw, d), lambda i: (layer, 0, 0, 0), **resident),
            pl.BlockSpec((None, d, d), lambda i: (layer, 0, 0), **resident),
            vec, vec,
            pl.BlockSpec((None, d, 2 * dff), lambda i: (layer, 0, 0), **resident),
            pl.BlockSpec((None, dff, d), lambda i: (layer, 0, 0), **resident),
            vec,
        ],
        out_specs=pl.BlockSpec((tm, d), lambda i: (i, 0)),
        out_shape=jax.ShapeDtypeStruct((n, d), F32),
        scratch_shapes=[pltpu.VMEM((tm, dff), BF16)],
        compiler_params=_cparams(("parallel",)),
        name="merge_ffn",
    )(x, o_a, o_b, o_c, z, z, z, w_branch_bf, w_out_bf, mpost_w.reshape(1, d), pre_w.reshape(1, d), w_in_bf, w_out2_bf,
      post_w.reshape(1, d))


ROPE_THETA = 500000.0
SCAN_CHUNK = 64
GDN_CHUNKS_PER_STEP = 4
HGRN_CHUNKS_PER_STEP = 4
HGRN_INSTANCES = 32
GDN_INSTANCES = 32
ROW_TILE = 1024
FFN_ROW_TILE = 512
FF_TILE = 256
PROJ_COL_TILES = 4
ATTN_TILE = 512
ATTN_QCHUNK = 1024
ATTN_HEADS_PER_STEP = 4
PREP_TILE = 512
SAMPLE_ROWS_PER_STEP = 2
SMALL_COLS = 128


def _split_w_in(w_in, sizes):
    offs = [0]
    for s in sizes:
        offs.append(offs[-1] + s)
    seg = lambda i: w_in[:, :, offs[i]:offs[i + 1]]
    main = jnp.concatenate([seg(11), seg(0), seg(1), seg(4), seg(5), seg(6), seg(7), seg(8), seg(9), seg(10)], axis=2)
    small = jnp.concatenate([seg(2), seg(3)], axis=2)
    small = jnp.pad(small, ((0, 0), (0, 0), (0, SMALL_COLS - small.shape[2])))
    return main.astype(BF16), small.astype(BF16)


def _row_tile(n, pref):
    return pref if n % pref == 0 else n


def kernel(x_prompt, x_sample, state_gdn, state_gdn_conv, state_hgrn, cache_k, cache_v, page_table,
           ffn1_norm_pre, ffn1_norm_post, ffn1_w_in, ffn1_w_out, mix_norm_pre, mix_norm_post, w_in,
           gdn_conv_w, gdn_a_log, gdn_dt_bias, gdn_norm_w, hgrn_lb_raw, hgrn_norm_w, diff_lambda,
           diff_norm_w, w_branch, w_out, ffn2_norm_pre, ffn2_norm_post, ffn2_w_in, ffn2_w_out):
    depth = w_in.shape[0]
    bp, tp, d = x_prompt.shape
    bs, ts, _ = x_sample.shape
    _, _, gh, gdk, gdv = state_gdn.shape
    cch = state_gdn_conv.shape[-1]
    _, _, hh, hdk, hdv = state_hgrn.shape
    _, n_pool, page, ah, adh2 = cache_k.shape
    adh = adh2 // 2
    adv = cache_v.shape[-1]
    aw = ah * adv
    assert ah * adh2 == aw and hh * hdk == aw and hh * hdv == aw and gh * gdv == aw and cch == 3 * aw and d == 2 * aw
    sizes = (cch, gh * gdv, gh, gh, hh * hdk, hh * hdk, hh * hdv, hh * hdv, ah * adh2, ah * adh2, aw, 3 * d)
    gate_blk, qkv_blk, za_blk, hq_blk, aq_blk = 0, (3 * d) // cch, (3 * d + cch) // aw, (3 * d + cch) // aw + 1, (3 * d + cch) // aw + 5
    past_len = page_table.shape[1] * page
    ck = cache_k.reshape(depth, n_pool, page * ah, adh2)
    cv = cache_v.reshape(depth, n_pool, page * ah, adv)
    tabs_p = rope_tables(tp, 0, adh, ROPE_THETA)
    tabs_s = tuple(jnp.tile(a, (bs, 1)) for a in rope_tables(ts, past_len, adh, ROPE_THETA))
    assert gdn_conv_w.shape[1] == CONV_TAPS and state_gdn_conv.shape[2] == CONV_TAPS - 1
    zeros_conv = jnp.zeros((1, bp, CONV_TAPS - 1, cch), F32)
    zeros_gdn = jnp.zeros((1, bp, gh, gdk, gdv), F32)
    zeros_hgrn = jnp.zeros((1, bp, hh, hdk, hdv), F32)

    f1_in, f1_out = ffn1_w_in.astype(BF16), ffn1_w_out.astype(BF16)
    f2_in, f2_out = ffn2_w_in.astype(BF16), ffn2_w_out.astype(BF16)
    wb, wo = w_branch.astype(BF16), w_out.astype(BF16)

    def run_layer(l, x, b, t, conv_state, gdn_state, hgrn_state, sl, tabs, prompt, w_main, w_small, acc):
        n = b * t
        tm = _row_tile(n, ROW_TILE)
        lam_init = 0.8 - 0.6 * math.exp(-0.3 * l)
        tmf = _row_tile(n, FFN_ROW_TILE)
        x = ffn(x, ffn1_norm_pre[l], f1_in, f1_out, ffn1_norm_post[l], layer=l, tm=tmf, tf=FF_TILE)
        z, zs = proj(x, mix_norm_pre[l], w_main, w_small, layer=l, tm=tm, tn=w_main.shape[2] // PROJ_COL_TILES)
        z3 = z.reshape(b, t, z.shape[-1])
        zs3 = zs.reshape(b, t, SMALL_COLS)
        c = math.gcd(SCAN_CHUNK, t)
        branch_dtype = BF16 if c % 16 == 0 else F32
        o_a, new_gdn, new_conv = gdn(z3, zs3, conv_state, gdn_state, gdn_conv_w[l], gdn_a_log[l], gdn_dt_bias[l],
                                     gdn_norm_w[l], state_layer=sl, c=c, nc=min(GDN_CHUNKS_PER_STEP, t // c),
                                     bb=math.gcd(b, max(1, GDN_INSTANCES // (gh * min(GDN_CHUNKS_PER_STEP, t // c)))),
                                     qkv_blk=qkv_blk, za_blk=za_blk, layer=l, depth=depth,
                                     prev=None if acc is None else acc[0:1], out_dtype=branch_dtype)
        o_b, new_hgrn = hgrn(z3, hgrn_lb_raw, hgrn_state, hgrn_norm_w[l], state_layer=sl, c=c,
                             nc=min(HGRN_CHUNKS_PER_STEP, t // c),
                             bb=math.gcd(b, max(1, HGRN_INSTANCES // (hh * min(HGRN_CHUNKS_PER_STEP, t // c)))),
                             layer=l, q_blk=hq_blk,
                             prev=None if acc is None else acc[1:2], out_dtype=branch_dtype)
        if prompt:
            q16, k32, k16, v32, v16 = qkv_prep(z3, tabs, q_blk=aq_blk, dh=adh, heads=ah, tm=_row_tile(t, PREP_TILE),
                                               v_transposed=True, layer=l, depth=depth,
                                               prev=None if acc is None else acc[2:4])
        else:
            q16, k32, k16, v32, v16 = qkv_prep(z.reshape(1, n, z.shape[-1]), tabs, q_blk=aq_blk, dh=adh, heads=ah,
                                               tm=_row_tile(n, PREP_TILE), v_transposed=False, layer=l, depth=depth,
                                               prev=None if acc is None else acc[2:4])
            q16, k16, v16 = (a.reshape(b, t, aw) for a in (q16, k16, v16))
        if prompt:
            o_c = attn_prompt(q16, k16, v16, diff_lambda[l], diff_norm_w[l], tq=_row_tile(t, ATTN_TILE), dh=adh,
                              lam_init=lam_init)
        else:
            o_c = attn_sample(q16, k16, v16, ck, cv, page_table, diff_lambda[l], diff_norm_w[l], layer=l, heads=ah, dh=adh,
                              lam_init=lam_init)
        x = merge_ffn(x, o_a.reshape(n, aw), o_b.reshape(n, aw), o_c.reshape(n, aw), z, wb, wo, mix_norm_post[l],
                      ffn2_norm_pre[l], f2_in, f2_out, ffn2_norm_post[l], layer=l, tm=tmf, tf=FF_TILE, gate_blk=gate_blk)
        return x, new_conv, (new_gdn, new_hgrn, k32, v32)

    xp = x_prompt.reshape(bp * tp, d)
    xs = x_sample.reshape(bs * ts, d)
    conv_p, conv_s, acc_p, acc_s = [], [], None, None
    w_main, w_small = _split_w_in(w_in, sizes)
    for l in range(depth):
        xp, cp, acc_p = run_layer(l, xp, bp, tp, zeros_conv, zeros_gdn, zeros_hgrn, 0, tabs_p, True, w_main, w_small, acc_p)
        conv_p.append(cp)
        xs, cs, acc_s = run_layer(l, xs, bs, ts, state_gdn_conv, state_gdn, state_hgrn, l, tabs_s, False, w_main, w_small,
                                  acc_s)
        conv_s.append(cs)

    def finish(x, b, t, convs, acc):
        new_gdn, new_hgrn, k32, v32 = acc
        return (x.reshape(b, t, d), new_gdn, jnp.stack(convs), new_hgrn,
                k32.reshape(depth, b, t, ah, adh2), v32.reshape(depth, b, t, ah, adv))

    yp, p_gdn, p_conv, p_hgrn, p_k, p_v = finish(xp, bp, tp, conv_p, acc_p)
    ys, s_gdn, s_conv, s_hgrn, s_k, s_v = finish(xs, bs, ts, conv_s, acc_s)
    return (yp, ys, p_gdn, p_conv, p_hgrn, p_k, p_v, s_gdn, s_conv, s_hgrn, s_k, s_v)
```

```python
import functools
import math

import jax
import jax.numpy as jnp
from jax import lax
from jax.experimental import pallas as pl
from jax.experimental.pallas import tpu as pltpu

F32 = jnp.float32
BF16 = jnp.bfloat16
EPS = 1e-6

VMEM_LIMIT_BYTES = 56 * 1024 * 1024


def _cparams(sem):
    return pltpu.CompilerParams(dimension_semantics=sem, vmem_limit_bytes=VMEM_LIMIT_BYTES)


def _rms(x, w):
    return x * lax.rsqrt(jnp.mean(x * x, axis=-1, keepdims=True) + EPS) * w


def _sigmoid(x):
    return 1.0 / (1.0 + jnp.exp(-x))


def _silu(x):
    return x * _sigmoid(x)


def _softplus(x):
    return jnp.maximum(x, 0.0) + jnp.log1p(jnp.exp(-jnp.abs(x)))


def _ffn_body(x, pre_ref, wi_ref, wo_ref, post_ref, a_scr, tf):
    dff = wo_ref.shape[0]
    h = _rms(x, pre_ref[...]).astype(BF16)
    for j in range(dff // tf):
        cols = slice(j * tf, (j + 1) * tf)
        g = _dot(h, wi_ref[:, cols])
        u = _dot(h, wi_ref[:, dff + j * tf:dff + (j + 1) * tf])
        a_scr[:, cols] = (_silu(g) * u).astype(BF16)
    f = _dot(a_scr[...], wo_ref[...])
    return x + 0.5 * _rms(f, post_ref[...])


def _ffn_kernel(x_ref, pre_ref, wi_ref, wo_ref, post_ref, o_ref, a_scr, *, tf):
    o_ref[...] = _ffn_body(x_ref[...], pre_ref, wi_ref, wo_ref, post_ref, a_scr, tf)


def ffn(x, pre_w, w_in_bf, w_out_bf, post_w, *, layer, tm, tf):
    n, d = x.shape
    dff = w_out_bf.shape[1]
    resident = dict(pipeline_mode=pl.Buffered(1))
    return pl.pallas_call(
        functools.partial(_ffn_kernel, tf=tf),
        grid=(n // tm,),
        in_specs=[
            pl.BlockSpec((tm, d), lambda i: (i, 0)),
            pl.BlockSpec((1, d), lambda i: (0, 0)),
            pl.BlockSpec((None, d, 2 * dff), lambda i: (layer, 0, 0), **resident),
            pl.BlockSpec((None, dff, d), lambda i: (layer, 0, 0), **resident),
            pl.BlockSpec((1, d), lambda i: (0, 0)),
        ],
        out_specs=pl.BlockSpec((tm, d), lambda i: (i, 0)),
        out_shape=jax.ShapeDtypeStruct((n, d), F32),
        scratch_shapes=[pltpu.VMEM((tm, dff), BF16)],
        compiler_params=_cparams(("parallel",)),
        name="ffn",
    )(x, pre_w.reshape(1, d), w_in_bf, w_out_bf, post_w.reshape(1, d))


def _proj_kernel(x_ref, pre_ref, w_ref, ws_ref, z_ref, zs_ref, h_scr):
    j = pl.program_id(1)

    @pl.when(j == 0)
    def _():
        h = _rms(x_ref[...], pre_ref[...]).astype(BF16)
        h_scr[...] = h
        zs_ref[...] = jnp.dot(h, ws_ref[...], preferred_element_type=F32)

    z_ref[...] = jnp.dot(h_scr[...], w_ref[...], preferred_element_type=F32)


def proj(x, pre_w, w_main_bf, w_small_bf, *, layer, tm, tn):
    n, d = x.shape
    cols = w_main_bf.shape[2]
    sc = w_small_bf.shape[2]
    return pl.pallas_call(
        _proj_kernel,
        grid=(n // tm, cols // tn),
        in_specs=[
            pl.BlockSpec((tm, d), lambda i, j: (i, 0)),
            pl.BlockSpec((1, d), lambda i, j: (0, 0)),
            pl.BlockSpec((None, d, tn), lambda i, j: (layer, 0, j)),
            pl.BlockSpec((None, d, sc), lambda i, j: (layer, 0, 0)),
        ],
        out_specs=[
            pl.BlockSpec((tm, tn), lambda i, j: (i, j)),
            pl.BlockSpec((tm, sc), lambda i, j: (i, 0)),
        ],
        out_shape=[jax.ShapeDtypeStruct((n, cols), F32), jax.ShapeDtypeStruct((n, sc), F32)],
        scratch_shapes=[pltpu.VMEM((tm, d), BF16)],
        compiler_params=_cparams(("parallel", "arbitrary")),
        name="proj",
    )(x, pre_w.reshape(1, d), w_main_bf, w_small_bf)


def _dot(a, b):
    return jnp.dot(a, b, preferred_element_type=F32)


def _dot_nt(a, b):
    return lax.dot_general(a, b, (((1,), (1,)), ((), ())), preferred_element_type=F32)


def _dot_tn(a, b):
    return lax.dot_general(a, b, (((0,), (0,)), ((), ())), preferred_element_type=F32)


def _iota2(shape, dim):
    return lax.broadcasted_iota(jnp.int32, shape, dim)


def _dot_cumsum(tril, x):
    hi = x.astype(BF16)
    r1 = x - hi.astype(F32)
    mid = r1.astype(BF16)
    lo = (r1 - mid.astype(F32)).astype(BF16)
    t16 = tril.astype(BF16)
    return _dot(t16, hi) + (_dot(t16, mid) + _dot(t16, lo))


def _tri_inverse_all(lmats, c):
    ri = _iota2((c, c), 0)
    ci = _iota2((c, c), 1)
    eye = jnp.where(ri == ci, 1.0, 0.0).astype(F32)
    nb = min(16, c)
    sh = int(math.log2(nb))
    same = (ri >> sh) == (ci >> sh)
    dot1 = lambda a, b: _dot(a.astype(BF16), b.astype(BF16))
    ps = [jnp.where(same, lm, 0.0) for lm in lmats]
    ts = [eye - p for p in ps]
    k = 2
    while k < nb:
        ps = [dot1(p, p) for p in ps]
        ts = [t + dot1(t, p) for t, p in zip(ts, ps)]
        k *= 2
    blk = nb
    while blk < c:
        s1 = int(math.log2(blk))
        offm = ((ri >> (s1 + 1)) == (ci >> (s1 + 1))) & ((ri >> s1) != (ci >> s1))
        tl = [dot1(t, jnp.where(offm, lm, 0.0)) for t, lm in zip(ts, lmats)]
        ts = [t - dot1(x, t) for t, x in zip(ts, tl)]
        blk *= 2
    return ts


CONV_TAPS = 4
CONV_TOP = 8
CONV_HIST = CONV_TOP - (CONV_TAPS - 1)


def _gdn_kernel(qkv_ref, za_ref, zs_ref, cs_ref, s0_ref, cw_ref, alog_ref, dtb_ref, nw_ref,
                o_ref, sout_ref, cout_ref, xbuf, s_scr, *, c, nc, bb, heads, dk, dv):
    n = pl.program_id(1)
    nlast = pl.num_programs(1) - 1
    nqk = heads * dk
    r = nc * c

    @pl.when(n == 0)
    def _():
        xbuf[:, CONV_HIST:CONV_TOP, :] = cs_ref[...]
        s_scr[...] = s0_ref[...]

    xbuf[:, CONV_TOP:CONV_TOP + r, :] = qkv_ref[...]

    @pl.when(n == nlast)
    def _():
        cout_ref[...] = xbuf[:, CONV_HIST + r:CONV_TOP + r, :]

    ri = _iota2((c, c), 0)
    ci = _iota2((c, c), 1)
    incl = ri >= ci
    strict = ri > ci
    eye = jnp.where(ri == ci, 1.0, 0.0).astype(F32)
    rr = _iota2((r, r), 0)
    rc = _iota2((r, r), 1)
    sh = int(math.log2(c))
    blocktril = jnp.where((rr >= rc) & ((rr >> sh) == (rc >> sh)), 1.0, 0.0).astype(F32)
    nw = nw_ref[...]

    inst = []
    for bi in range(bb):
        y = xbuf[bi, CONV_HIST:CONV_HIST + r, :] * cw_ref[0:1, :]
        for j in range(1, CONV_TAPS):
            y = y + xbuf[bi, CONV_HIST + j:CONV_HIST + j + r, :] * cw_ref[j:j + 1, :]
        y = _silu(y)
        zs = zs_ref[bi]
        beta_all = _sigmoid(zs)
        g_all = -jnp.exp(alog_ref[...]) * _softplus(zs + dtb_ref[...])
        gcum_all = _dot_cumsum(blocktril, g_all)
        gcum_t = gcum_all.T
        for i in range(nc):
            rows = slice(i * c, (i + 1) * c)
            for h in range(heads):
                inst.append(dict(
                    bi=bi, i=i, h=h,
                    q=y[rows, h * dk:(h + 1) * dk],
                    k=y[rows, nqk + h * dk:nqk + (h + 1) * dk],
                    v=y[rows, 2 * nqk + h * dv:2 * nqk + (h + 1) * dv],
                    beta=beta_all[rows, h:h + 1],
                    gc=gcum_all[rows, heads + h:heads + h + 1],
                    grow=gcum_t[heads + h:heads + h + 1, rows]))
    xbuf[:, CONV_HIST:CONV_TOP, :] = xbuf[:, CONV_HIST + r:CONV_TOP + r, :]

    for d in inst:
        q, k = d["q"], d["k"]
        d["q"] = q * lax.rsqrt(jnp.sum(q * q, axis=-1, keepdims=True) + EPS) * (dk ** -0.5)
        d["k"] = k * lax.rsqrt(jnp.sum(k * k, axis=-1, keepdims=True) + EPS)
        d["k16"] = d["k"].astype(BF16)
        d["kb"] = d["k"] * d["beta"]
        d["decay"] = jnp.where(incl, jnp.exp(jnp.minimum(d["gc"] - d["grow"], 0.0)), 0.0)
    for d in inst:
        d["lmat"] = jnp.where(strict, _dot_nt(d["kb"].astype(BF16), d["k16"]) * d["decay"], 0.0)
    tinvs = _tri_inverse_all([d["lmat"] for d in inst], c)
    for d, tinv in zip(inst, tinvs):
        eg = jnp.exp(d["gc"])
        rhs = jnp.concatenate([d["v"] * d["beta"], d["kb"] * eg], axis=-1)
        sol = rhs + _dot((tinv - eye).astype(BF16), rhs.astype(BF16))
        d["u"] = sol[:, :dv]
        d["w16"] = sol[:, dv:].astype(BF16)
        d["attn16"] = (_dot_nt(d["q"].astype(BF16), d["k16"]) * d["decay"]).astype(BF16)
        d["qe16"] = (d["q"] * eg).astype(BF16)
        gl = d["gc"][c - 1:c, :]
        d["kdec16"] = (d["k"] * jnp.exp(gl - d["gc"])).astype(BF16)
        d["egl"] = jnp.exp(gl)

    state = {(bi, h): s_scr[bi, h] for bi in range(bb) for h in range(heads)}
    for i in range(nc):
        for d in inst:
            if d["i"] != i:
                continue
            bi, h = d["bi"], d["h"]
            s = state[(bi, h)]
            s16 = s.astype(BF16)
            v_new = d["u"] - _dot(d["w16"], s16)
            vn16 = v_new.astype(BF16)
            o = _dot(d["qe16"], s16) + _dot(d["attn16"], vn16)
            state[(bi, h)] = s * d["egl"] + _dot_tn(d["kdec16"], vn16)
            za = za_ref[bi, i * c:(i + 1) * c, h * dv:(h + 1) * dv]
            o_ref[bi, i * c:(i + 1) * c, h * dv:(h + 1) * dv] = (_rms(o, nw) * _silu(za)).astype(o_ref.dtype)
    for (bi, h), s in state.items():
        s_scr[bi, h] = s

    @pl.when(n == nlast)
    def _():
        sout_ref[...] = s_scr[...]


def gdn(z3, zs3, conv_state, state, conv_w, a_log, dt_bias, norm_w, *, state_layer, c, nc, bb, qkv_blk, za_blk,
        layer, depth, prev, out_dtype):
    b, t, _ = z3.shape
    _, _, heads, dk, dv = state.shape
    cch = conv_state.shape[-1]
    sc = zs3.shape[-1]
    r = nc * c
    pad = jnp.zeros((1, sc), F32)
    alog_p = lax.dynamic_update_slice(pad, a_log.reshape(1, heads).astype(F32), (0, heads))
    dtb_p = lax.dynamic_update_slice(pad, dt_bias.reshape(1, heads).astype(F32), (0, heads))
    kern = functools.partial(_gdn_kernel, c=c, nc=nc, bb=bb, heads=heads, dk=dk, dv=dv)
    prev_ops, prev_specs, aliases = _alias_plan(prev, 9, (1,))
    return pl.pallas_call(
        _drop_refs(kern, 9, len(prev_ops)),
        grid=(b // bb, t // r),
        in_specs=[
            pl.BlockSpec((bb, r, cch), lambda i, n: (i, n, qkv_blk)),
            pl.BlockSpec((bb, r, heads * dv), lambda i, n: (i, n, za_blk)),
            pl.BlockSpec((bb, r, sc), lambda i, n: (i, n, 0)),
            pl.BlockSpec((None, bb, CONV_TAPS - 1, cch), lambda i, n: (state_layer, i, 0, 0)),
            pl.BlockSpec((None, bb, heads, dk, dv), lambda i, n: (state_layer, i, 0, 0, 0)),
            pl.BlockSpec((CONV_TAPS, cch), lambda i, n: (0, 0)),
            pl.BlockSpec((1, sc), lambda i, n: (0, 0)),
            pl.BlockSpec((1, sc), lambda i, n: (0, 0)),
            pl.BlockSpec((1, dv), lambda i, n: (0, 0)),
        ] + prev_specs,
        out_specs=[
            pl.BlockSpec((bb, r, heads * dv), lambda i, n: (i, n, 0)),
            pl.BlockSpec((None, bb, heads, dk, dv), lambda i, n: (layer, i, 0, 0, 0)),
            pl.BlockSpec((bb, CONV_TAPS - 1, cch), lambda i, n: (i, 0, 0)),
        ],
        out_shape=[
            jax.ShapeDtypeStruct((b, t, heads * dv), out_dtype),
            jax.ShapeDtypeStruct((depth, b, heads, dk, dv), F32),
            jax.ShapeDtypeStruct((b, 3, cch), F32),
        ],
        scratch_shapes=[pltpu.VMEM((bb, CONV_TOP + r, cch), F32), pltpu.VMEM((bb, heads, dk, dv), F32)],
        input_output_aliases=aliases,
        compiler_params=_cparams(("parallel", "arbitrary")),
        name="gdn",
    )(z3, z3, zs3, conv_state, state, conv_w, alog_p, dtb_p, norm_w.reshape(1, dv), *prev_ops)


SB = 8


def _hgrn_kernel(q_ref, f_ref, i_ref, g_ref, lbraw_ref, s0_ref, nw_ref, o_ref, sout_ref,
                 st_scr, *, c, nc, bb, heads, dk, dv, layer):
    n = pl.program_id(1)
    nlast = pl.num_programs(1) - 1
    bh = [(bi, h) for bi in range(bb) for h in range(heads)]

    @pl.when(n == 0)
    def _():
        for bi, h in bh:
            st_scr[bi * heads + h] = s0_ref[bi, h].T

    raw = lbraw_ref[...]
    e = jnp.exp(raw - jnp.max(raw, axis=0, keepdims=True))
    sm = e / jnp.sum(e, axis=0, keepdims=True)
    lb = jnp.zeros((1, heads * dk), F32)
    for l in range(1, layer + 1):
        lb = lb + sm[l:l + 1, :]

    r = bb * nc * c
    flat = lambda ref: ref[...].reshape(r, ref.shape[-1])
    zf = flat(f_ref)
    logsig = jnp.minimum(zf, 0.0) - jnp.log1p(jnp.exp(-jnp.abs(zf)))
    la = jnp.log(lb)
    lbb = jnp.log1p(-lb) + logsig
    logf = jnp.maximum(la, lbb) + jnp.log1p(jnp.exp(-jnp.abs(la - lbb)))
    kh = (1.0 - lb) * _sigmoid(-zf)
    qh = _silu(flat(q_ref)) * (dk ** -0.5)
    vv = flat(i_ref)
    gate = _sigmoid(flat(g_ref))

    ri = _iota2((r, r), 0)
    ci = _iota2((r, r), 1)
    sh = int(math.log2(c))
    tril = jnp.where((ri >= ci) & ((ri >> sh) == (ci >> sh)), 1.0, 0.0).astype(F32)
    bcum = _dot_cumsum(tril, logf)
    fgate = jnp.exp(logf)
    nsb = c // SB
    sub = _iota2((nsb, SB, dk), 1)
    nw = nw_ref[...]
    g3 = lambda x: x.reshape(nsb, SB, x.shape[-1])
    states = {(bi, h): st_scr[bi * heads + h] for bi, h in bh}

    for ic, bi, h in [(ic, bi, h) for ic in range(nc) for bi, h in bh]:
        rows = slice((bi * nc + ic) * c, (bi * nc + ic + 1) * c)
        ks = slice(h * dk, (h + 1) * dk)
        vs = slice(h * dv, (h + 1) * dv)
        q = qh[rows, ks]
        k = kh[rows, ks]
        b = bcum[rows, ks]
        v = vv[rows, vs]
        st = states[(bi, h)]
        o = _dot_nt((q * jnp.exp(b)).astype(BF16), st.astype(BF16))
        q3, k3, v3, f3 = g3(q), g3(k), g3(v), g3(fgate[rows, ks])
        o3 = jnp.sum(q3 * k3, axis=-1, keepdims=True) * v3
        e = None
        for d in range(1, SB):
            fd = f3 if d == 1 else pltpu.roll(f3, d - 1, 1)
            e = fd if e is None else e * fd
            m = jnp.where(sub >= d, q3 * pltpu.roll(k3, d, 1) * e, 0.0)
            o3 = o3 + jnp.sum(m, axis=-1, keepdims=True) * pltpu.roll(v3, d, 1)
        o = o + o3.reshape(c, dv)
        if nsb > 1:
            b3 = g3(b)
            rend = jnp.broadcast_to(b3[:, SB - 1:SB, :], b3.shape).reshape(c, dk)
            khat = k * jnp.exp(jnp.minimum(rend - b, 0.0))
            zq = jnp.zeros((c, dk), F32)
            qs, kk = [], []
            for jb in range(nsb - 1):
                lo, hi = jb * SB, (jb + 1) * SB
                rj = b[hi - 1:hi, :]
                qpart = q[hi:, :] * jnp.exp(jnp.minimum(b[hi:, :] - rj, 0.0))
                qs.append(jnp.concatenate([zq[:hi], qpart], axis=0))
                kparts = [khat[lo:hi]] if lo == 0 else [zq[:lo], khat[lo:hi]]
                kk.append(jnp.concatenate(kparts + [zq[hi:]], axis=0))
            amat = _dot_nt(jnp.concatenate(qs, axis=-1).astype(BF16), jnp.concatenate(kk, axis=-1).astype(BF16))
            o = o + _dot(amat.astype(BF16), v.astype(BF16))
        bl = b[c - 1:c, :]
        kdec = k * jnp.exp(bl - b)
        states[(bi, h)] = st * jnp.exp(bl) + _dot_tn(v.astype(BF16), kdec.astype(BF16))
        o_ref[bi, ic * c:(ic + 1) * c, vs] = (_rms(o, nw) * gate[rows, vs]).astype(o_ref.dtype)
    for bi, h in bh:
        st_scr[bi * heads + h] = states[(bi, h)]

    @pl.when(n == nlast)
    def _():
        for bi, h in bh:
            sout_ref[bi, h] = st_scr[bi * heads + h].T


def hgrn(z3, lb_raw, state, norm_w, *, state_layer, c, nc, bb, layer, q_blk, prev, out_dtype):
    b, t, _ = z3.shape
    _, _, heads, dk, dv = state.shape
    depth = lb_raw.shape[0]
    wk = heads * dk
    assert c % SB == 0
    kern = functools.partial(_hgrn_kernel, c=c, nc=nc, bb=bb, heads=heads, dk=dk, dv=dv, layer=layer)
    r = nc * c
    zspec = lambda off: pl.BlockSpec((bb, r, wk), lambda i, n: (i, n, q_blk + off))
    prev_ops, prev_specs, aliases = _alias_plan(prev, 7, (1,))
    return pl.pallas_call(
        _drop_refs(kern, 7, len(prev_ops)),
        grid=(b // bb, t // r),
        in_specs=[
            zspec(0), zspec(1), zspec(2), zspec(3),
            pl.BlockSpec((depth, wk), lambda i, n: (0, 0)),
            pl.BlockSpec((None, bb, heads, dk, dv), lambda i, n: (state_layer, i, 0, 0, 0)),
            pl.BlockSpec((1, dv), lambda i, n: (0, 0)),
        ] + prev_specs,
        out_specs=[
            pl.BlockSpec((bb, r, heads * dv), lambda i, n: (i, n, 0)),
            pl.BlockSpec((None, bb, heads, dk, dv), lambda i, n: (layer, i, 0, 0, 0)),
        ],
        out_shape=[
            jax.ShapeDtypeStruct((b, t, heads * dv), out_dtype),
            jax.ShapeDtypeStruct((depth, b, heads, dk, dv), F32),
        ],
        scratch_shapes=[pltpu.VMEM((bb * heads, dv, dk), F32)],
        input_output_aliases=aliases,
        compiler_params=_cparams(("parallel", "arbitrary")),
        name="hgrn",
    )(z3, z3, z3, z3, lb_raw, state, norm_w.reshape(1, dv), *prev_ops)


def rope_tables(t, pos0, dh, theta):
    rd = dh // 4
    half = rd // 2
    inv = jnp.power(jnp.float32(theta), -jnp.arange(half, dtype=F32) / half)
    ang = (pos0 + jnp.arange(t)).astype(F32)[:, None] * inv
    cos, sin = jnp.cos(ang), jnp.sin(ang)
    d = jnp.arange(128) % dh
    f = d % half
    cc = jnp.where(d < rd, cos[:, f], 1.0)
    s1 = jnp.where(d < half, -sin[:, f], 0.0)
    s2 = jnp.where((d >= half) & (d < rd), sin[:, f], 0.0)
    return cc.astype(F32), s1.astype(F32), s2.astype(F32)


def _prep_kernel(q_ref, k_ref, v_ref, c_ref, s1_ref, s2_ref, q16_ref, k32_ref, k16_ref, v32_ref, v16_ref,
                 *, scale, half, heads, v_transposed):
    tm, w = q_ref.shape[1:]
    hw = w // heads
    reps = w // c_ref.shape[-1]
    cc = jnp.concatenate([c_ref[...]] * reps, axis=-1)
    s1 = jnp.concatenate([s1_ref[...]] * reps, axis=-1)
    s2 = jnp.concatenate([s2_ref[...]] * reps, axis=-1)

    def rope(x):
        return x * cc + pltpu.roll(x, w - half, 1) * s1 + pltpu.roll(x, half, 1) * s2

    q16_ref[0] = (rope(q_ref[0]) * scale).astype(BF16)
    k = rope(k_ref[0])
    k16_ref[0] = k.astype(BF16)
    v = v_ref[0]
    v16_ref[0] = (v.T if v_transposed else v).astype(BF16)
    for h in range(heads):
        k32_ref[0, pl.ds(h, tm, stride=heads), :] = k[:, h * hw:(h + 1) * hw]
        v32_ref[0, pl.ds(h, tm, stride=heads), :] = v[:, h * hw:(h + 1) * hw]


def _drop_refs(kern, start, count):
    def wrapped(*refs):
        return kern(*refs[:start], *refs[start + count:])
    return wrapped


def _alias_plan(prev, n_in, out_idx):
    prev = [] if prev is None else list(prev)
    specs = [pl.BlockSpec(memory_space=pl.ANY)] * len(prev)
    return prev, specs, {n_in + k: out_idx[k] for k in range(len(prev))}


def qkv_prep(z3, tables, *, q_blk, dh, heads, tm, v_transposed, layer, depth, prev):
    b, t, _ = z3.shape
    w = tables[0].shape[-1] * (heads * 2 * dh // tables[0].shape[-1])
    hw = w // heads
    cc, s1, s2 = tables
    kern = functools.partial(_prep_kernel, scale=dh ** -0.5 * math.log2(math.e), half=dh // 8, heads=heads,
                             v_transposed=v_transposed)
    zspec = lambda off: pl.BlockSpec((1, tm, w), lambda i, n: (i, n, q_blk + off))
    tspec = pl.BlockSpec((tm, cc.shape[-1]), lambda i, n: (n, 0))
    ospec = pl.BlockSpec((1, tm, w), lambda i, n: (i, n, 0))
    cspec = pl.BlockSpec((None, 1, tm * heads, hw), lambda i, n: (layer, i, n, 0))
    sd = lambda dt: jax.ShapeDtypeStruct((b, t, w), dt)
    sc = jax.ShapeDtypeStruct((depth, b, t * heads, hw), F32)
    vspec = pl.BlockSpec((1, w, tm), lambda i, n: (i, 0, n)) if v_transposed else ospec
    vsd = jax.ShapeDtypeStruct((b, w, t), BF16) if v_transposed else sd(BF16)
    prev_ops, prev_specs, aliases = _alias_plan(prev, 6, (1, 3))
    return pl.pallas_call(
        _drop_refs(kern, 6, len(prev_ops)),
        grid=(b, t // tm),
        in_specs=[zspec(0), zspec(1), zspec(2), tspec, tspec, tspec] + prev_specs,
        out_specs=[ospec, cspec, ospec, cspec, vspec],
        out_shape=[sd(BF16), sc, sd(BF16), sc, vsd],
        input_output_aliases=aliases,
        compiler_params=_cparams(("parallel", "parallel")),
        name="qkv_prep",
    )(z3, z3, z3, cc, s1, s2, *prev_ops)


def _lambda(lam_ref, lam_init):
    lm = lam_ref[...]
    a = jnp.sum(lm[0:1, :] * lm[1:2, :], axis=-1, keepdims=True)
    b = jnp.sum(lm[2:3, :] * lm[3:4, :], axis=-1, keepdims=True)
    return jnp.exp(a) - jnp.exp(b) + lam_init


ONES_ROWS = 16


def _attn_prompt_kernel(qt_ref, kt_ref, q_ref, k_ref, vt_ref, lam_ref, nw_ref, o_ref,
                        qs_scr, m_scr, acc_scr, *, tq, dh, dv, hpb, qchunk, lam_init):
    p = pl.program_id(2)
    qi = qt_ref[p]
    ki = kt_ref[p]

    @pl.when(ki == 0)
    def _():
        for hh in range(hpb):
            q = q_ref[0, :, hh * dv:(hh + 1) * dv]
            lane = _iota2(q.shape, 1)
            zero = jnp.zeros_like(q)
            qs_scr[hh, 0:tq, :] = jnp.where(lane < dh, q, zero)
            qs_scr[hh, tq:2 * tq, :] = jnp.where(lane >= dh, q, zero)
        m_scr[...] = jnp.full(m_scr.shape, -jnp.inf, F32)
        acc_scr[...] = jnp.zeros_like(acc_scr)

    def update(diag):
        for hh in range(hpb):
            k = k_ref[0, :, hh * dv:(hh + 1) * dv]
            vt = vt_ref[0, hh * dv:(hh + 1) * dv, :]
            vt1 = jnp.concatenate([vt, jnp.ones((ONES_ROWS, vt.shape[1]), BF16)], axis=0)
            for c0 in range(0, 2 * tq, qchunk):
                cs = slice(c0, c0 + qchunk)
                st = _dot_nt(k, qs_scr[hh, cs, :])
                if diag:
                    key = _iota2(st.shape, 0)
                    qry = (_iota2(st.shape, 1) + c0) & (tq - 1)
                    st = jnp.where(key <= qry, st, -jnp.inf)
                m_prev = m_scr[hh, :, cs]
                m_new = jnp.maximum(m_prev, jnp.max(st, axis=0, keepdims=True))
                alpha = jnp.exp2(m_prev - m_new)
                pr = jnp.exp2(st - m_new).astype(BF16)
                acc_scr[hh, :, cs] = alpha * acc_scr[hh, :, cs] + _dot(vt1, pr)
                m_scr[hh, :, cs] = m_new

    @pl.when(ki < qi)
    def _():
        update(False)

    @pl.when(ki == qi)
    def _():
        update(True)
        lam = _lambda(lam_ref, lam_init)
        for hh in range(hpb):
            o1 = acc_scr[hh, 0:dv, 0:tq] / acc_scr[hh, dv:dv + 1, 0:tq]
            o2 = acc_scr[hh, 0:dv, tq:2 * tq] / acc_scr[hh, dv:dv + 1, tq:2 * tq]
            o = (o1 - lam * o2).T
            o_ref[0, :, hh * dv:(hh + 1) * dv] = (_rms(o, nw_ref[...]) * (1.0 - lam_init)).astype(o_ref.dtype)


def attn_prompt(q16, k16, vt16, lam, norm_w, *, tq, dh, lam_init):
    b, t, w = q16.shape
    dv = norm_w.shape[-1]
    heads = w // dv
    nq = t // tq
    pairs = [(i, j) for i in range(nq) for j in range(i + 1)]
    qt = jnp.asarray([pq for pq, _ in pairs], jnp.int32)
    kt = jnp.asarray([pk for _, pk in pairs], jnp.int32)
    hpb = math.gcd(heads, ATTN_HEADS_PER_STEP)
    kern = functools.partial(_attn_prompt_kernel, tq=tq, dh=dh, dv=dv, hpb=hpb, qchunk=min(ATTN_QCHUNK, 2 * tq),
                             lam_init=lam_init)
    grid_spec = pltpu.PrefetchScalarGridSpec(
        num_scalar_prefetch=2,
        grid=(b, heads // hpb, len(pairs)),
        in_specs=[
            pl.BlockSpec((1, tq, hpb * dv), lambda i, h, p, qt, kt: (i, qt[p], h)),
            pl.BlockSpec((1, tq, hpb * dv), lambda i, h, p, qt, kt: (i, kt[p], h)),
            pl.BlockSpec((1, hpb * dv, tq), lambda i, h, p, qt, kt: (i, h, kt[p])),
            pl.BlockSpec(lam.shape, lambda i, h, p, qt, kt: (0, 0)),
            pl.BlockSpec((1, dv), lambda i, h, p, qt, kt: (0, 0)),
        ],
        out_specs=pl.BlockSpec((1, tq, hpb * dv), lambda i, h, p, qt, kt: (i, qt[p], h)),
        scratch_shapes=[
            pltpu.VMEM((hpb, 2 * tq, dv), BF16),
            pltpu.VMEM((hpb, 1, 2 * tq), F32),
            pltpu.VMEM((hpb, dv + ONES_ROWS, 2 * tq), F32),
        ],
    )
    return pl.pallas_call(
        kern,
        grid_spec=grid_spec,
        out_shape=jax.ShapeDtypeStruct((b, t, w), BF16),
        compiler_params=_cparams(("parallel", "parallel", "arbitrary")),
        name="attn_prompt",
    )(qt, kt, q16, k16, vt16, lam, norm_w.reshape(1, dv))


def _attn_sample_kernel(pt_ref, q_ref, kc_ref, vc_ref, lam_ref, nw_ref, *rest,
                        npages, bpb, tq, heads, dh, dv, page, lam_init):
    k_refs = rest[:bpb * npages]
    v_refs = rest[bpb * npages:2 * bpb * npages]
    o_ref = rest[2 * bpb * npages]
    lam = _lambda(lam_ref, lam_init)
    nw = nw_ref[...]
    lane = _iota2((tq, dv), 1)
    qidx = _iota2((2 * tq, tq), 0) % tq
    kidx = _iota2((2 * tq, tq), 1)
    hsl = [slice(h * dv, (h + 1) * dv) for h in range(heads)]
    inst = [(bi, h) for bi in range(bpb) for h in range(heads)]

    def head_rows(refs, bi, h):
        return jnp.concatenate([r[0, 0, pl.ds(h, page, stride=heads), :].astype(BF16)
                                for r in refs[bi * npages:(bi + 1) * npages]], axis=0)

    qs = [q_ref[bi].astype(F32) for bi in range(bpb)]
    qrows = [jnp.concatenate([jnp.where(lane < dh, qs[bi][:, hsl[h]], 0.0), jnp.where(lane >= dh, qs[bi][:, hsl[h]], 0.0)],
                             axis=0).astype(BF16) for bi, h in inst]
    s_past = [_dot_nt(qr, head_rows(k_refs, bi, h)) for qr, (bi, h) in zip(qrows, inst)]
    s_cur = [jnp.where(kidx <= qidx, _dot_nt(qr, kc_ref[bi, :, hsl[h]]), -jnp.inf) for qr, (bi, h) in zip(qrows, inst)]
    ms = [jnp.maximum(jnp.max(sp, axis=-1, keepdims=True), jnp.max(sc, axis=-1, keepdims=True))
          for sp, sc in zip(s_past, s_cur)]
    p_past = [jnp.exp2(sp - m) for sp, m in zip(s_past, ms)]
    p_cur = [jnp.exp2(sc - m) for sc, m in zip(s_cur, ms)]
    invs = [1.0 / (jnp.sum(pp, axis=-1, keepdims=True) + jnp.sum(pc, axis=-1, keepdims=True))
            for pp, pc in zip(p_past, p_cur)]

    def diff(pp, inv):
        pn = pp * inv
        return pn[0:tq] - lam * pn[tq:2 * tq]

    for j, (bi, h) in enumerate(inst):
        o = (_dot(diff(p_past[j], invs[j]).astype(BF16), head_rows(v_refs, bi, h))
             + _dot(diff(p_cur[j], invs[j]), vc_ref[bi, :, hsl[h]].astype(F32)))
        o_ref[bi, :, hsl[h]] = _rms(o, nw) * (1.0 - lam_init)


def attn_sample(q16, k16, v16, cache_k, cache_v, page_table, lam, norm_w, *, layer, heads, dh, lam_init):
    b, tq, w = q16.shape
    dv = norm_w.shape[-1]
    npages = page_table.shape[1]
    prow = cache_k.shape[2]
    bpb = math.gcd(b, SAMPLE_ROWS_PER_STEP)
    kern = functools.partial(_attn_sample_kernel, npages=npages, bpb=bpb, tq=tq, heads=heads, dh=dh, dv=dv,
                             page=prow // heads, lam_init=lam_init)
    cur = pl.BlockSpec((bpb, tq, w), lambda i, pt: (i, 0, 0))

    def page_spec(bi, j):
        return pl.BlockSpec((1, 1, prow, dv), lambda i, pt: (layer, pt[i * bpb + bi, j], 0, 0))

    pages = [page_spec(bi, j) for bi in range(bpb) for j in range(npages)]
    grid_spec = pltpu.PrefetchScalarGridSpec(
        num_scalar_prefetch=1,
        grid=(b // bpb,),
        in_specs=[cur, cur, cur,
                  pl.BlockSpec(lam.shape, lambda i, pt: (0, 0)),
                  pl.BlockSpec((1, dv), lambda i, pt: (0, 0))]
        + pages * 2,
        out_specs=pl.BlockSpec((bpb, tq, w), lambda i, pt: (i, 0, 0)),
    )
    return pl.pallas_call(
        kern,
        grid_spec=grid_spec,
        out_shape=jax.ShapeDtypeStruct((b, tq, w), F32),
        compiler_params=_cparams(("parallel",)),
        name="attn_sample",
    )(page_table, q16, k16, v16, lam, norm_w.reshape(1, dv), *([cache_k] * len(pages)), *([cache_v] * len(pages)))


def _merge_ffn_kernel(x_ref, oa_ref, ob_ref, oc_ref, g0_ref, g1_ref, g2_ref, wb_ref, wo_ref, mpost_ref,
                      pre_ref, wi_ref, wo2_ref, post_ref, o_ref, a_scr, *, tf):
    y = None
    for i, (o_i, g_i) in enumerate(((oa_ref, g0_ref), (ob_ref, g1_ref), (oc_ref, g2_ref))):
        ys = _dot(o_i[...].astype(BF16), wb_ref[i])
        t = _sigmoid(g_i[...]) * ys
        y = t if y is None else y + t
    y2 = _dot(y.astype(BF16), wo_ref[...])
    x1 = x_ref[...] + _rms(y2, mpost_ref[...])
    o_ref[...] = _ffn_body(x1, pre_ref, wi_ref, wo2_ref, post_ref, a_scr, tf)


def merge_ffn(x, o_a, o_b, o_c, z, w_branch_bf, w_out_bf, mpost_w, pre_w, w_in_bf, w_out2_bf, post_w, *,
              layer, tm, tf, gate_blk):
    n, d = x.shape
    bw = o_a.shape[-1]
    nb = w_branch_bf.shape[1]
    dff = w_out2_bf.shape[1]
    resident = dict(pipeline_mode=pl.Buffered(1))
    ospec = pl.BlockSpec((tm, bw), lambda i: (i, 0))
    gspec = lambda k: pl.BlockSpec((tm, d), lambda i: (i, gate_blk + k))
    vec = pl.BlockSpec((1, d), lambda i: (0, 0))
    return pl.pallas_call(
        functools.partial(_merge_ffn_kernel, tf=tf),
        grid=(n // tm,),
        in_specs=[
            pl.BlockSpec((tm, d), lambda i: (i, 0)),
            ospec, ospec, ospec, gspec(0), gspec(1), gspec(2),
            pl.BlockSpec((None, nb, bw, d), lambda i: (layer, 0, 0, 0), **resident),
            pl.BlockSpec((None, d, d), lambda i: (layer, 0, 0), **resident),
            vec, vec,
            pl.BlockSpec((None, d, 2 * dff), lambda i: (layer, 0, 0), **resident),
            pl.BlockSpec((None, dff, d), lambda i: (layer, 0, 0), **resident),
            vec,
        ],
        out_specs=pl.BlockSpec((tm, d), lambda i: (i, 0)),
        out_shape=jax.ShapeDtypeStruct((n, d), F32),
        scratch_shapes=[pltpu.VMEM((tm, dff), BF16)],
        compiler_params=_cparams(("parallel",)),
        name="merge_ffn",
    )(x, o_a, o_b, o_c, z, z, z, w_branch_bf, w_out_bf, mpost_w.reshape(1, d), pre_w.reshape(1, d), w_in_bf, w_out2_bf,
      post_w.reshape(1, d))


ROPE_THETA = 500000.0
SCAN_CHUNK = 64
GDN_CHUNKS_PER_STEP = 4
HGRN_CHUNKS_PER_STEP = 4
HGRN_INSTANCES = 32
GDN_INSTANCES = 32
ROW_TILE = 1024
FFN_ROW_TILE = 512
FF_TILE = 256
PROJ_COL_TILES = 4
ATTN_TILE = 512
ATTN_QCHUNK = 1024
ATTN_HEADS_PER_STEP = 4
PREP_TILE = 512
SAMPLE_ROWS_PER_STEP = 2
SMALL_COLS = 128


def _split_w_in(w_in, sizes):
    offs = [0]
    for s in sizes:
        offs.append(offs[-1] + s)
    seg = lambda i: w_in[:, :, offs[i]:offs[i + 1]]
    main = jnp.concatenate([seg(11), seg(0), seg(1), seg(4), seg(5), seg(6), seg(7), seg(8), seg(9), seg(10)], axis=2)
    small = jnp.concatenate([seg(2), seg(3)], axis=2)
    small = jnp.pad(small, ((0, 0), (0, 0), (0, SMALL_COLS - small.shape[2])))
    return main.astype(BF16), small.astype(BF16)


def _row_tile(n, pref):
    return pref if n % pref == 0 else n


def kernel(x_prompt, x_sample, state_gdn, state_gdn_conv, state_hgrn, cache_k, cache_v, page_table,
           ffn1_norm_pre, ffn1_norm_post, ffn1_w_in, ffn1_w_out, mix_norm_pre, mix_norm_post, w_in,
           gdn_conv_w, gdn_a_log, gdn_dt_bias, gdn_norm_w, hgrn_lb_raw, hgrn_norm_w, diff_lambda,
           diff_norm_w, w_branch, w_out, ffn2_norm_pre, ffn2_norm_post, ffn2_w_in, ffn2_w_out):
    depth = w_in.shape[0]
    bp, tp, d = x_prompt.shape
    bs, ts, _ = x_sample.shape
    _, _, gh, gdk, gdv = state_gdn.shape
    cch = state_gdn_conv.shape[-1]
    _, _, hh, hdk, hdv = state_hgrn.shape
    _, n_pool, page, ah, adh2 = cache_k.shape
    adh = adh2 // 2
    adv = cache_v.shape[-1]
    aw = ah * adv
    assert ah * adh2 == aw and hh * hdk == aw and hh * hdv == aw and gh * gdv == aw and cch == 3 * aw and d == 2 * aw
    sizes = (cch, gh * gdv, gh, gh, hh * hdk, hh * hdk, hh * hdv, hh * hdv, ah * adh2, ah * adh2, aw, 3 * d)
    gate_blk, qkv_blk, za_blk, hq_blk, aq_blk = 0, (3 * d) // cch, (3 * d + cch) // aw, (3 * d + cch) // aw + 1, (3 * d + cch) // aw + 5
    past_len = page_table.shape[1] * page
    ck = cache_k.reshape(depth, n_pool, page * ah, adh2)
    cv = cache_v.reshape(depth, n_pool, page * ah, adv)
    tabs_p = rope_tables(tp, 0, adh, ROPE_THETA)
    tabs_s = tuple(jnp.tile(a, (bs, 1)) for a in rope_tables(ts, past_len, adh, ROPE_THETA))
    assert gdn_conv_w.shape[1] == CONV_TAPS and state_gdn_conv.shape[2] == CONV_TAPS - 1
    zeros_conv = jnp.zeros((1, bp, CONV_TAPS - 1, cch), F32)
    zeros_gdn = jnp.zeros((1, bp, gh, gdk, gdv), F32)
    zeros_hgrn = jnp.zeros((1, bp, hh, hdk, hdv), F32)

    f1_in, f1_out = ffn1_w_in.astype(BF16), ffn1_w_out.astype(BF16)
    f2_in, f2_out = ffn2_w_in.astype(BF16), ffn2_w_out.astype(BF16)
    wb, wo = w_branch.astype(BF16), w_out.astype(BF16)

    def run_layer(l, x, b, t, conv_state, gdn_state, hgrn_state, sl, tabs, prompt, w_main, w_small, acc):
        n = b * t
        tm = _row_tile(n, ROW_TILE)
        lam_init = 0.8 - 0.6 * math.exp(-0.3 * l)
        tmf = _row_tile(n, FFN_ROW_TILE)
        x = ffn(x, ffn1_norm_pre[l], f1_in, f1_out, ffn1_norm_post[l], layer=l, tm=tmf, tf=FF_TILE)
        z, zs = proj(x, mix_norm_pre[l], w_main, w_small, layer=l, tm=tm, tn=w_main.shape[2] // PROJ_COL_TILES)
        z3 = z.reshape(b, t, z.shape[-1])
        zs3 = zs.reshape(b, t, SMALL_COLS)
        c = math.gcd(SCAN_CHUNK, t)
        branch_dtype = BF16 if c % 16 == 0 else F32
        o_a, new_gdn, new_conv = gdn(z3, zs3, conv_state, gdn_state, gdn_conv_w[l], gdn_a_log[l], gdn_dt_bias[l],
                                     gdn_norm_w[l], state_layer=sl, c=c, nc=min(GDN_CHUNKS_PER_STEP, t // c),
                                     bb=math.gcd(b, max(1, GDN_INSTANCES // (gh * min(GDN_CHUNKS_PER_STEP, t // c)))),
                                     qkv_blk=qkv_blk, za_blk=za_blk, layer=l, depth=depth,
                                     prev=acc[0:1], out_dtype=branch_dtype)
        o_b, new_hgrn = hgrn(z3, hgrn_lb_raw, hgrn_state, hgrn_norm_w[l], state_layer=sl, c=c,
                             nc=min(HGRN_CHUNKS_PER_STEP, t // c),
                             bb=math.gcd(b, max(1, HGRN_INSTANCES // (hh * min(HGRN_CHUNKS_PER_STEP, t // c)))),
                             layer=l, q_blk=hq_blk,
                             prev=acc[1:2], out_dtype=branch_dtype)
        if prompt:
            q16, k32, k16, v32, v16 = qkv_prep(z3, tabs, q_blk=aq_blk, dh=adh, heads=ah, tm=_row_tile(t, PREP_TILE),
                                               v_transposed=True, layer=l, depth=depth,
                                               prev=acc[2:4])
        else:
            q16, k32, k16, v32, v16 = qkv_prep(z.reshape(1, n, z.shape[-1]), tabs, q_blk=aq_blk, dh=adh, heads=ah,
                                               tm=_row_tile(n, PREP_TILE), v_transposed=False, layer=l, depth=depth,
                                               prev=acc[2:4])
            q16, k16, v16 = (a.reshape(b, t, aw) for a in (q16, k16, v16))
        if prompt:
            o_c = attn_prompt(q16, k16, v16, diff_lambda[l], diff_norm_w[l], tq=_row_tile(t, ATTN_TILE), dh=adh,
                              lam_init=lam_init)
        else:
            o_c = attn_sample(q16, k16, v16, ck, cv, page_table, diff_lambda[l], diff_norm_w[l], layer=l, heads=ah, dh=adh,
                              lam_init=lam_init)
        x = merge_ffn(x, o_a.reshape(n, aw), o_b.reshape(n, aw), o_c.reshape(n, aw), z, wb, wo, mix_norm_post[l],
                      ffn2_norm_pre[l], f2_in, f2_out, ffn2_norm_post[l], layer=l, tm=tmf, tf=FF_TILE, gate_blk=gate_blk)
        return x, new_conv, (new_gdn, new_hgrn, k32, v32)

    xp = x_prompt.reshape(bp * tp, d)
    xs = x_sample.reshape(bs * ts, d)
    conv_p, conv_s = [], []

    def stacked_outputs(b, kv_b, kv_t):
        return (jnp.zeros((depth, b, gh, gdk, gdv), F32), jnp.zeros((depth, b, hh, hdk, hdv), F32),
                jnp.zeros((depth, kv_b, kv_t * ah, adh2), F32), jnp.zeros((depth, kv_b, kv_t * ah, adv), F32))

    acc_p, acc_s = stacked_outputs(bp, bp, tp), stacked_outputs(bs, 1, bs * ts)
    w_main, w_small = _split_w_in(w_in, sizes)
    for l in range(depth):
        xp, cp, acc_p = run_layer(l, xp, bp, tp, zeros_conv, zeros_gdn, zeros_hgrn, 0, tabs_p, True, w_main, w_small, acc_p)
        conv_p.append(cp)
        xs, cs, acc_s = run_layer(l, xs, bs, ts, state_gdn_conv, state_gdn, state_hgrn, l, tabs_s, False, w_main, w_small,
                                  acc_s)
        conv_s.append(cs)

    def finish(x, b, t, convs, acc):
        new_gdn, new_hgrn, k32, v32 = acc
        return (x.reshape(b, t, d), new_gdn, jnp.stack(convs), new_hgrn,
                k32.reshape(depth, b, t, ah, adh2), v32.reshape(depth, b, t, ah, adv))

    yp, p_gdn, p_conv, p_hgrn, p_k, p_v = finish(xp, bp, tp, conv_p, acc_p)
    ys, s_gdn, s_conv, s_hgrn, s_k, s_v = finish(xs, bs, ts, conv_s, acc_s)
    return (yp, ys, p_gdn, p_conv, p_hgrn, p_k, p_v, s_gdn, s_conv, s_hgrn, s_k, s_v)
```
